```python
import jax, jax.numpy as jnp
from jax import lax
import numpy as np

D_MODEL = 1024
BATCH = 8
SEQ = 2048
DEPTH = 1
DEC_BATCH = 8
DEC_SEQ = 32
PAST_LEN = 2048

CHUNK = 64
H_A = 8
DK_A = 128
DV_A = D_MODEL // H_A
HGRN_BLOCK = 16
H_B = 8
DH_B = 128
H_I = 8
D_I = 64
TOPK_MAX = 256
Q_BLOCK = 128
INDEX_SCALE = (H_I * D_I) ** -0.5
N_EXPERTS = 32
TOP_K = 4
D_FF = D_MODEL
SWIGLU_ALPHA = 1.702
SWIGLU_LIMIT = 7.0
MOE_BLOCK = 128
DN_ALPHA = (2 * DEPTH) ** 0.25
DN_BETA = (8 * DEPTH) ** -0.25
EPS = 1e-5

_SIZES = (H_A * DK_A, H_A * DK_A, H_A * DV_A, H_A * DV_A,
          H_B * DH_B, DH_B, DH_B,
          H_I * D_I, H_I, D_I,
          D_MODEL, D_MODEL)
N_IN = int(sum(_SIZES))
SPLIT_POINTS = tuple(int(s) for s in np.cumsum(_SIZES)[:-1])

kernel_name = 'hgrn2_dsa_moe_streaming_step'

F32 = jnp.float32


def layer_norm(x, g, b):
    xf = x.astype(F32)
    mu = jnp.mean(xf, axis=-1, keepdims=True)
    var = jnp.mean(jnp.square(xf - mu), axis=-1, keepdims=True)
    return ((xf - mu) * lax.rsqrt(var + EPS) * g.astype(F32) + b.astype(F32)).astype(x.dtype)


def hgrn2_recurrence(q, k, v, log_f, s0):
    bsz, L = q.shape[0], q.shape[1]
    n_blk = -(-L // HGRN_BLOCK)
    pad = n_blk * HGRN_BLOCK - L

    def blocks(a):
        a = jnp.pad(a, ((0, 0), (0, pad), (0, 0), (0, 0)))
        a = a.reshape(bsz, n_blk, HGRN_BLOCK, a.shape[2], a.shape[3])
        return a.transpose(1, 0, 3, 2, 4)

    tri = jnp.tril(jnp.ones((HGRN_BLOCK, HGRN_BLOCK), bool))
    ref = HGRN_BLOCK // 2

    def step(S, xs):
        qb, kb, vb, lfb = xs
        cum = jnp.cumsum(lfb, axis=2)
        last = cum[:, :, -1:, :]
        mid = cum[:, :, ref:ref + 1, :]
        o_inter = jnp.einsum('bhtk,bhkv->bhtv', qb * jnp.exp(cum), S)
        att = jnp.einsum('bhtk,bhsk->bhts', qb * jnp.exp(cum - mid), kb * jnp.exp(mid - cum))
        att = jnp.where(tri, att, 0.0)
        o = o_inter + jnp.einsum('bhts,bhsv->bhtv', att, vb)
        S = jnp.exp(last)[:, :, 0, :, None] * S + jnp.einsum('bhsk,bhsv->bhkv', kb * jnp.exp(last - cum), vb)
        return S, o

    s_fin, o = lax.scan(step, s0, (blocks(q), blocks(k), blocks(v), blocks(log_f)))
    o = o.transpose(1, 0, 3, 2, 4).reshape(bsz, n_blk * HGRN_BLOCK, q.shape[2], v.shape[3])[:, :L]
    return o, s_fin


def dsa_select_attend(q, qi, wi, limit, k_all, v_all, ki_all, n_sel):
    n_keys = k_all.shape[1]
    visible = jnp.arange(n_keys)[None, :] < limit[:, None]
    s = jnp.einsum('bthd,bsd->bths', qi.astype(F32), ki_all.astype(F32))
    score = jnp.einsum('bths,bth->bts', jax.nn.relu(s), wi.astype(F32))
    score = jnp.where(visible[None], score, -jnp.inf)
    _, idx = lax.top_k(score, n_sel)
    valid = idx < limit[None, :, None]
    gather = jax.vmap(lambda a, i: a[i])
    k_sel = gather(k_all, idx).astype(F32)
    v_sel = gather(v_all, idx).astype(F32)
    logits = jnp.einsum('bthd,btnd->bthn', q.astype(F32), k_sel) * (DH_B ** -0.5)
    logits = jnp.where(valid[:, :, None, :], logits, -jnp.inf)
    p = jax.nn.softmax(logits, axis=-1)
    o = jnp.einsum('bthn,btnd->bthd', p, v_sel)
    return o.reshape(o.shape[0], o.shape[1], -1).astype(q.dtype)


def moe(x, w_router, b_router, w_up, b_up, w_down, b_down):
    bsz, L, D = x.shape
    xt = x.reshape(-1, D)
    n_tok = xt.shape[0]
    logits = (xt @ w_router).astype(F32) + b_router.astype(F32)
    top_val, top_idx = lax.top_k(logits, TOP_K)
    gate = jax.nn.softmax(top_val, axis=-1)
    n_asg = n_tok * TOP_K
    flat_e = top_idx.reshape(-1)
    order = jnp.argsort(flat_e)
    e_sorted = flat_e[order]
    tok_sorted = order // TOP_K
    g_sorted = gate.reshape(-1)[order]
    counts = jnp.bincount(flat_e, length=N_EXPERTS)
    padded = (counts + MOE_BLOCK - 1) // MOE_BLOCK * MOE_BLOCK
    start = jnp.cumsum(counts) - counts
    pend = jnp.cumsum(padded)
    pstart = pend - padded
    dest = pstart[e_sorted] + jnp.arange(n_asg) - start[e_sorted]
    n_blocks = -(-(n_asg + N_EXPERTS * (MOE_BLOCK - 1)) // MOE_BLOCK)
    rows = jnp.zeros((n_blocks * MOE_BLOCK, D), x.dtype).at[dest].set(xt[tok_sorted])
    blk_e = jnp.minimum(jnp.searchsorted(pend, jnp.arange(n_blocks) * MOE_BLOCK, side='right'), N_EXPERTS - 1)

    def expert_block(args):
        xb, e = args
        h = xb @ w_up[e] + b_up[e]
        glu, lin = h[:, :D_FF], h[:, D_FF:]
        glu = jnp.minimum(glu, SWIGLU_LIMIT)
        lin = jnp.clip(lin, -SWIGLU_LIMIT, SWIGLU_LIMIT)
        a = glu * jax.nn.sigmoid(SWIGLU_ALPHA * glu) * (lin + 1)
        return a @ w_down[e] + b_down[e]

    out = lax.map(expert_block, (rows.reshape(n_blocks, MOE_BLOCK, D), blk_e))
    y_sorted = out.reshape(-1, D)[dest].astype(F32) * g_sorted[:, None]
    y = jnp.zeros((n_tok, D), F32).at[tok_sorted].add(y_sorted)
    return y.reshape(bsz, L, D).astype(x.dtype)


def trunk_layer(x, past, lb, w_in, gn_a, w_oa, w_ob, w_out, ln1_g, ln1_b,
                w_router, b_router, w_up, b_up, w_down, b_down, ln2_g, ln2_b):
    bsz, L, _ = x.shape
    (q_a, f_a, i_a, g_a, q_b, k_b, v_b, qi, wi, ki, gt_a, gt_b) = jnp.split(x @ w_in, SPLIT_POINTS, axis=-1)
    heads = lambda a, h: a.reshape(bsz, L, h, -1)

    f = lb + (1.0 - lb) * jax.nn.sigmoid(f_a.astype(F32))
    s0 = jnp.zeros((bsz, H_A, DK_A, DV_A), F32) if past is None else past[3].astype(F32)
    o_a, s_new = hgrn2_recurrence(heads(q_a.astype(F32), H_A), heads(1.0 - f, H_A),
                                  heads(i_a.astype(F32), H_A), heads(jnp.log(f), H_A), s0)
    o_a = o_a * lax.rsqrt(jnp.mean(jnp.square(o_a), axis=-1, keepdims=True) + EPS) * gn_a.astype(F32)
    o_a = o_a * jax.nn.silu(heads(g_a, H_A).astype(F32))
    y_a = o_a.reshape(bsz, L, -1).astype(x.dtype) @ w_oa

    q_b = heads(q_b, H_B)
    qi = heads(qi, H_I)
    wi = wi * INDEX_SCALE
    if past is None:
        n_sel = min(TOPK_MAX, L // 4)
        n_q = L // Q_BLOCK
        limits = ((jnp.arange(L) // CHUNK + 1) * CHUNK).reshape(n_q, Q_BLOCK)
        blk = lambda a: jnp.moveaxis(a.reshape(bsz, n_q, Q_BLOCK, *a.shape[2:]), 1, 0)
        o_b = lax.map(lambda xs: dsa_select_attend(xs[0], xs[1], xs[2], xs[3], k_b, v_b, ki, n_sel),
                      (blk(q_b), blk(qi), blk(wi), limits))
        o_b = jnp.moveaxis(o_b, 0, 1).reshape(bsz, L, -1)
    else:
        k_all = jnp.concatenate([past[0].astype(k_b.dtype), k_b], axis=1)
        v_all = jnp.concatenate([past[1].astype(v_b.dtype), v_b], axis=1)
        ki_all = jnp.concatenate([past[2].astype(ki.dtype), ki], axis=1)
        n_keys = k_all.shape[1]
        n_sel = min(TOPK_MAX, n_keys // 4)
        limit = jnp.full((L,), n_keys, jnp.int32)
        o_b = dsa_select_attend(q_b, qi, wi, limit, k_all, v_all, ki_all, n_sel)
    y_b = o_b @ w_ob

    mixed = (jax.nn.sigmoid(gt_a) * y_a + jax.nn.sigmoid(gt_b) * y_b) @ w_out
    x = layer_norm(DN_ALPHA * x + mixed, ln1_g, ln1_b)
    x = layer_norm(DN_ALPHA * x + moe(x, w_router, b_router, w_up, b_up, w_down, b_down), ln2_g, ln2_b)
    return x, (k_b, v_b, ki, s_new.astype(x.dtype))


def setup_inputs(seed: int = 0) -> dict:
    key = jax.random.key(seed)
    ks = jax.random.split(key, 24)
    nrm = lambda k, shape, s: jax.random.normal(k, shape, jnp.float32) * s
    return {
        'x_prompt': nrm(ks[0], (BATCH, SEQ, D_MODEL), 1.0),
        'x_sample': nrm(ks[1], (DEC_BATCH, DEC_SEQ, D_MODEL), 1.0),
        'cache_k': nrm(ks[2], (DEPTH, DEC_BATCH, PAST_LEN, DH_B), 1.0),
        'cache_v': nrm(ks[3], (DEPTH, DEC_BATCH, PAST_LEN, DH_B), 1.0),
        'cache_kidx': nrm(ks[4], (DEPTH, DEC_BATCH, PAST_LEN, D_I), 1.0),
        'state_hgrn': nrm(ks[5], (DEPTH, DEC_BATCH, H_A, DK_A, DV_A), 0.5),
        'w_in': nrm(ks[6], (DEPTH, D_MODEL, N_IN), D_MODEL ** -0.5),
        'lb_logits': nrm(ks[7], (DEPTH + 1, H_A * DK_A), 0.1),
        'gn_a': 1.0 + nrm(ks[8], (DEPTH, DV_A), 0.01),
        'w_oa': nrm(ks[9], (DEPTH, H_A * DV_A, D_MODEL), (H_A * DV_A) ** -0.5),
        'w_ob': nrm(ks[10], (DEPTH, H_B * DH_B, D_MODEL), (H_B * DH_B) ** -0.5),
        'w_out': nrm(ks[11], (DEPTH, D_MODEL, D_MODEL), D_MODEL ** -0.5 * DN_BETA),
        'ln1_g': 1.0 + nrm(ks[12], (DEPTH, D_MODEL), 0.01),
        'ln1_b': nrm(ks[13], (DEPTH, D_MODEL), 0.01),
        'w_router': nrm(ks[14], (DEPTH, D_MODEL, N_EXPERTS), D_MODEL ** -0.5),
        'b_router': nrm(ks[15], (DEPTH, N_EXPERTS), 0.01),
        'w_up': nrm(ks[16], (DEPTH, N_EXPERTS, D_MODEL, 2 * D_FF), D_MODEL ** -0.5),
        'b_up': nrm(ks[17], (DEPTH, N_EXPERTS, 2 * D_FF), 0.01),
        'w_down': nrm(ks[18], (DEPTH, N_EXPERTS, D_FF, D_MODEL), D_FF ** -0.5 * DN_BETA),
        'b_down': nrm(ks[19], (DEPTH, N_EXPERTS, D_MODEL), 0.01),
        'ln2_g': 1.0 + nrm(ks[20], (DEPTH, D_MODEL), 0.01),
        'ln2_b': nrm(ks[21], (DEPTH, D_MODEL), 0.01),
    }


def reference(x_prompt, x_sample, cache_k, cache_v, cache_kidx, state_hgrn,
              w_in, lb_logits, gn_a, w_oa, w_ob, w_out, ln1_g, ln1_b,
              w_router, b_router, w_up, b_up, w_down, b_down, ln2_g, ln2_b):
    lb_all = jnp.cumsum(jax.nn.softmax(lb_logits.astype(F32), axis=0), axis=0)
    hp, hs = x_prompt, x_sample
    kp, vp, kip, sp = [], [], [], []
    kd, vd, kid, sd = [], [], [], []
    for l in range(DEPTH):
        w = (w_in[l], gn_a[l], w_oa[l], w_ob[l], w_out[l], ln1_g[l], ln1_b[l],
             w_router[l], b_router[l], w_up[l], b_up[l], w_down[l], b_down[l], ln2_g[l], ln2_b[l])
        hp, st_p = trunk_layer(hp, None, lb_all[l], *w)
        hs, st_s = trunk_layer(hs, (cache_k[l], cache_v[l], cache_kidx[l], state_hgrn[l]), lb_all[l], *w)
        kp.append(st_p[0]); vp.append(st_p[1]); kip.append(st_p[2]); sp.append(st_p[3])
        kd.append(st_s[0]); vd.append(st_s[1]); kid.append(st_s[2]); sd.append(st_s[3])
    k_prompt = jnp.stack(kp)
    v_prompt = jnp.stack(vp)
    kidx_prompt = jnp.stack(kip)
    hgrn_prompt = jnp.stack(sp)
    k_sample = jnp.stack(kd)
    v_sample = jnp.stack(vd)
    kidx_sample = jnp.stack(kid)
    hgrn_sample = jnp.stack(sd)
    return (hp, hs, k_prompt, v_prompt, kidx_prompt, hgrn_prompt, k_sample, v_sample, kidx_sample, hgrn_sample)
```

```python
import functools

import jax
import jax.numpy as jnp
import numpy as np
from jax import lax
from jax.experimental import pallas as pl
from jax.experimental.pallas import tpu as pltpu

F32 = jnp.float32
BF16 = jnp.bfloat16
I32 = jnp.int32

CHUNK = 64
H_A = 8
DK_A = 128
DV_A = 128
HGRN_BLOCK = 16
H_B = 8
DH_B = 128
H_I = 8
D_I = 64
TOPK_MAX = 256
N_EXPERTS = 32
TOP_K = 4
SWIGLU_ALPHA = 1.702
SWIGLU_LIMIT = 7.0
EPS = 1e-5

LANES = 128
SUBLANES = 8
VMEM_LIMIT_BYTES = 56 * 1024 * 1024

INT_MIN = -(2 ** 31)


def _cparams(sem):
    return pltpu.CompilerParams(dimension_semantics=sem, vmem_limit_bytes=VMEM_LIMIT_BYTES)


def _sigmoid(x):
    return 1.0 / (1.0 + jnp.exp(-x))


def _full_spec(shape):
    nd = len(shape)
    return pl.BlockSpec(shape, lambda *_: (0,) * nd)


def _layer_norm(x, g, b):
    mu = jnp.mean(x, axis=-1, keepdims=True)
    xc = x - mu
    var = jnp.mean(xc * xc, axis=-1, keepdims=True)
    return xc * lax.rsqrt(var + EPS) * g + b


def _proj_kernel(x_ref, wqa_ref, wfa_ref, wia_ref, wga_ref, wqb_ref, wsm_ref, lbl_ref,
                 qa_ref, f_ref, ia_ref, sg_ref, qb_ref, sm_ref):
    xb = x_ref[...].astype(BF16)
    dot = lambda w_ref: jnp.dot(xb, w_ref[...], preferred_element_type=F32)
    qa_ref[...] = dot(wqa_ref)
    lbl = lbl_ref[...]
    lbe = jnp.exp(lbl - jnp.max(lbl, axis=0, keepdims=True))
    lb = lbe[0:1, :] / jnp.sum(lbe, axis=0, keepdims=True)
    f_ref[...] = lb + (1.0 - lb) * _sigmoid(dot(wfa_ref))
    ia_ref[...] = dot(wia_ref)
    g = dot(wga_ref)
    sg_ref[...] = g * _sigmoid(g)
    qb_ref[...] = dot(wqb_ref).astype(BF16)
    sm_ref[...] = dot(wsm_ref)


def _proj(x, wqa, wfa, wia, wga, wqb, wsm, lb_logits, tm):
    n, d = x.shape
    nsm = wsm.shape[1]
    row = lambda w: pl.BlockSpec((tm, w), lambda i: (i, 0))
    wspec = lambda w: pl.BlockSpec(w.shape, lambda i: (0, 0))
    return pl.pallas_call(
        _proj_kernel,
        grid=(n // tm,),
        in_specs=[row(d), wspec(wqa), wspec(wfa), wspec(wia), wspec(wga), wspec(wqb), wspec(wsm),
                  wspec(lb_logits)],
        out_specs=[row(d), row(d), row(d), row(d), row(d), row(nsm)],
        out_shape=[jax.ShapeDtypeStruct((n, d), F32)] * 4
        + [jax.ShapeDtypeStruct((n, d), BF16), jax.ShapeDtypeStruct((n, nsm), F32)],
        compiler_params=_cparams(("arbitrary",)),
        name="proj",
    )(x, wqa, wfa, wia, wga, wqb, wsm, lb_logits)


def _block_cumsum(a, row_in_block):
    s = 1
    while s < HGRN_BLOCK:
        a = a + jnp.where(row_in_block >= s, pltpu.roll(a, s, 0), 0.0)
        s *= 2
    return a


def _hgrn_kernel(qa_ref, f_ref, ia_ref, sg_ref, gn_ref, s0_ref, o_ref, sfin_ref,
                 s_scr, qd_scr, qm_scr, km_scr, kl_scr, dl_scr, o_scr):
    j = pl.program_id(1)
    tl = qa_ref.shape[0]
    nblk = tl // HGRN_BLOCK
    half = HGRN_BLOCK // 2

    @pl.when(j == 0)
    def _():
        for h in range(H_A):
            s_scr[h] = s0_ref[0, h].T

    f = f_ref[...]
    q = qa_ref[...]
    d = f.shape[1]
    row_in_block = lax.broadcasted_iota(I32, (tl, d), 0) % HGRN_BLOCK
    cum = _block_cumsum(jnp.log(f), row_in_block)
    cum3 = cum.reshape(nblk, HGRN_BLOCK, d)
    mid = jnp.broadcast_to(cum3[:, half:half + 1, :], cum3.shape).reshape(tl, d)
    last3 = cum3[:, HGRN_BLOCK - 1:HGRN_BLOCK, :]
    last = jnp.broadcast_to(last3, cum3.shape).reshape(tl, d)
    k = 1.0 - f
    qd_scr[...] = (q * jnp.exp(cum)).astype(BF16)
    qm_scr[...] = (q * jnp.exp(cum - mid)).astype(BF16)
    km_scr[...] = (k * jnp.exp(mid - cum)).astype(BF16)
    kl_scr[...] = (k * jnp.exp(last - cum)).astype(BF16)
    dl_scr[...] = jnp.exp(last3.reshape(nblk, d))

    tri = (lax.broadcasted_iota(I32, (HGRN_BLOCK, HGRN_BLOCK), 0)
           >= lax.broadcasted_iota(I32, (HGRN_BLOCK, HGRN_BLOCK), 1))

    def block(b, carry):
        r0 = pl.multiple_of(b * HGRN_BLOCK, HGRN_BLOCK)
        rows = pl.ds(r0, HGRN_BLOCK)
        dl = dl_scr[pl.ds(b, 1), :]
        for h in range(H_A):
            cols = slice(h * DK_A, (h + 1) * DK_A)
            vcols = slice(h * DV_A, (h + 1) * DV_A)
            v = ia_ref[rows, vcols].astype(BF16)
            st_h = s_scr[h]
            o_inter = lax.dot_general(qd_scr[rows, cols], st_h.astype(BF16), (((1,), (1,)), ((), ())),
                                      preferred_element_type=F32)
            att = lax.dot_general(qm_scr[rows, cols], km_scr[rows, cols], (((1,), (1,)), ((), ())),
                                  preferred_element_type=F32)
            att = jnp.where(tri, att, 0.0)
            o_scr[rows, vcols] = o_inter + jnp.dot(att.astype(BF16), v, preferred_element_type=F32)
            upd_t = lax.dot_general(v, kl_scr[rows, cols], (((0,), (0,)), ((), ())),
                                    preferred_element_type=F32)
            s_scr[h] = dl[:, cols] * st_h + upd_t
        return carry

    lax.fori_loop(0, nblk, block, 0)

    gn = gn_ref[...]
    for h in range(H_A):
        vcols = slice(h * DV_A, (h + 1) * DV_A)
        o = o_scr[:, vcols]
        o = o * lax.rsqrt(jnp.mean(o * o, axis=-1, keepdims=True) + EPS) * gn
        o_ref[:, vcols] = (o * sg_ref[:, vcols]).astype(BF16)

    @pl.when(j == pl.num_programs(1) - 1)
    def _():
        for h in range(H_A):
            sfin_ref[0, h] = s_scr[h].T


def _hgrn(qa, f, ia, sg, gn, s0, *, bsz, seq, row0, tl):
    d = qa.shape[1]
    nt = seq // tl
    blk0 = row0 // tl
    row = pl.BlockSpec((tl, d), lambda b, j: (blk0 + b * nt + j, 0))
    orow = pl.BlockSpec((tl, d), lambda b, j: (b * nt + j, 0))
    sspec = pl.BlockSpec((1, H_A, DK_A, DV_A), lambda b, j: (b, 0, 0, 0))
    return pl.pallas_call(
        _hgrn_kernel,
        grid=(bsz, nt),
        in_specs=[row, row, row, row, pl.BlockSpec((1, DV_A), lambda b, j: (0, 0)), sspec],
        out_specs=[orow, sspec],
        out_shape=[jax.ShapeDtypeStruct((bsz * seq, d), BF16),
                   jax.ShapeDtypeStruct((bsz, H_A, DK_A, DV_A), F32)],
        scratch_shapes=[pltpu.VMEM((H_A, DK_A, DV_A), F32)]
        + [pltpu.VMEM((tl, d), BF16)] * 4
        + [pltpu.VMEM((tl // HGRN_BLOCK, d), F32), pltpu.VMEM((tl, d), F32)],
        compiler_params=_cparams(("arbitrary", "arbitrary")),
        name="hgrn",
    )(qa, f, ia, sg, gn, s0)


SM_KB, SM_VB, SM_QI, SM_KI, SM_WI = 0, DH_B, 2 * DH_B, 2 * DH_B + H_I * D_I, 2 * DH_B + H_I * D_I + D_I
SM_USED = SM_WI + H_I
SM_WIDTH = -(-SM_USED // LANES) * LANES
INDEX_SCALE = (H_I * D_I) ** -0.5
NEG_INF = float("-inf")


def _dsa_kernel(qb_ref, sm_ref, k_ref, v_ref, ki_ref, lim_ref, o_ref, key_scr, bias_scr, *, n_sel):
    tq = qb_ref.shape[0]
    s = k_ref.shape[1]
    nt = (((1,), (1,)), ((), ()))

    kidx = ki_ref[0]
    score = jnp.zeros((tq, s), F32)
    for h in range(H_I):
        qih = sm_ref[:, SM_QI + h * D_I:SM_QI + (h + 1) * D_I].astype(BF16)
        sh = lax.dot_general(qih, kidx, nt, preferred_element_type=F32)
        wih = sm_ref[:, SM_WI + h:SM_WI + h + 1] * INDEX_SCALE
        score = score + jnp.maximum(sh, 0.0) * wih
    adm = lax.broadcasted_iota(I32, (tq, s), 1) < lim_ref[...]
    score = jnp.where(adm, score, NEG_INF)
    score = jnp.where(score == 0.0, 0.0, score)
    bits = lax.bitcast_convert_type(score, I32)
    key_scr[...] = jnp.where(bits < 0, bits ^ jnp.int32(0x7FFFFFFF), bits)

    def search(i, t_u):
        cand_u = t_u | lax.shift_left(jnp.int32(1), 31 - i)
        cnt = jnp.sum(jnp.where(key_scr[...] >= (cand_u ^ jnp.int32(INT_MIN)), 1.0, 0.0), axis=1, keepdims=True)
        return jnp.where(cnt >= n_sel, cand_u, t_u)

    t_s = lax.fori_loop(0, 32, search, jnp.zeros((tq, 1), I32)) ^ jnp.int32(INT_MIN)

    key = key_scr[...]
    gt = key > t_s
    eq = key == t_s
    need = n_sel - jnp.sum(jnp.where(gt, 1.0, 0.0), axis=1, keepdims=True)
    upper = (lax.broadcasted_iota(I32, (LANES, LANES), 0)
             <= lax.broadcasted_iota(I32, (LANES, LANES), 1)).astype(BF16)
    carry = jnp.zeros((tq, 1), F32)
    for c in range(s // LANES):
        cs = slice(c * LANES, (c + 1) * LANES)
        rank = jnp.dot(jnp.where(eq[:, cs], 1.0, 0.0).astype(BF16), upper, preferred_element_type=F32) + carry
        sel = (gt[:, cs] | (eq[:, cs] & (rank <= need))) & adm[:, cs]
        bias_scr[:, cs] = jnp.where(sel, 0.0, NEG_INF)
        carry = rank[:, LANES - 1:LANES]

    kk = k_ref[0]
    vv = v_ref[0]
    for h in range(H_B):
        hs = slice(h * DH_B, (h + 1) * DH_B)
        logits = lax.dot_general(qb_ref[:, hs], kk, nt, preferred_element_type=F32) * (DH_B ** -0.5)
        logits = logits + bias_scr[...]
        p = jnp.exp(logits - jnp.max(logits, axis=1, keepdims=True))
        denom = jnp.sum(p, axis=1, keepdims=True)
        o = jnp.dot(p.astype(BF16), vv, preferred_element_type=F32)
        o_ref[:, hs] = (o / denom).astype(BF16)


def _dsa(qb, sm, k16, v16, ki16, limits, *, bsz, seq, row0, tq, n_sel):
    d = qb.shape[1]
    s = k16.shape[1]
    nt = seq // tq
    blk0 = row0 // tq
    row = lambda w: pl.BlockSpec((tq, w), lambda b, j: (blk0 + b * nt + j, 0))
    keys = lambda w: pl.BlockSpec((1, s, w), lambda b, j: (b, 0, 0))
    return pl.pallas_call(
        functools.partial(_dsa_kernel, n_sel=n_sel),
        grid=(bsz, nt),
        in_specs=[row(d), row(sm.shape[1]), keys(DH_B), keys(DH_B), keys(D_I),
                  pl.BlockSpec((tq, 1), lambda b, j: (j, 0))],
        out_specs=pl.BlockSpec((tq, d), lambda b, j: (b * nt + j, 0)),
        out_shape=jax.ShapeDtypeStruct((bsz * seq, d), BF16),
        scratch_shapes=[pltpu.VMEM((tq, s), I32), pltpu.VMEM((tq, s), F32)],
        compiler_params=_cparams(("arbitrary", "arbitrary")),
        name="dsa",
    )(qb, sm, k16, v16, ki16, limits)


def _split_bf16(a):
    hi = a.astype(BF16)
    return hi, (a - hi.astype(F32)).astype(BF16)


def _mix_kernel(oa_ref, ob_ref, x_ref, woa_ref, wob_ref, wga_ref, wgb_ref, wout_ref, g1_ref, b1_ref,
                wrh_ref, wrl_ref, br_ref, x1_ref, eidx_ref, gate_ref, cnt_ref, *, alpha):
    i = pl.program_id(0)
    x = x_ref[...]
    xb = x.astype(BF16)
    dot = lambda a, w_ref: jnp.dot(a, w_ref[...], preferred_element_type=F32)
    y_a = dot(oa_ref[...], woa_ref)
    y_b = dot(ob_ref[...], wob_ref)
    merged = _sigmoid(dot(xb, wga_ref)) * y_a + _sigmoid(dot(xb, wgb_ref)) * y_b
    mixed = dot(merged.astype(BF16), wout_ref)
    x1 = _layer_norm(alpha * x + mixed, g1_ref[...], b1_ref[...])
    x1_ref[...] = x1

    nt = (((1,), (1,)), ((), ()))
    xh, xl = _split_bf16(x1)
    wh, wl = wrh_ref[...], wrl_ref[...]
    logits = (lax.dot_general(wh, xh, nt, preferred_element_type=F32)
              + lax.dot_general(wh, xl, nt, preferred_element_type=F32)
              + lax.dot_general(wl, xh, nt, preferred_element_type=F32)) + br_ref[...]
    ne, tm = logits.shape
    erow = lax.broadcasted_iota(I32, (ne, tm), 0)
    vals, idxs = [], []
    for _ in range(TOP_K):
        m = jnp.max(logits, axis=0, keepdims=True)
        idx = jnp.min(jnp.where(logits == m, erow, ne), axis=0, keepdims=True)
        vals.append(m)
        idxs.append(idx)
        logits = jnp.where(erow == idx, NEG_INF, logits)
    ex = [jnp.exp(v - vals[0]) for v in vals]
    denom = ex[0] + ex[1] + ex[2] + ex[3]
    pad = SUBLANES - TOP_K
    eidx = jnp.concatenate(idxs + [jnp.zeros((pad, tm), I32)], axis=0)
    eidx_ref[...] = eidx
    gate_ref[...] = jnp.concatenate([e / denom for e in ex] + [jnp.zeros((pad, tm), F32)], axis=0)

    onehot = jnp.zeros((ne, tm), F32)
    for idx in idxs:
        onehot = onehot + jnp.where(erow == idx, 1.0, 0.0)
    tile_cnt = jnp.broadcast_to(jnp.sum(onehot, axis=1, keepdims=True), cnt_ref.shape)

    @pl.when(i == 0)
    def _():
        cnt_ref[...] = tile_cnt

    @pl.when(i > 0)
    def _():
        cnt_ref[...] = cnt_ref[...] + tile_cnt


def _mix(oa, ob, x, woa, wob, wga, wgb, wout, g1, b1, wrh, wrl, br, *, tm, alpha):
    n, d = x.shape
    ne = wrh.shape[0]
    row = lambda w: pl.BlockSpec((tm, w), lambda i: (i, 0))
    col = pl.BlockSpec((SUBLANES, tm), lambda i: (0, i))
    full = lambda a: pl.BlockSpec(a.shape, lambda i: (0,) * a.ndim)
    return pl.pallas_call(
        functools.partial(_mix_kernel, alpha=alpha),
        grid=(n // tm,),
        in_specs=[row(d), row(d), row(d)] + [full(a) for a in (woa, wob, wga, wgb, wout, g1, b1, wrh, wrl, br)],
        out_specs=[row(d), col, col, pl.BlockSpec((ne, LANES), lambda i: (0, 0))],
        out_shape=[jax.ShapeDtypeStruct((n, d), F32), jax.ShapeDtypeStruct((SUBLANES, n), I32),
                   jax.ShapeDtypeStruct((SUBLANES, n), F32), jax.ShapeDtypeStruct((ne, LANES), F32)],
        compiler_params=_cparams(("arbitrary",)),
        name="mix",
    )(oa, ob, x, woa, wob, wga, wgb, wout, g1, b1, wrh, wrl, br)


def _sublane_cumsum(a):
    n = a.shape[0]
    row = lax.broadcasted_iota(I32, a.shape, 0)
    s = 1
    while s < n:
        a = a + jnp.where(row >= s, pltpu.roll(a, s, 0), 0.0)
        s *= 2
    return a


def _route_kernel(eidx_ref, cnt_ref, dest_ref, blke_ref, carry_scr, *, bm, sub):
    i = pl.program_id(0)
    ne = cnt_ref.shape[0]
    tl = eidx_ref.shape[1]

    @pl.when(i == 0)
    def _():
        carry_scr[...] = jnp.zeros_like(carry_scr)

    counts = cnt_ref[...]
    padded = jnp.ceil(counts / bm) * bm
    pend = _sublane_cumsum(padded)
    pstart = (pend - padded)[:, 0:1]

    @pl.when(i == 0)
    def _():
        nb = blke_ref.shape[1]
        first_row = (lax.broadcasted_iota(I32, (ne, nb), 1) * bm).astype(F32)
        below = jnp.sum(jnp.where(pend[:, 0:1] <= first_row, 1.0, 0.0), axis=0, keepdims=True)
        blke_ref[...] = jnp.minimum(below, ne - 1.0).astype(I32)

    eidx = eidx_ref[...]
    erow = lax.broadcasted_iota(I32, (ne, tl), 0)
    hot = [jnp.where(erow == eidx[k:k + 1, :], 1.0, 0.0) for k in range(TOP_K)]
    onehot = hot[0] + hot[1] + hot[2] + hot[3]
    before = (lax.broadcasted_iota(I32, (tl, tl), 0) < lax.broadcasted_iota(I32, (tl, tl), 1)).astype(BF16)
    base = jnp.dot(onehot.astype(BF16), before, preferred_element_type=F32) + carry_scr[:, 0:1] + pstart
    dest = jnp.concatenate([jnp.sum(hk * base, axis=0, keepdims=True) for hk in hot], axis=0).astype(I32)
    for c in range(tl // sub):
        dest_ref[c] = dest[:, c * sub:(c + 1) * sub]
    carry_scr[...] = carry_scr[...] + jnp.sum(onehot, axis=1, keepdims=True)


def _route(eidx, counts, *, tl, sub, bm, nb):
    n = eidx.shape[1]
    ne = counts.shape[0]
    nb_pad = -(-nb // LANES) * LANES
    return pl.pallas_call(
        functools.partial(_route_kernel, bm=bm, sub=sub),
        grid=(n // tl,),
        in_specs=[pl.BlockSpec((SUBLANES, tl), lambda i: (0, i)), pl.BlockSpec(counts.shape, lambda i: (0, 0))],
        out_specs=[pl.BlockSpec((tl // sub, TOP_K, sub), lambda i: (i, 0, 0)),
                   pl.BlockSpec((1, nb_pad), lambda i: (0, 0))],
        out_shape=[jax.ShapeDtypeStruct((n // sub, TOP_K, sub), I32), jax.ShapeDtypeStruct((1, nb_pad), I32)],
        scratch_shapes=[pltpu.VMEM((ne, LANES), F32)],
        compiler_params=_cparams(("arbitrary",)),
        name="route",
    )(eidx, counts)


def _dispatch_kernel(dest_ref, x_ref, zeros_ref, xs_ref, sem):
    del zeros_ref
    tm = x_ref.shape[0]

    def row_copy(t, k):
        d = dest_ref[0, k, t]
        return pltpu.make_async_copy(x_ref.at[pl.ds(t, 1), :], xs_ref.at[pl.ds(d, 1), :], sem)

    def start(t, carry):
        for k in range(TOP_K):
            row_copy(t, k).start()
        return carry

    def wait(t, carry):
        for k in range(TOP_K):
            row_copy(t, k).wait()
        return carry

    lax.fori_loop(0, tm, start, 0)
    lax.fori_loop(0, tm, wait, 0)


def _dispatch(dest, x1, n_rows, *, tm):
    n, d = x1.shape
    zeros = jnp.zeros((n_rows, d), x1.dtype)
    return pl.pallas_call(
        _dispatch_kernel,
        grid=(n // tm,),
        in_specs=[pl.BlockSpec((1, TOP_K, tm), lambda i: (i, 0, 0), memory_space=pltpu.SMEM),
                  pl.BlockSpec((tm, d), lambda i: (i, 0)),
                  pl.BlockSpec(memory_space=pl.ANY)],
        out_specs=pl.BlockSpec(memory_space=pl.ANY),
        out_shape=jax.ShapeDtypeStruct((n_rows, d), x1.dtype),
        scratch_shapes=[pltpu.SemaphoreType.DMA(())],
        input_output_aliases={2: 0},
        compiler_params=_cparams(("arbitrary",)),
        name="dispatch",
    )(dest, x1, zeros)


def _experts_kernel(blke_ref, xs_ref, wu_ref, bu_ref, wd_ref, bd_ref, ys_ref):
    del blke_ref
    dff = wd_ref.shape[1]
    h = jnp.dot(xs_ref[...].astype(BF16), wu_ref[0], preferred_element_type=F32) + bu_ref[0]
    glu = jnp.minimum(h[:, :dff], SWIGLU_LIMIT)
    lin = jnp.clip(h[:, dff:], -SWIGLU_LIMIT, SWIGLU_LIMIT)
    act = glu * _sigmoid(SWIGLU_ALPHA * glu) * (lin + 1.0)
    ys_ref[...] = jnp.dot(act.astype(BF16), wd_ref[0], preferred_element_type=F32) + bd_ref[0]


def _experts(blk_e, xs, wu, bu, wd, bd, *, bm):
    n_rows, d = xs.shape
    ne, _, dff2 = wu.shape
    dff = wd.shape[1]
    grid_spec = pltpu.PrefetchScalarGridSpec(
        num_scalar_prefetch=1,
        grid=(n_rows // bm,),
        in_specs=[pl.BlockSpec((bm, d), lambda i, e: (i, 0)),
                  pl.BlockSpec((1, d, dff2), lambda i, e: (e[i], 0, 0)),
                  pl.BlockSpec((1, 1, dff2), lambda i, e: (e[i], 0, 0)),
                  pl.BlockSpec((1, dff, d), lambda i, e: (e[i], 0, 0)),
                  pl.BlockSpec((1, 1, d), lambda i, e: (e[i], 0, 0))],
        out_specs=pl.BlockSpec((bm, d), lambda i, e: (i, 0)),
    )
    return pl.pallas_call(
        _experts_kernel,
        grid_spec=grid_spec,
        out_shape=jax.ShapeDtypeStruct((n_rows, d), F32),
        compiler_params=_cparams(("arbitrary",)),
        name="experts",
    )(blk_e, xs, wu, bu.reshape(ne, 1, dff2), wd, bd.reshape(ne, 1, d))


def _combine_kernel(dest_ref, ys_ref, gate_ref, x1_ref, g2_ref, b2_ref, out_ref, buf, sem, *, alpha):
    tm = x1_ref.shape[0]

    def row_copy(t, k):
        d = dest_ref[0, k, t]
        return pltpu.make_async_copy(ys_ref.at[pl.ds(d, 1), :], buf.at[k, pl.ds(t, 1), :], sem)

    def start(t, carry):
        for k in range(TOP_K):
            row_copy(t, k).start()
        return carry

    def wait(t, carry):
        for k in range(TOP_K):
            row_copy(t, k).wait()
        return carry

    lax.fori_loop(0, tm, start, 0)
    lax.fori_loop(0, tm, wait, 0)

    gate = gate_ref[...].T
    y = buf[0] * gate[:, 0:1]
    for k in range(1, TOP_K):
        y = y + buf[k] * gate[:, k:k + 1]
    out_ref[...] = _layer_norm(alpha * x1_ref[...] + y, g2_ref[...], b2_ref[...])


def _combine(dest, ys, gate, x1, g2, b2, *, tm, alpha):
    n, d = x1.shape
    return pl.pallas_call(
        functools.partial(_combine_kernel, alpha=alpha),
        grid=(n // tm,),
        in_specs=[pl.BlockSpec((1, TOP_K, tm), lambda i: (i, 0, 0), memory_space=pltpu.SMEM),
                  pl.BlockSpec(memory_space=pl.ANY),
                  pl.BlockSpec((SUBLANES, tm), lambda i: (0, i)),
                  pl.BlockSpec((tm, d), lambda i: (i, 0)),
                  pl.BlockSpec((1, d), lambda i: (0, 0)),
                  pl.BlockSpec((1, d), lambda i: (0, 0))],
        out_specs=pl.BlockSpec((tm, d), lambda i: (i, 0)),
        out_shape=jax.ShapeDtypeStruct((n, d), F32),
        scratch_shapes=[pltpu.VMEM((TOP_K, tm, d), F32), pltpu.SemaphoreType.DMA(())],
        compiler_params=_cparams(("arbitrary",)),
        name="combine",
    )(dest, ys, gate, x1, g2, b2)


PROJ_TM = 256
HGRN_TL = 256
DSA_TQ = 128
MIX_TM = 256
ROUTE_TL = 1280
MOE_TM = 256
MOE_BM = 256


def kernel(x_prompt, x_sample, cache_k, cache_v, cache_kidx, state_hgrn, w_in, lb_logits, gn_a, w_oa, w_ob,
           w_out, ln1_g, ln1_b, w_router, b_router, w_up, b_up, w_down, b_down, ln2_g, ln2_b):
    depth = w_in.shape[0]
    assert depth == 1
    bsz, seq, d = x_prompt.shape
    dbsz, dseq, _ = x_sample.shape
    past = cache_k.shape[2]
    n_p, n_s = bsz * seq, dbsz * dseq
    n = n_p + n_s
    alpha = (2 * depth) ** 0.25

    sizes = (H_A * DK_A, H_A * DK_A, H_A * DV_A, H_A * DV_A, H_B * DH_B, DH_B, DH_B, H_I * D_I, H_I, D_I, d, d)
    offs = np.concatenate([[0], np.cumsum(sizes)])
    w = w_in[0].astype(BF16)
    grp = lambda g: w[:, offs[g]:offs[g + 1]]
    wqa, wfa, wia, wg, wqb, wkb, wvb, wqi, wwi, wki, wgta, wgtb = (grp(g) for g in range(12))
    wsm = jnp.concatenate([wkb, wvb, wqi, wki, wwi, jnp.zeros((d, SM_WIDTH - SM_USED), BF16)], axis=1)

    x_all = jnp.concatenate([x_prompt.reshape(n_p, d), x_sample.reshape(n_s, d)], axis=0)
    qa, f, ia, sg, qb, sm = _proj(x_all, wqa, wfa, wia, wg, wqb, wsm, lb_logits, PROJ_TM)

    gn = gn_a[0].reshape(1, DV_A)
    oa_p, sfin_p = _hgrn(qa, f, ia, sg, gn, jnp.zeros((bsz, H_A, DK_A, DV_A), F32),
                         bsz=bsz, seq=seq, row0=0, tl=HGRN_TL)
    oa_s, sfin_s = _hgrn(qa, f, ia, sg, gn, state_hgrn[0], bsz=dbsz, seq=dseq, row0=n_p, tl=dseq)

    kb_p = sm[:n_p, SM_KB:SM_KB + DH_B].reshape(bsz, seq, DH_B)
    vb_p = sm[:n_p, SM_VB:SM_VB + DH_B].reshape(bsz, seq, DH_B)
    ki_p = sm[:n_p, SM_KI:SM_KI + D_I].reshape(bsz, seq, D_I)
    kb_s = sm[n_p:, SM_KB:SM_KB + DH_B].reshape(dbsz, dseq, DH_B)
    vb_s = sm[n_p:, SM_VB:SM_VB + DH_B].reshape(dbsz, dseq, DH_B)
    ki_s = sm[n_p:, SM_KI:SM_KI + D_I].reshape(dbsz, dseq, D_I)
    lim_p = ((jnp.arange(seq, dtype=I32) // CHUNK + 1) * CHUNK).reshape(seq, 1)
    ob_p = _dsa(qb, sm, kb_p.astype(BF16), vb_p.astype(BF16), ki_p.astype(BF16), lim_p,
                bsz=bsz, seq=seq, row0=0, tq=DSA_TQ, n_sel=min(TOPK_MAX, seq // 4))
    n_keys = past + dseq
    key_pad = -(-n_keys // LANES) * LANES - n_keys
    keys_s = lambda c, new: jnp.pad(jnp.concatenate([c[0].astype(BF16), new.astype(BF16)], axis=1),
                                    ((0, 0), (0, key_pad), (0, 0)))
    lim_s = jnp.full((dseq, 1), n_keys, I32)
    ob_s = _dsa(qb, sm, keys_s(cache_k, kb_s), keys_s(cache_v, vb_s), keys_s(cache_kidx, ki_s), lim_s,
                bsz=dbsz, seq=dseq, row0=n_p, tq=dseq, n_sel=min(TOPK_MAX, n_keys // 4))

    oa = jnp.concatenate([oa_p, oa_s], axis=0)
    ob = jnp.concatenate([ob_p, ob_s], axis=0)
    wrh, wrl = _split_bf16(w_router[0].T)
    x1, eidx, gate, counts = _mix(
        oa, ob, x_all, w_oa[0].astype(BF16), w_ob[0].astype(BF16), wgta, wgtb, w_out[0].astype(BF16),
        ln1_g[0].reshape(1, d), ln1_b[0].reshape(1, d), wrh, wrl, b_router[0].reshape(N_EXPERTS, 1),
        tm=MIX_TM, alpha=alpha)

    n_rows = -(-(n * TOP_K + N_EXPERTS * (MOE_BM - 1)) // MOE_BM) * MOE_BM
    dest, blk_e = _route(eidx, counts, tl=ROUTE_TL, sub=MOE_TM, bm=MOE_BM, nb=n_rows // MOE_BM)
    xs = _dispatch(dest, x1, n_rows, tm=MOE_TM)
    ys = _experts(blk_e[0, :n_rows // MOE_BM], xs, w_up[0].astype(BF16), b_up[0], w_down[0].astype(BF16),
                  b_down[0], bm=MOE_BM)
    out = _combine(dest, ys, gate, x1, ln2_g[0].reshape(1, d), ln2_b[0].reshape(1, d), tm=MOE_TM, alpha=alpha)

    return (out[:n_p].reshape(bsz, seq, d), out[n_p:].reshape(dbsz, dseq, d),
            kb_p[None], vb_p[None], ki_p[None], sfin_p[None],
            kb_s[None], vb_s[None], ki_s[None], sfin_s[None])
```

```python
import functools

import jax
import jax.numpy as jnp
import numpy as np
from jax import lax
from jax.experimental import pallas as pl
from jax.experimental.pallas import tpu as pltpu

F32 = jnp.float32
BF16 = jnp.bfloat16
I32 = jnp.int32

CHUNK = 64
H_A = 8
DK_A = 128
DV_A = 128
HGRN_BLOCK = 16
H_B = 8
DH_B = 128
H_I = 8
D_I = 64
TOPK_MAX = 256
N_EXPERTS = 32
TOP_K = 4
SWIGLU_ALPHA = 1.702
SWIGLU_LIMIT = 7.0
EPS = 1e-5

LANES = 128
SUBLANES = 8
VMEM_LIMIT_BYTES = 56 * 1024 * 1024

INT_MIN = -(2 ** 31)


def _cparams(sem):
    return pltpu.CompilerParams(dimension_semantics=sem, vmem_limit_bytes=VMEM_LIMIT_BYTES)


def _sigmoid(x):
    return 1.0 / (1.0 + jnp.exp(-x))


def _full_spec(shape):
    nd = len(shape)
    return pl.BlockSpec(shape, lambda *_: (0,) * nd)


def _layer_norm(x, g, b):
    mu = jnp.mean(x, axis=-1, keepdims=True)
    xc = x - mu
    var = jnp.mean(xc * xc, axis=-1, keepdims=True)
    return xc * lax.rsqrt(var + EPS) * g + b


def _proj_kernel(x_ref, wqa_ref, wfa_ref, wia_ref, wga_ref, wqb_ref, wsm_ref, lbl_ref,
                 qa_ref, f_ref, ia_ref, sg_ref, qb_ref, sm_ref):
    xb = x_ref[...].astype(BF16)
    dot = lambda w_ref: jnp.dot(xb, w_ref[...], preferred_element_type=F32)
    qa_ref[...] = dot(wqa_ref)
    lbl = lbl_ref[...]
    lbe = jnp.exp(lbl - jnp.max(lbl, axis=0, keepdims=True))
    lb = lbe[0:1, :] / jnp.sum(lbe, axis=0, keepdims=True)
    f_ref[...] = lb + (1.0 - lb) * _sigmoid(dot(wfa_ref))
    ia_ref[...] = dot(wia_ref)
    g = dot(wga_ref)
    sg_ref[...] = g * _sigmoid(g)
    qb_ref[...] = dot(wqb_ref).astype(BF16)
    sm_ref[...] = dot(wsm_ref)


def _proj(x, wqa, wfa, wia, wga, wqb, wsm, lb_logits, tm):
    n, d = x.shape
    nsm = wsm.shape[1]
    row = lambda w: pl.BlockSpec((tm, w), lambda i: (i, 0))
    wspec = lambda w: pl.BlockSpec(w.shape, lambda i: (0, 0))
    return pl.pallas_call(
        _proj_kernel,
        grid=(n // tm,),
        in_specs=[row(d), wspec(wqa), wspec(wfa), wspec(wia), wspec(wga), wspec(wqb), wspec(wsm),
                  wspec(lb_logits)],
        out_specs=[row(d), row(d), row(d), row(d), row(d), row(nsm)],
        out_shape=[jax.ShapeDtypeStruct((n, d), F32)] * 4
        + [jax.ShapeDtypeStruct((n, d), BF16), jax.ShapeDtypeStruct((n, nsm), F32)],
        compiler_params=_cparams(("arbitrary",)),
        name="proj",
    )(x, wqa, wfa, wia, wga, wqb, wsm, lb_logits)


def _block_cumsum(a, row_in_block):
    s = 1
    while s < HGRN_BLOCK:
        a = a + jnp.where(row_in_block >= s, pltpu.roll(a, s, 0), 0.0)
        s *= 2
    return a


def _hgrn_kernel(qa_ref, f_ref, ia_ref, sg_ref, gn_ref, s0_ref, o_ref, sfin_ref,
                 s_scr, qd_scr, qm_scr, km_scr, kl_scr, v_scr, dl_scr, o_scr):
    j = pl.program_id(1)
    tl = qa_ref.shape[0]
    nblk = tl // HGRN_BLOCK
    half = HGRN_BLOCK // 2

    @pl.when(j == 0)
    def _():
        for h in range(H_A):
            s_scr[h] = s0_ref[0, h].T

    f = f_ref[...]
    q = qa_ref[...]
    d = f.shape[1]
    row_in_block = lax.broadcasted_iota(I32, (tl, d), 0) % HGRN_BLOCK
    cum = _block_cumsum(jnp.log(f), row_in_block)
    cum3 = cum.reshape(nblk, HGRN_BLOCK, d)
    mid = jnp.broadcast_to(cum3[:, half:half + 1, :], cum3.shape).reshape(tl, d)
    last3 = cum3[:, HGRN_BLOCK - 1:HGRN_BLOCK, :]
    last = jnp.broadcast_to(last3, cum3.shape).reshape(tl, d)
    k = 1.0 - f
    qd_scr[...] = (q * jnp.exp(cum)).astype(BF16)
    qm_scr[...] = (q * jnp.exp(cum - mid)).astype(BF16)
    km_scr[...] = (k * jnp.exp(mid - cum)).astype(BF16)
    kl_scr[...] = (k * jnp.exp(last - cum)).astype(BF16)
    dl_scr[...] = jnp.exp(last3.reshape(nblk, d))
    v_scr[...] = ia_ref[...].astype(BF16)

    nt = (((1,), (1,)), ((), ()))
    ri = lax.broadcasted_iota(I32, (tl, tl), 0)
    ci = lax.broadcasted_iota(I32, (tl, tl), 1)
    keep = (ri // HGRN_BLOCK == ci // HGRN_BLOCK) & (ri >= ci)
    for h in range(H_A):
        cols = slice(h * DK_A, (h + 1) * DK_A)
        vcols = slice(h * DV_A, (h + 1) * DV_A)
        att = lax.dot_general(qm_scr[:, cols], km_scr[:, cols], nt, preferred_element_type=F32)
        att = jnp.where(keep, att, 0.0).astype(BF16)
        o_scr[:, vcols] = jnp.dot(att, v_scr[:, vcols], preferred_element_type=F32)

    def block(b, carry):
        r0 = pl.multiple_of(b * HGRN_BLOCK, HGRN_BLOCK)
        rows = pl.ds(r0, HGRN_BLOCK)
        dl = dl_scr[pl.ds(b, 1), :]
        for h in range(H_A):
            cols = slice(h * DK_A, (h + 1) * DK_A)
            vcols = slice(h * DV_A, (h + 1) * DV_A)
            st_h = s_scr[h]
            o_scr[rows, vcols] += lax.dot_general(qd_scr[rows, cols], st_h.astype(BF16), nt,
                                                  preferred_element_type=F32)
            upd_t = lax.dot_general(v_scr[rows, vcols], kl_scr[rows, cols], (((0,), (0,)), ((), ())),
                                    preferred_element_type=F32)
            s_scr[h] = dl[:, cols] * st_h + upd_t
        return carry

    lax.fori_loop(0, nblk, block, 0, unroll=2 if nblk % 2 == 0 else 1)

    gn = gn_ref[...]
    for h in range(H_A):
        vcols = slice(h * DV_A, (h + 1) * DV_A)
        o = o_scr[:, vcols]
        o = o * lax.rsqrt(jnp.mean(o * o, axis=-1, keepdims=True) + EPS) * gn
        o_ref[:, vcols] = (o * sg_ref[:, vcols]).astype(BF16)

    @pl.when(j == pl.num_programs(1) - 1)
    def _():
        for h in range(H_A):
            sfin_ref[0, h] = s_scr[h].T


def _hgrn(qa, f, ia, sg, gn, s0, *, bsz, seq, row0, tl):
    d = qa.shape[1]
    nt = seq // tl
    blk0 = row0 // tl
    row = pl.BlockSpec((tl, d), lambda b, j: (blk0 + b * nt + j, 0))
    orow = pl.BlockSpec((tl, d), lambda b, j: (b * nt + j, 0))
    sspec = pl.BlockSpec((1, H_A, DK_A, DV_A), lambda b, j: (b, 0, 0, 0))
    return pl.pallas_call(
        _hgrn_kernel,
        grid=(bsz, nt),
        in_specs=[row, row, row, row, pl.BlockSpec((1, DV_A), lambda b, j: (0, 0)), sspec],
        out_specs=[orow, sspec],
        out_shape=[jax.ShapeDtypeStruct((bsz * seq, d), BF16),
                   jax.ShapeDtypeStruct((bsz, H_A, DK_A, DV_A), F32)],
        scratch_shapes=[pltpu.VMEM((H_A, DK_A, DV_A), F32)]
        + [pltpu.VMEM((tl, d), BF16)] * 5
        + [pltpu.VMEM((tl // HGRN_BLOCK, d), F32), pltpu.VMEM((tl, d), F32)],
        compiler_params=_cparams(("arbitrary", "arbitrary")),
        name="hgrn",
    )(qa, f, ia, sg, gn, s0)


SM_KB, SM_VB, SM_QI, SM_KI, SM_WI = 0, DH_B, 2 * DH_B, 2 * DH_B + H_I * D_I, 2 * DH_B + H_I * D_I + D_I
SM_USED = SM_WI + H_I
SM_WIDTH = -(-SM_USED // LANES) * LANES
INDEX_SCALE = (H_I * D_I) ** -0.5
NEG_INF = float("-inf")


F32_EXP_MASK = 0x7F800000
F32_MIN_NORMAL = 0x00800000
SEARCH_UNROLL = 4


def _key_to_float(u):
    key = u ^ jnp.int32(INT_MIN)
    bits = jnp.where(key < 0, key ^ jnp.int32(0x7FFFFFFF), key)
    below_neg_inf = (bits < 0) & ((bits & jnp.int32(0x7FFFFFFF)) > jnp.int32(F32_EXP_MASK))
    return jnp.where(below_neg_inf, NEG_INF, lax.bitcast_convert_type(bits, F32))


def _positive_bits_to_float(bits):
    return jnp.where(bits < jnp.int32(F32_MIN_NORMAL), 0.0, lax.bitcast_convert_type(bits, F32))


def _count(mask):
    return jnp.sum(jnp.where(mask, 1.0, 0.0), axis=1, keepdims=True)


def _select_bias(score_scr, bias_scr, adm, s, n_sel):
    tq = score_scr.shape[0]

    def resolve_bit(i, t_u, cnt_t):
        cand_u = t_u | lax.shift_left(jnp.int32(1), 31 - i)
        cnt = _count(score_scr[:, :s] >= _key_to_float(cand_u))
        ok = cnt >= n_sel
        return jnp.where(ok, cand_u, t_u), jnp.where(ok, cnt, cnt_t)

    def unresolved(c):
        i, _, cnt_t = c
        return jnp.logical_and(i < 32, jnp.any(cnt_t != n_sel))

    def resolve_bits(c):
        i, t_u, cnt_t = c
        for b in range(SEARCH_UNROLL):
            t_u, cnt_t = resolve_bit(i + b, t_u, cnt_t)
        return i + SEARCH_UNROLL, t_u, cnt_t

    _, t_u, cnt_t = lax.while_loop(unresolved, resolve_bits,
                                   (jnp.int32(0), jnp.zeros((tq, 1), I32), jnp.full((tq, 1), float(s), F32)))
    t_f = _key_to_float(t_u)
    score = score_scr[:, :s]
    bias_scr[:, :s] = jnp.where((score >= t_f) & adm, 0.0, NEG_INF)

    @pl.when(jnp.any(cnt_t > n_sel))
    def _():
        above = score >= _key_to_float(t_u + 1)
        bucket = (score >= t_f) & jnp.logical_not(above)
        need = n_sel - _count(above)
        off = jnp.where(bucket, score - jnp.where(t_f == NEG_INF, 0.0, t_f), -1.0)

        def resolve_offset_bit(i, r_bits):
            cand = r_bits | lax.shift_left(jnp.int32(1), 30 - i)
            ok = _count(off >= _positive_bits_to_float(cand)) >= need
            return jnp.where(ok, cand, r_bits)

        r_bits = lax.fori_loop(0, 31, resolve_offset_bit, jnp.zeros((tq, 1), I32))
        above2 = off >= _positive_bits_to_float(r_bits + 1)
        tie = (off >= _positive_bits_to_float(r_bits)) & jnp.logical_not(above2)
        need2 = need - _count(above2)
        upper = (lax.broadcasted_iota(I32, (LANES, LANES), 0)
                 <= lax.broadcasted_iota(I32, (LANES, LANES), 1)).astype(BF16)
        carry = jnp.zeros((tq, 1), F32)
        for c in range(s // LANES):
            cs = slice(c * LANES, (c + 1) * LANES)
            rank = jnp.dot(jnp.where(tie[:, cs], 1.0, 0.0).astype(BF16), upper, preferred_element_type=F32) + carry
            sel = (above[:, cs] | above2[:, cs] | (tie[:, cs] & (rank <= need2))) & adm[:, cs]
            bias_scr[:, cs] = jnp.where(sel, 0.0, NEG_INF)
            carry = rank[:, LANES - 1:LANES]


def _dsa_tile(qb_ref, sm_ref, k_ref, v_ref, ki_ref, lim_ref, o_ref, score_scr, bias_scr, *, s, search, n_sel):
    tq = qb_ref.shape[0]
    nt = (((1,), (1,)), ((), ()))
    adm = lax.broadcasted_iota(I32, (tq, s), 1) < lim_ref[...]

    if search:
        kidx = ki_ref[0, :s, :]
        score = jnp.zeros((tq, s), F32)
        for h in range(H_I):
            qih = sm_ref[:, SM_QI + h * D_I:SM_QI + (h + 1) * D_I].astype(BF16)
            sh = lax.dot_general(qih, kidx, nt, preferred_element_type=F32)
            wih = sm_ref[:, SM_WI + h:SM_WI + h + 1] * INDEX_SCALE
            score = score + jnp.maximum(sh, 0.0) * wih
        score_scr[:, :s] = jnp.where(adm, score, NEG_INF)
        _select_bias(score_scr, bias_scr, adm, s, n_sel)
    else:
        bias_scr[:, :s] = jnp.where(adm, 0.0, NEG_INF)

    kk = k_ref[0, :s, :]
    vv = v_ref[0, :s, :]
    for h in range(H_B):
        hs = slice(h * DH_B, (h + 1) * DH_B)
        logits = lax.dot_general(qb_ref[:, hs], kk, nt, preferred_element_type=F32) * (DH_B ** -0.5)
        logits = logits + bias_scr[:, :s]
        p = jnp.exp(logits - jnp.max(logits, axis=1, keepdims=True))
        denom = jnp.sum(p, axis=1, keepdims=True)
        o = jnp.dot(p.astype(BF16), vv, preferred_element_type=F32)
        o_ref[:, hs] = (o / denom).astype(BF16)


def _dsa_kernel(*refs, variants, tile_variant, n_sel):
    j = pl.program_id(1)
    for vid, (s, search) in enumerate(variants):
        tiles = [t for t, v in enumerate(tile_variant) if v == vid]
        lo, hi = tiles[0], tiles[-1]
        assert tiles == list(range(lo, hi + 1))

        @pl.when((j >= lo) & (j <= hi))
        def _(s=s, search=search):
            _dsa_tile(*refs, s=s, search=search, n_sel=n_sel)


DSA_KEY_GRAN = 256


def _dsa(qb, sm, k16, v16, ki16, limits, *, bsz, seq, row0, tq, n_sel):
    d = qb.shape[1]
    s = k16.shape[1]
    nt = seq // tq
    blk0 = row0 // tq
    limits = np.asarray(limits, np.int32)
    tile_max = limits.reshape(nt, tq).max(axis=1)
    per_tile = [(int(min(s, -(-m // DSA_KEY_GRAN) * DSA_KEY_GRAN)), bool(m > n_sel)) for m in tile_max]
    variants = tuple(sorted(set(per_tile)))
    tile_variant = tuple(variants.index(v) for v in per_tile)
    row = lambda w: pl.BlockSpec((tq, w), lambda b, j: (blk0 + b * nt + j, 0))
    keys = lambda w: pl.BlockSpec((1, s, w), lambda b, j: (b, 0, 0))
    return pl.pallas_call(
        functools.partial(_dsa_kernel, variants=variants, tile_variant=tile_variant, n_sel=n_sel),
        grid=(bsz, nt),
        in_specs=[row(d), row(sm.shape[1]), keys(DH_B), keys(DH_B), keys(D_I),
                  pl.BlockSpec((tq, 1), lambda b, j: (j, 0))],
        out_specs=pl.BlockSpec((tq, d), lambda b, j: (b * nt + j, 0)),
        out_shape=jax.ShapeDtypeStruct((bsz * seq, d), BF16),
        scratch_shapes=[pltpu.VMEM((tq, s), F32), pltpu.VMEM((tq, s), F32)],
        compiler_params=_cparams(("arbitrary", "arbitrary")),
        name="dsa",
    )(qb, sm, k16, v16, ki16, jnp.asarray(limits.reshape(seq, 1)))


def _split_bf16(a):
    hi = a.astype(BF16)
    return hi, (a - hi.astype(F32)).astype(BF16)


def _mix_kernel(oa_ref, ob_ref, x_ref, woa_ref, wob_ref, wga_ref, wgb_ref, wout_ref, g1_ref, b1_ref,
                wrh_ref, wrl_ref, br_ref, x1_ref, eidx_ref, gate_ref, cnt_ref, *, alpha):
    i = pl.program_id(0)
    x = x_ref[...]
    xb = x.astype(BF16)
    dot = lambda a, w_ref: jnp.dot(a, w_ref[...], preferred_element_type=F32)
    y_a = dot(oa_ref[...], woa_ref)
    y_b = dot(ob_ref[...], wob_ref)
    merged = _sigmoid(dot(xb, wga_ref)) * y_a + _sigmoid(dot(xb, wgb_ref)) * y_b
    mixed = dot(merged.astype(BF16), wout_ref)
    x1 = _layer_norm(alpha * x + mixed, g1_ref[...], b1_ref[...])
    x1_ref[...] = x1

    nt = (((1,), (1,)), ((), ()))
    xh, xl = _split_bf16(x1)
    wh, wl = wrh_ref[...], wrl_ref[...]
    logits = (lax.dot_general(wh, xh, nt, preferred_element_type=F32)
              + lax.dot_general(wh, xl, nt, preferred_element_type=F32)
              + lax.dot_general(wl, xh, nt, preferred_element_type=F32)) + br_ref[...]
    ne, tm = logits.shape
    erow = lax.broadcasted_iota(I32, (ne, tm), 0)
    vals, idxs = [], []
    for _ in range(TOP_K):
        m = jnp.max(logits, axis=0, keepdims=True)
        idx = jnp.min(jnp.where(logits == m, erow, ne), axis=0, keepdims=True)
        vals.append(m)
        idxs.append(idx)
        logits = jnp.where(erow == idx, NEG_INF, logits)
    ex = [jnp.exp(v - vals[0]) for v in vals]
    denom = ex[0] + ex[1] + ex[2] + ex[3]
    pad = SUBLANES - TOP_K
    eidx = jnp.concatenate(idxs + [jnp.zeros((pad, tm), I32)], axis=0)
    eidx_ref[...] = eidx
    gate_ref[...] = jnp.concatenate([e / denom for e in ex] + [jnp.zeros((pad, tm), F32)], axis=0)

    onehot = jnp.zeros((ne, tm), F32)
    for idx in idxs:
        onehot = onehot + jnp.where(erow == idx, 1.0, 0.0)
    tile_cnt = jnp.broadcast_to(jnp.sum(onehot, axis=1, keepdims=True), cnt_ref.shape)

    @pl.when(i == 0)
    def _():
        cnt_ref[...] = tile_cnt

    @pl.when(i > 0)
    def _():
        cnt_ref[...] = cnt_ref[...] + tile_cnt


def _mix(oa, ob, x, woa, wob, wga, wgb, wout, g1, b1, wrh, wrl, br, *, tm, alpha):
    n, d = x.shape
    ne = wrh.shape[0]
    row = lambda w: pl.BlockSpec((tm, w), lambda i: (i, 0))
    col = pl.BlockSpec((SUBLANES, tm), lambda i: (0, i))
    full = lambda a: pl.BlockSpec(a.shape, lambda i: (0,) * a.ndim)
    return pl.pallas_call(
        functools.partial(_mix_kernel, alpha=alpha),
        grid=(n // tm,),
        in_specs=[row(d), row(d), row(d)] + [full(a) for a in (woa, wob, wga, wgb, wout, g1, b1, wrh, wrl, br)],
        out_specs=[row(d), col, col, pl.BlockSpec((ne, LANES), lambda i: (0, 0))],
        out_shape=[jax.ShapeDtypeStruct((n, d), F32), jax.ShapeDtypeStruct((SUBLANES, n), I32),
                   jax.ShapeDtypeStruct((SUBLANES, n), F32), jax.ShapeDtypeStruct((ne, LANES), F32)],
        compiler_params=_cparams(("arbitrary",)),
        name="mix",
    )(oa, ob, x, woa, wob, wga, wgb, wout, g1, b1, wrh, wrl, br)


def _sublane_cumsum(a):
    n = a.shape[0]
    row = lax.broadcasted_iota(I32, a.shape, 0)
    s = 1
    while s < n:
        a = a + jnp.where(row >= s, pltpu.roll(a, s, 0), 0.0)
        s *= 2
    return a


def _route_kernel(eidx_ref, cnt_ref, dest_ref, blke_ref, carry_scr, *, bm, sub):
    i = pl.program_id(0)
    ne = cnt_ref.shape[0]
    tl = eidx_ref.shape[1]

    @pl.when(i == 0)
    def _():
        carry_scr[...] = jnp.zeros_like(carry_scr)

    counts = cnt_ref[...]
    padded = jnp.ceil(counts / bm) * bm
    pend = _sublane_cumsum(padded)
    pstart = (pend - padded)[:, 0:1]

    @pl.when(i == 0)
    def _():
        nb = blke_ref.shape[1]
        first_row = (lax.broadcasted_iota(I32, (ne, nb), 1) * bm).astype(F32)
        below = jnp.sum(jnp.where(pend[:, 0:1] <= first_row, 1.0, 0.0), axis=0, keepdims=True)
        blke_ref[...] = jnp.minimum(below, ne - 1.0).astype(I32)

    eidx = eidx_ref[...]
    erow = lax.broadcasted_iota(I32, (ne, tl), 0)
    hot = [jnp.where(erow == eidx[k:k + 1, :], 1.0, 0.0) for k in range(TOP_K)]
    onehot = hot[0] + hot[1] + hot[2] + hot[3]
    before = (lax.broadcasted_iota(I32, (tl, tl), 0) < lax.broadcasted_iota(I32, (tl, tl), 1)).astype(BF16)
    base = jnp.dot(onehot.astype(BF16), before, preferred_element_type=F32) + carry_scr[:, 0:1] + pstart
    dest = jnp.concatenate([jnp.sum(hk * base, axis=0, keepdims=True) for hk in hot], axis=0).astype(I32)
    for c in range(tl // sub):
        dest_ref[c] = dest[:, c * sub:(c + 1) * sub]
    carry_scr[...] = carry_scr[...] + jnp.sum(onehot, axis=1, keepdims=True)


def _route(eidx, counts, *, tl, sub, bm, nb):
    n = eidx.shape[1]
    ne = counts.shape[0]
    nb_pad = -(-nb // LANES) * LANES
    return pl.pallas_call(
        functools.partial(_route_kernel, bm=bm, sub=sub),
        grid=(n // tl,),
        in_specs=[pl.BlockSpec((SUBLANES, tl), lambda i: (0, i)), pl.BlockSpec(counts.shape, lambda i: (0, 0))],
        out_specs=[pl.BlockSpec((tl // sub, TOP_K, sub), lambda i: (i, 0, 0)),
                   pl.BlockSpec((1, nb_pad), lambda i: (0, 0))],
        out_shape=[jax.ShapeDtypeStruct((n // sub, TOP_K, sub), I32), jax.ShapeDtypeStruct((1, nb_pad), I32)],
        scratch_shapes=[pltpu.VMEM((ne, LANES), F32)],
        compiler_params=_cparams(("arbitrary",)),
        name="route",
    )(eidx, counts)


def _dispatch_kernel(dest_ref, x_ref, zeros_ref, xs_ref, sems):
    del zeros_ref
    i = pl.program_id(0)
    tm = dest_ref.shape[2]

    def start(t, carry):
        for k in range(TOP_K):
            pltpu.make_async_copy(x_ref.at[pl.ds(i * tm + t, 1), :], xs_ref.at[pl.ds(dest_ref[0, k, t], 1), :],
                                  sems.at[i % 2]).start()
        return carry

    def wait_step(parity):
        def wait(t, carry):
            for _ in range(TOP_K):
                pltpu.make_async_copy(x_ref.at[pl.ds(0, 1), :], xs_ref.at[pl.ds(0, 1), :], sems.at[parity]).wait()
            return carry
        lax.fori_loop(0, tm, wait, 0)

    lax.fori_loop(0, tm, start, 0)

    @pl.when(i > 0)
    def _():
        wait_step((i - 1) % 2)

    @pl.when(i == pl.num_programs(0) - 1)
    def _():
        wait_step(i % 2)


def _dispatch(dest, x1, n_rows, *, tm):
    n, d = x1.shape
    zeros = jnp.zeros((n_rows, d), x1.dtype)
    return pl.pallas_call(
        _dispatch_kernel,
        grid=(n // tm,),
        in_specs=[pl.BlockSpec((1, TOP_K, tm), lambda i: (i, 0, 0), memory_space=pltpu.SMEM),
                  pl.BlockSpec(memory_space=pl.ANY),
                  pl.BlockSpec(memory_space=pl.ANY)],
        out_specs=pl.BlockSpec(memory_space=pl.ANY),
        out_shape=jax.ShapeDtypeStruct((n_rows, d), x1.dtype),
        scratch_shapes=[pltpu.SemaphoreType.DMA((2,))],
        input_output_aliases={2: 0},
        compiler_params=_cparams(("arbitrary",)),
        name="dispatch",
    )(dest, x1, zeros)


def _experts_kernel(blke_ref, xs_ref, wu_ref, bu_ref, wd_ref, bd_ref, ys_ref, wu16, wd16):
    i = pl.program_id(0)
    dff = wd_ref.shape[1]

    @pl.when(jnp.logical_or(i == 0, blke_ref[i] != blke_ref[jnp.maximum(i - 1, 0)]))
    def _():
        wu16[...] = wu_ref[0].astype(BF16)
        wd16[...] = wd_ref[0].astype(BF16)

    h = jnp.dot(xs_ref[...].astype(BF16), wu16[...], preferred_element_type=F32) + bu_ref[0]
    glu = jnp.minimum(h[:, :dff], SWIGLU_LIMIT)
    lin = jnp.clip(h[:, dff:], -SWIGLU_LIMIT, SWIGLU_LIMIT)
    act = glu * _sigmoid(SWIGLU_ALPHA * glu) * (lin + 1.0)
    ys_ref[...] = jnp.dot(act.astype(BF16), wd16[...], preferred_element_type=F32) + bd_ref[0]


def _experts(blk_e, xs, wu, bu, wd, bd, *, bm):
    n_rows, d = xs.shape
    ne, _, dff2 = wu.shape
    dff = wd.shape[1]
    grid_spec = pltpu.PrefetchScalarGridSpec(
        num_scalar_prefetch=1,
        grid=(n_rows // bm,),
        in_specs=[pl.BlockSpec((bm, d), lambda i, e: (i, 0)),
                  pl.BlockSpec((1, d, dff2), lambda i, e: (e[i], 0, 0)),
                  pl.BlockSpec((1, 1, dff2), lambda i, e: (e[i], 0, 0)),
                  pl.BlockSpec((1, dff, d), lambda i, e: (e[i], 0, 0)),
                  pl.BlockSpec((1, 1, d), lambda i, e: (e[i], 0, 0))],
        out_specs=pl.BlockSpec((bm, d), lambda i, e: (i, 0)),
        scratch_shapes=[pltpu.VMEM((d, dff2), BF16), pltpu.VMEM((dff, d), BF16)],
    )
    return pl.pallas_call(
        _experts_kernel,
        grid_spec=grid_spec,
        out_shape=jax.ShapeDtypeStruct((n_rows, d), F32),
        compiler_params=_cparams(("arbitrary",)),
        name="experts",
    )(blk_e, xs, wu, bu.reshape(ne, 1, dff2), wd, bd.reshape(ne, 1, d))


def _combine_kernel(dest_ref, dest_next_ref, ys_ref, gate_ref, x1_ref, g2_ref, b2_ref, out_ref, buf, sems, *, alpha):
    i = pl.program_id(0)
    tm = x1_ref.shape[0]
    slot = i % 2

    def fetch(d_ref, to_slot):
        def start(t, carry):
            for k in range(TOP_K):
                pltpu.make_async_copy(ys_ref.at[pl.ds(d_ref[0, k, t], 1), :], buf.at[to_slot, k, pl.ds(t, 1), :],
                                      sems.at[to_slot]).start()
            return carry
        lax.fori_loop(0, tm, start, 0)

    @pl.when(i == 0)
    def _():
        fetch(dest_ref, slot)

    @pl.when(i + 1 < pl.num_programs(0))
    def _():
        fetch(dest_next_ref, 1 - slot)

    def wait(t, carry):
        for k in range(TOP_K):
            pltpu.make_async_copy(ys_ref.at[pl.ds(0, 1), :], buf.at[slot, k, pl.ds(t, 1), :], sems.at[slot]).wait()
        return carry
    lax.fori_loop(0, tm, wait, 0)

    gate = gate_ref[...].T
    y = buf[slot, 0] * gate[:, 0:1]
    for k in range(1, TOP_K):
        y = y + buf[slot, k] * gate[:, k:k + 1]
    out_ref[...] = _layer_norm(alpha * x1_ref[...] + y, g2_ref[...], b2_ref[...])


def _combine(dest, ys, gate, x1, g2, b2, *, tm, alpha):
    n, d = x1.shape
    return pl.pallas_call(
        functools.partial(_combine_kernel, alpha=alpha),
        grid=(n // tm,),
        in_specs=[pl.BlockSpec((1, TOP_K, tm), lambda i: (i, 0, 0), memory_space=pltpu.SMEM),
                  pl.BlockSpec((1, TOP_K, tm), lambda i: (jnp.minimum(i + 1, n // tm - 1), 0, 0),
                               memory_space=pltpu.SMEM),
                  pl.BlockSpec(memory_space=pl.ANY),
                  pl.BlockSpec((SUBLANES, tm), lambda i: (0, i)),
                  pl.BlockSpec((tm, d), lambda i: (i, 0)),
                  pl.BlockSpec((1, d), lambda i: (0, 0)),
                  pl.BlockSpec((1, d), lambda i: (0, 0))],
        out_specs=pl.BlockSpec((tm, d), lambda i: (i, 0)),
        out_shape=jax.ShapeDtypeStruct((n, d), F32),
        scratch_shapes=[pltpu.VMEM((2, TOP_K, tm, d), F32), pltpu.SemaphoreType.DMA((2,))],
        compiler_params=_cparams(("arbitrary",)),
        name="combine",
    )(dest, dest, ys, gate, x1, g2, b2)


PROJ_TM = 256
HGRN_TL = 256
DSA_TQ = 128
MIX_TM = 256
ROUTE_TL = 1280
MOE_TM = 256
MOE_BM = 256


def kernel(x_prompt, x_sample, cache_k, cache_v, cache_kidx, state_hgrn, w_in, lb_logits, gn_a, w_oa, w_ob,
           w_out, ln1_g, ln1_b, w_router, b_router, w_up, b_up, w_down, b_down, ln2_g, ln2_b):
    depth = w_in.shape[0]
    assert depth == 1
    bsz, seq, d = x_prompt.shape
    dbsz, dseq, _ = x_sample.shape
    past = cache_k.shape[2]
    n_p, n_s = bsz * seq, dbsz * dseq
    n = n_p + n_s
    alpha = (2 * depth) ** 0.25

    sizes = (H_A * DK_A, H_A * DK_A, H_A * DV_A, H_A * DV_A, H_B * DH_B, DH_B, DH_B, H_I * D_I, H_I, D_I, d, d)
    offs = np.concatenate([[0], np.cumsum(sizes)])
    w = w_in[0].astype(BF16)
    grp = lambda g: w[:, offs[g]:offs[g + 1]]
    wqa, wfa, wia, wg, wqb, wkb, wvb, wqi, wwi, wki, wgta, wgtb = (grp(g) for g in range(12))
    wsm = jnp.concatenate([wkb, wvb, wqi, wki, wwi, jnp.zeros((d, SM_WIDTH - SM_USED), BF16)], axis=1)

    x_all = jnp.concatenate([x_prompt.reshape(n_p, d), x_sample.reshape(n_s, d)], axis=0)
    qa, f, ia, sg, qb, sm = _proj(x_all, wqa, wfa, wia, wg, wqb, wsm, lb_logits, PROJ_TM)

    gn = gn_a[0].reshape(1, DV_A)
    oa_p, sfin_p = _hgrn(qa, f, ia, sg, gn, jnp.zeros((bsz, H_A, DK_A, DV_A), F32),
                         bsz=bsz, seq=seq, row0=0, tl=HGRN_TL)
    oa_s, sfin_s = _hgrn(qa, f, ia, sg, gn, state_hgrn[0], bsz=dbsz, seq=dseq, row0=n_p, tl=dseq)

    kb_p = sm[:n_p, SM_KB:SM_KB + DH_B].reshape(bsz, seq, DH_B)
    vb_p = sm[:n_p, SM_VB:SM_VB + DH_B].reshape(bsz, seq, DH_B)
    ki_p = sm[:n_p, SM_KI:SM_KI + D_I].reshape(bsz, seq, D_I)
    kb_s = sm[n_p:, SM_KB:SM_KB + DH_B].reshape(dbsz, dseq, DH_B)
    vb_s = sm[n_p:, SM_VB:SM_VB + DH_B].reshape(dbsz, dseq, DH_B)
    ki_s = sm[n_p:, SM_KI:SM_KI + D_I].reshape(dbsz, dseq, D_I)
    lim_p = (np.arange(seq) // CHUNK + 1) * CHUNK
    ob_p = _dsa(qb, sm, kb_p.astype(BF16), vb_p.astype(BF16), ki_p.astype(BF16), lim_p,
                bsz=bsz, seq=seq, row0=0, tq=DSA_TQ, n_sel=min(TOPK_MAX, seq // 4))
    n_keys = past + dseq
    key_pad = -(-n_keys // LANES) * LANES - n_keys
    keys_s = lambda c, new: jnp.pad(jnp.concatenate([c[0].astype(BF16), new.astype(BF16)], axis=1),
                                    ((0, 0), (0, key_pad), (0, 0)))
    lim_s = np.full((dseq,), n_keys)
    ob_s = _dsa(qb, sm, keys_s(cache_k, kb_s), keys_s(cache_v, vb_s), keys_s(cache_kidx, ki_s), lim_s,
                bsz=dbsz, seq=dseq, row0=n_p, tq=dseq, n_sel=min(TOPK_MAX, n_keys // 4))

    oa = jnp.concatenate([oa_p, oa_s], axis=0)
    ob = jnp.concatenate([ob_p, ob_s], axis=0)
    wrh, wrl = _split_bf16(w_router[0].T)
    x1, eidx, gate, counts = _mix(
        oa, ob, x_all, w_oa[0].astype(BF16), w_ob[0].astype(BF16), wgta, wgtb, w_out[0].astype(BF16),
        ln1_g[0].reshape(1, d), ln1_b[0].reshape(1, d), wrh, wrl, b_router[0].reshape(N_EXPERTS, 1),
        tm=MIX_TM, alpha=alpha)

    n_rows = -(-(n * TOP_K + N_EXPERTS * (MOE_BM - 1)) // MOE_BM) * MOE_BM
    dest, blk_e = _route(eidx, counts, tl=ROUTE_TL, sub=MOE_TM, bm=MOE_BM, nb=n_rows // MOE_BM)
    xs = _dispatch(dest, x1, n_rows, tm=MOE_TM)
    ys = _experts(blk_e[0, :n_rows // MOE_BM], xs, w_up[0], b_up[0], w_down[0], b_down[0], bm=MOE_BM)
    out = _combine(dest, ys, gate, x1, ln2_g[0].reshape(1, d), ln2_b[0].reshape(1, d), tm=MOE_TM, alpha=alpha)

    return (out[:n_p].reshape(bsz, seq, d), out[n_p:].reshape(dbsz, dseq, d),
            kb_p[None], vb_p[None], ki_p[None], sfin_p[None],
            kb_s[None], vb_s[None], ki_s[None], sfin_s[None])
```

```python
import functools

import jax
import jax.numpy as jnp
import numpy as np
from jax import lax
from jax.experimental import pallas as pl
from jax.experimental.pallas import tpu as pltpu

F32 = jnp.float32
BF16 = jnp.bfloat16
I32 = jnp.int32

CHUNK = 64
H_A = 8
DK_A = 128
DV_A = 128
HGRN_BLOCK = 16
H_B = 8
DH_B = 128
H_I = 8
D_I = 64
TOPK_MAX = 256
N_EXPERTS = 32
TOP_K = 4
SWIGLU_ALPHA = 1.702
SWIGLU_LIMIT = 7.0
EPS = 1e-5

LANES = 128
SUBLANES = 8
VMEM_LIMIT_BYTES = 56 * 1024 * 1024

INT_MIN = -(2 ** 31)


def _cparams(sem):
    return pltpu.CompilerParams(dimension_semantics=sem, vmem_limit_bytes=VMEM_LIMIT_BYTES)


def _sigmoid(x):
    return 1.0 / (1.0 + jnp.exp(-x))


def _full_spec(shape):
    nd = len(shape)
    return pl.BlockSpec(shape, lambda *_: (0,) * nd)


def _layer_norm(x, g, b):
    mu = jnp.mean(x, axis=-1, keepdims=True)
    xc = x - mu
    var = jnp.mean(xc * xc, axis=-1, keepdims=True)
    return xc * lax.rsqrt(var + EPS) * g + b


def _proj_kernel(x_ref, wqa_ref, wfa_ref, wia_ref, wga_ref, wqb_ref, wsm_ref, lbl_ref,
                 qa_ref, f_ref, ia_ref, sg_ref, qb_ref, sm_ref):
    xb = x_ref[...].astype(BF16)
    dot = lambda w_ref: jnp.dot(xb, w_ref[...], preferred_element_type=F32)
    qa_ref[...] = dot(wqa_ref)
    lbl = lbl_ref[...]
    lbe = jnp.exp(lbl - jnp.max(lbl, axis=0, keepdims=True))
    lb = lbe[0:1, :] / jnp.sum(lbe, axis=0, keepdims=True)
    f_ref[...] = lb + (1.0 - lb) * _sigmoid(dot(wfa_ref))
    ia_ref[...] = dot(wia_ref)
    g = dot(wga_ref)
    sg_ref[...] = g * _sigmoid(g)
    qb_ref[...] = dot(wqb_ref).astype(BF16)
    sm_ref[...] = dot(wsm_ref)


def _proj(x, wqa, wfa, wia, wga, wqb, wsm, lb_logits, tm):
    n, d = x.shape
    nsm = wsm.shape[1]
    row = lambda w: pl.BlockSpec((tm, w), lambda i: (i, 0))
    wspec = lambda w: pl.BlockSpec(w.shape, lambda i: (0, 0))
    return pl.pallas_call(
        _proj_kernel,
        grid=(n // tm,),
        in_specs=[row(d), wspec(wqa), wspec(wfa), wspec(wia), wspec(wga), wspec(wqb), wspec(wsm),
                  wspec(lb_logits)],
        out_specs=[row(d), row(d), row(d), row(d), row(d), row(nsm)],
        out_shape=[jax.ShapeDtypeStruct((n, d), F32)] * 4
        + [jax.ShapeDtypeStruct((n, d), BF16), jax.ShapeDtypeStruct((n, nsm), F32)],
        compiler_params=_cparams(("arbitrary",)),
        name="proj",
    )(x, wqa, wfa, wia, wga, wqb, wsm, lb_logits)


def _block_cumsum(a, row_in_block):
    s = 1
    while s < HGRN_BLOCK:
        a = a + jnp.where(row_in_block >= s, pltpu.roll(a, s, 0), 0.0)
        s *= 2
    return a


def _hgrn_kernel(qa_ref, f_ref, ia_ref, sg_ref, gn_ref, s0_ref, o_ref, sfin_ref,
                 s_scr, qd_scr, qm_scr, km_scr, kl_scr, v_scr, dl_scr, o_scr):
    j = pl.program_id(1)
    tl = qa_ref.shape[0]
    nblk = tl // HGRN_BLOCK
    half = HGRN_BLOCK // 2

    @pl.when(j == 0)
    def _():
        for h in range(H_A):
            s_scr[h] = s0_ref[0, h].T

    f = f_ref[...]
    q = qa_ref[...]
    d = f.shape[1]
    row_in_block = lax.broadcasted_iota(I32, (tl, d), 0) % HGRN_BLOCK
    cum = _block_cumsum(jnp.log(f), row_in_block)
    cum3 = cum.reshape(nblk, HGRN_BLOCK, d)
    mid = jnp.broadcast_to(cum3[:, half:half + 1, :], cum3.shape).reshape(tl, d)
    last3 = cum3[:, HGRN_BLOCK - 1:HGRN_BLOCK, :]
    last = jnp.broadcast_to(last3, cum3.shape).reshape(tl, d)
    k = 1.0 - f
    qd_scr[...] = (q * jnp.exp(cum)).astype(BF16)
    qm_scr[...] = (q * jnp.exp(cum - mid)).astype(BF16)
    km_scr[...] = (k * jnp.exp(mid - cum)).astype(BF16)
    kl_scr[...] = (k * jnp.exp(last - cum)).astype(BF16)
    dl_scr[...] = jnp.exp(last3.reshape(nblk, d))
    v_scr[...] = ia_ref[...].astype(BF16)

    nt = (((1,), (1,)), ((), ()))
    ri = lax.broadcasted_iota(I32, (tl, tl), 0)
    ci = lax.broadcasted_iota(I32, (tl, tl), 1)
    keep = (ri // HGRN_BLOCK == ci // HGRN_BLOCK) & (ri >= ci)
    for h in range(H_A):
        cols = slice(h * DK_A, (h + 1) * DK_A)
        vcols = slice(h * DV_A, (h + 1) * DV_A)
        att = lax.dot_general(qm_scr[:, cols], km_scr[:, cols], nt, preferred_element_type=F32)
        att = jnp.where(keep, att, 0.0).astype(BF16)
        o_scr[:, vcols] = jnp.dot(att, v_scr[:, vcols], preferred_element_type=F32)

    def block(b, carry):
        r0 = pl.multiple_of(b * HGRN_BLOCK, HGRN_BLOCK)
        rows = pl.ds(r0, HGRN_BLOCK)
        dl = dl_scr[pl.ds(b, 1), :]
        for h in range(H_A):
            cols = slice(h * DK_A, (h + 1) * DK_A)
            vcols = slice(h * DV_A, (h + 1) * DV_A)
            st_h = s_scr[h]
            o_scr[rows, vcols] += lax.dot_general(qd_scr[rows, cols], st_h.astype(BF16), nt,
                                                  preferred_element_type=F32)
            upd_t = lax.dot_general(v_scr[rows, vcols], kl_scr[rows, cols], (((0,), (0,)), ((), ())),
                                    preferred_element_type=F32)
            s_scr[h] = dl[:, cols] * st_h + upd_t
        return carry

    lax.fori_loop(0, nblk, block, 0, unroll=2 if nblk % 2 == 0 else 1)

    gn = gn_ref[...]
    for h in range(H_A):
        vcols = slice(h * DV_A, (h + 1) * DV_A)
        o = o_scr[:, vcols]
        o = o * lax.rsqrt(jnp.mean(o * o, axis=-1, keepdims=True) + EPS) * gn
        o_ref[:, vcols] = (o * sg_ref[:, vcols]).astype(BF16)

    @pl.when(j == pl.num_programs(1) - 1)
    def _():
        for h in range(H_A):
            sfin_ref[0, h] = s_scr[h].T


def _hgrn(qa, f, ia, sg, gn, s0, *, bsz, seq, row0, tl):
    d = qa.shape[1]
    nt = seq // tl
    blk0 = row0 // tl
    row = pl.BlockSpec((tl, d), lambda b, j: (blk0 + b * nt + j, 0))
    orow = pl.BlockSpec((tl, d), lambda b, j: (b * nt + j, 0))
    sspec = pl.BlockSpec((1, H_A, DK_A, DV_A), lambda b, j: (b, 0, 0, 0))
    return pl.pallas_call(
        _hgrn_kernel,
        grid=(bsz, nt),
        in_specs=[row, row, row, row, pl.BlockSpec((1, DV_A), lambda b, j: (0, 0)), sspec],
        out_specs=[orow, sspec],
        out_shape=[jax.ShapeDtypeStruct((bsz * seq, d), BF16),
                   jax.ShapeDtypeStruct((bsz, H_A, DK_A, DV_A), F32)],
        scratch_shapes=[pltpu.VMEM((H_A, DK_A, DV_A), F32)]
        + [pltpu.VMEM((tl, d), BF16)] * 5
        + [pltpu.VMEM((tl // HGRN_BLOCK, d), F32), pltpu.VMEM((tl, d), F32)],
        compiler_params=_cparams(("arbitrary", "arbitrary")),
        name="hgrn",
    )(qa, f, ia, sg, gn, s0)


SM_KB, SM_VB, SM_QI, SM_KI, SM_WI = 0, DH_B, 2 * DH_B, 2 * DH_B + H_I * D_I, 2 * DH_B + H_I * D_I + D_I
SM_USED = SM_WI + H_I
SM_WIDTH = -(-SM_USED // LANES) * LANES
INDEX_SCALE = (H_I * D_I) ** -0.5
NEG_INF = float("-inf")


F32_EXP_MASK = 0x7F800000
F32_MIN_NORMAL = 0x00800000
SEARCH_UNROLL = 4


def _key_to_float(u):
    key = u ^ jnp.int32(INT_MIN)
    bits = jnp.where(key < 0, key ^ jnp.int32(0x7FFFFFFF), key)
    below_neg_inf = (bits < 0) & ((bits & jnp.int32(0x7FFFFFFF)) > jnp.int32(F32_EXP_MASK))
    return jnp.where(below_neg_inf, NEG_INF, lax.bitcast_convert_type(bits, F32))


def _positive_bits_to_float(bits):
    return jnp.where(bits < jnp.int32(F32_MIN_NORMAL), 0.0, lax.bitcast_convert_type(bits, F32))


def _count(mask):
    return jnp.sum(jnp.where(mask, 1.0, 0.0), axis=1, keepdims=True)


def _select_bias(score_scr, bias_scr, adm, s, n_sel):
    tq = score_scr.shape[0]

    def resolve_bit(i, t_u, cnt_t):
        cand_u = t_u | lax.shift_left(jnp.int32(1), 31 - i)
        cnt = _count(score_scr[:, :s] >= _key_to_float(cand_u))
        ok = cnt >= n_sel
        return jnp.where(ok, cand_u, t_u), jnp.where(ok, cnt, cnt_t)

    def unresolved(c):
        i, _, cnt_t = c
        return jnp.logical_and(i < 32, jnp.any(cnt_t != n_sel))

    def resolve_bits(c):
        i, t_u, cnt_t = c
        for b in range(SEARCH_UNROLL):
            t_u, cnt_t = resolve_bit(i + b, t_u, cnt_t)
        return i + SEARCH_UNROLL, t_u, cnt_t

    _, t_u, cnt_t = lax.while_loop(unresolved, resolve_bits,
                                   (jnp.int32(0), jnp.zeros((tq, 1), I32), jnp.full((tq, 1), float(s), F32)))
    t_f = _key_to_float(t_u)
    score = score_scr[:, :s]
    bias_scr[:, :s] = jnp.where((score >= t_f) & adm, 0.0, NEG_INF)

    @pl.when(jnp.any(cnt_t > n_sel))
    def _():
        above = score >= _key_to_float(t_u + 1)
        bucket = (score >= t_f) & jnp.logical_not(above)
        need = n_sel - _count(above)
        off = jnp.where(bucket, score - jnp.where(t_f == NEG_INF, 0.0, t_f), -1.0)

        def resolve_offset_bit(i, r_bits):
            cand = r_bits | lax.shift_left(jnp.int32(1), 30 - i)
            ok = _count(off >= _positive_bits_to_float(cand)) >= need
            return jnp.where(ok, cand, r_bits)

        r_bits = lax.fori_loop(0, 31, resolve_offset_bit, jnp.zeros((tq, 1), I32))
        above2 = off >= _positive_bits_to_float(r_bits + 1)
        tie = (off >= _positive_bits_to_float(r_bits)) & jnp.logical_not(above2)
        need2 = need - _count(above2)
        upper = (lax.broadcasted_iota(I32, (LANES, LANES), 0)
                 <= lax.broadcasted_iota(I32, (LANES, LANES), 1)).astype(BF16)
        carry = jnp.zeros((tq, 1), F32)
        for c in range(s // LANES):
            cs = slice(c * LANES, (c + 1) * LANES)
            rank = jnp.dot(jnp.where(tie[:, cs], 1.0, 0.0).astype(BF16), upper, preferred_element_type=F32) + carry
            sel = (above[:, cs] | above2[:, cs] | (tie[:, cs] & (rank <= need2))) & adm[:, cs]
            bias_scr[:, cs] = jnp.where(sel, 0.0, NEG_INF)
            carry = rank[:, LANES - 1:LANES]


def _dsa_tile(qb_ref, sm_ref, k_ref, v_ref, ki_ref, lim_ref, o_ref, score_scr, bias_scr, *, s, search, n_sel):
    tq = qb_ref.shape[0]
    nt = (((1,), (1,)), ((), ()))
    adm = lax.broadcasted_iota(I32, (tq, s), 1) < lim_ref[...]

    if search:
        kidx = ki_ref[0, :s, :]
        score = jnp.zeros((tq, s), F32)
        for h in range(H_I):
            qih = sm_ref[:, SM_QI + h * D_I:SM_QI + (h + 1) * D_I].astype(BF16)
            sh = lax.dot_general(qih, kidx, nt, preferred_element_type=F32)
            wih = sm_ref[:, SM_WI + h:SM_WI + h + 1] * INDEX_SCALE
            score = score + jnp.maximum(sh, 0.0) * wih
        score_scr[:, :s] = jnp.where(adm, score, NEG_INF)
        _select_bias(score_scr, bias_scr, adm, s, n_sel)
    else:
        bias_scr[:, :s] = jnp.where(adm, 0.0, NEG_INF)

    kk = k_ref[0, :s, :]
    vv = v_ref[0, :s, :]
    for h in range(H_B):
        hs = slice(h * DH_B, (h + 1) * DH_B)
        logits = lax.dot_general(qb_ref[:, hs], kk, nt, preferred_element_type=F32) * (DH_B ** -0.5)
        logits = logits + bias_scr[:, :s]
        p = jnp.exp(logits - jnp.max(logits, axis=1, keepdims=True))
        denom = jnp.sum(p, axis=1, keepdims=True)
        o = jnp.dot(p.astype(BF16), vv, preferred_element_type=F32)
        o_ref[:, hs] = (o / denom).astype(BF16)


DSA_KEY_GRAN = 256


def _dsa(qb, sm, k16, v16, ki16, limits, *, bsz, seq, row0, tq, n_sel):
    d = qb.shape[1]
    n_keys = k16.shape[1]
    nt = seq // tq
    blk0 = row0 // tq
    limits = np.asarray(limits, np.int32)
    tile_max = limits.reshape(nt, tq).max(axis=1)
    per_tile = [(int(min(n_keys, -(-m // DSA_KEY_GRAN) * DSA_KEY_GRAN)), bool(m > n_sel)) for m in tile_max]
    lim2d = jnp.asarray(limits.reshape(seq, 1))
    outs, lo = [], 0
    while lo < nt:
        hi = lo
        while hi + 1 < nt and per_tile[hi + 1] == per_tile[lo]:
            hi += 1
        s, search = per_tile[lo]
        ntv = hi - lo + 1
        row = lambda w, lo=lo: pl.BlockSpec((tq, w), lambda b, j: (blk0 + b * nt + lo + j, 0))
        keys = lambda w, s=s: pl.BlockSpec((1, s, w), lambda b, j: (b, 0, 0))
        out = pl.pallas_call(
            functools.partial(_dsa_tile, s=s, search=search, n_sel=n_sel),
            grid=(bsz, ntv),
            in_specs=[row(d), row(sm.shape[1]), keys(DH_B), keys(DH_B), keys(D_I),
                      pl.BlockSpec((tq, 1), lambda b, j, lo=lo: (lo + j, 0))],
            out_specs=pl.BlockSpec((tq, d), lambda b, j, ntv=ntv: (b * ntv + j, 0)),
            out_shape=jax.ShapeDtypeStruct((bsz * ntv * tq, d), BF16),
            scratch_shapes=[pltpu.VMEM((tq, s), F32), pltpu.VMEM((tq, s), F32)],
            compiler_params=_cparams(("arbitrary", "arbitrary")),
            name="dsa",
        )(qb, sm, k16, v16, ki16, lim2d)
        outs.append(out.reshape(bsz, ntv * tq, d))
        lo = hi + 1
    return jnp.concatenate(outs, axis=1).reshape(bsz * seq, d)


def _split_bf16(a):
    hi = a.astype(BF16)
    return hi, (a - hi.astype(F32)).astype(BF16)


def _mix_kernel(oa_ref, ob_ref, x_ref, woa_ref, wob_ref, wga_ref, wgb_ref, wout_ref, g1_ref, b1_ref,
                wrh_ref, wrl_ref, br_ref, x1_ref, eidx_ref, gate_ref, cnt_ref, *, alpha):
    i = pl.program_id(0)
    x = x_ref[...]
    xb = x.astype(BF16)
    dot = lambda a, w_ref: jnp.dot(a, w_ref[...], preferred_element_type=F32)
    y_a = dot(oa_ref[...], woa_ref)
    y_b = dot(ob_ref[...], wob_ref)
    merged = _sigmoid(dot(xb, wga_ref)) * y_a + _sigmoid(dot(xb, wgb_ref)) * y_b
    mixed = dot(merged.astype(BF16), wout_ref)
    x1 = _layer_norm(alpha * x + mixed, g1_ref[...], b1_ref[...])
    x1_ref[...] = x1

    nt = (((1,), (1,)), ((), ()))
    xh, xl = _split_bf16(x1)
    wh, wl = wrh_ref[...], wrl_ref[...]
    logits = (lax.dot_general(wh, xh, nt, preferred_element_type=F32)
              + lax.dot_general(wh, xl, nt, preferred_element_type=F32)
              + lax.dot_general(wl, xh, nt, preferred_element_type=F32)) + br_ref[...]
    ne, tm = logits.shape
    erow = lax.broadcasted_iota(I32, (ne, tm), 0)
    vals, idxs = [], []
    for _ in range(TOP_K):
        m = jnp.max(logits, axis=0, keepdims=True)
        idx = jnp.min(jnp.where(logits == m, erow, ne), axis=0, keepdims=True)
        vals.append(m)
        idxs.append(idx)
        logits = jnp.where(erow == idx, NEG_INF, logits)
    ex = [jnp.exp(v - vals[0]) for v in vals]
    denom = ex[0] + ex[1] + ex[2] + ex[3]
    pad = SUBLANES - TOP_K
    eidx = jnp.concatenate(idxs + [jnp.zeros((pad, tm), I32)], axis=0)
    eidx_ref[...] = eidx
    gate_ref[...] = jnp.concatenate([e / denom for e in ex] + [jnp.zeros((pad, tm), F32)], axis=0)

    onehot = jnp.zeros((ne, tm), F32)
    for idx in idxs:
        onehot = onehot + jnp.where(erow == idx, 1.0, 0.0)
    tile_cnt = jnp.broadcast_to(jnp.sum(onehot, axis=1, keepdims=True), cnt_ref.shape)

    @pl.when(i == 0)
    def _():
        cnt_ref[...] = tile_cnt

    @pl.when(i > 0)
    def _():
        cnt_ref[...] = cnt_ref[...] + tile_cnt


def _mix(oa, ob, x, woa, wob, wga, wgb, wout, g1, b1, wrh, wrl, br, *, tm, alpha):
    n, d = x.shape
    ne = wrh.shape[0]
    row = lambda w: pl.BlockSpec((tm, w), lambda i: (i, 0))
    col = pl.BlockSpec((SUBLANES, tm), lambda i: (0, i))
    full = lambda a: pl.BlockSpec(a.shape, lambda i: (0,) * a.ndim)
    return pl.pallas_call(
        functools.partial(_mix_kernel, alpha=alpha),
        grid=(n // tm,),
        in_specs=[row(d), row(d), row(d)] + [full(a) for a in (woa, wob, wga, wgb, wout, g1, b1, wrh, wrl, br)],
        out_specs=[row(d), col, col, pl.BlockSpec((ne, LANES), lambda i: (0, 0))],
        out_shape=[jax.ShapeDtypeStruct((n, d), F32), jax.ShapeDtypeStruct((SUBLANES, n), I32),
                   jax.ShapeDtypeStruct((SUBLANES, n), F32), jax.ShapeDtypeStruct((ne, LANES), F32)],
        compiler_params=_cparams(("arbitrary",)),
        name="mix",
    )(oa, ob, x, woa, wob, wga, wgb, wout, g1, b1, wrh, wrl, br)


def _sublane_cumsum(a):
    n = a.shape[0]
    row = lax.broadcasted_iota(I32, a.shape, 0)
    s = 1
    while s < n:
        a = a + jnp.where(row >= s, pltpu.roll(a, s, 0), 0.0)
        s *= 2
    return a


def _route_kernel(eidx_ref, cnt_ref, dest_ref, blke_ref, carry_scr, *, bm, sub):
    i = pl.program_id(0)
    ne = cnt_ref.shape[0]
    tl = eidx_ref.shape[1]

    @pl.when(i == 0)
    def _():
        carry_scr[...] = jnp.zeros_like(carry_scr)

    counts = cnt_ref[...]
    padded = jnp.ceil(counts / bm) * bm
    pend = _sublane_cumsum(padded)
    pstart = (pend - padded)[:, 0:1]

    @pl.when(i == 0)
    def _():
        nb = blke_ref.shape[1]
        first_row = (lax.broadcasted_iota(I32, (ne, nb), 1) * bm).astype(F32)
        below = jnp.sum(jnp.where(pend[:, 0:1] <= first_row, 1.0, 0.0), axis=0, keepdims=True)
        blke_ref[...] = jnp.minimum(below, ne - 1.0).astype(I32)

    eidx = eidx_ref[...]
    erow = lax.broadcasted_iota(I32, (ne, tl), 0)
    hot = [jnp.where(erow == eidx[k:k + 1, :], 1.0, 0.0) for k in range(TOP_K)]
    onehot = hot[0] + hot[1] + hot[2] + hot[3]
    before = (lax.broadcasted_iota(I32, (tl, tl), 0) < lax.broadcasted_iota(I32, (tl, tl), 1)).astype(BF16)
    base = jnp.dot(onehot.astype(BF16), before, preferred_element_type=F32) + carry_scr[:, 0:1] + pstart
    dest = jnp.concatenate([jnp.sum(hk * base, axis=0, keepdims=True) for hk in hot], axis=0).astype(I32)
    for c in range(tl // sub):
        dest_ref[c] = dest[:, c * sub:(c + 1) * sub]
    carry_scr[...] = carry_scr[...] + jnp.sum(onehot, axis=1, keepdims=True)


def _route(eidx, counts, *, tl, sub, bm, nb):
    n = eidx.shape[1]
    ne = counts.shape[0]
    nb_pad = -(-nb // LANES) * LANES
    return pl.pallas_call(
        functools.partial(_route_kernel, bm=bm, sub=sub),
        grid=(n // tl,),
        in_specs=[pl.BlockSpec((SUBLANES, tl), lambda i: (0, i)), pl.BlockSpec(counts.shape, lambda i: (0, 0))],
        out_specs=[pl.BlockSpec((tl // sub, TOP_K, sub), lambda i: (i, 0, 0)),
                   pl.BlockSpec((1, nb_pad), lambda i: (0, 0))],
        out_shape=[jax.ShapeDtypeStruct((n // sub, TOP_K, sub), I32), jax.ShapeDtypeStruct((1, nb_pad), I32)],
        scratch_shapes=[pltpu.VMEM((ne, LANES), F32)],
        compiler_params=_cparams(("arbitrary",)),
        name="route",
    )(eidx, counts)


def _dispatch_kernel(dest_ref, x_ref, zeros_ref, xs_ref, sem):
    del zeros_ref
    tm = x_ref.shape[0]

    def row_copy(t, k):
        d = dest_ref[0, k, t]
        return pltpu.make_async_copy(x_ref.at[pl.ds(t, 1), :], xs_ref.at[pl.ds(d, 1), :], sem)

    def start(t, carry):
        for k in range(TOP_K):
            row_copy(t, k).start()
        return carry

    def wait(t, carry):
        for k in range(TOP_K):
            row_copy(t, k).wait()
        return carry

    lax.fori_loop(0, tm, start, 0)
    lax.fori_loop(0, tm, wait, 0)


def _dispatch(dest, x1, n_rows, *, tm):
    n, d = x1.shape
    zeros = jnp.zeros((n_rows, d), x1.dtype)
    return pl.pallas_call(
        _dispatch_kernel,
        grid=(n // tm,),
        in_specs=[pl.BlockSpec((1, TOP_K, tm), lambda i: (i, 0, 0), memory_space=pltpu.SMEM),
                  pl.BlockSpec((tm, d), lambda i: (i, 0)),
                  pl.BlockSpec(memory_space=pl.ANY)],
        out_specs=pl.BlockSpec(memory_space=pl.ANY),
        out_shape=jax.ShapeDtypeStruct((n_rows, d), x1.dtype),
        scratch_shapes=[pltpu.SemaphoreType.DMA(())],
        input_output_aliases={2: 0},
        compiler_params=_cparams(("arbitrary",)),
        name="dispatch",
    )(dest, x1, zeros)


def _experts_kernel(blke_ref, xs_ref, wu_ref, bu_ref, wd_ref, bd_ref, ys_ref, wu16, wd16):
    i = pl.program_id(0)
    dff = wd_ref.shape[1]

    @pl.when(jnp.logical_or(i == 0, blke_ref[i] != blke_ref[jnp.maximum(i - 1, 0)]))
    def _():
        wu16[...] = wu_ref[0].astype(BF16)
        wd16[...] = wd_ref[0].astype(BF16)

    h = jnp.dot(xs_ref[...].astype(BF16), wu16[...], preferred_element_type=F32) + bu_ref[0]
    glu = jnp.minimum(h[:, :dff], SWIGLU_LIMIT)
    lin = jnp.clip(h[:, dff:], -SWIGLU_LIMIT, SWIGLU_LIMIT)
    act = glu * _sigmoid(SWIGLU_ALPHA * glu) * (lin + 1.0)
    ys_ref[...] = jnp.dot(act.astype(BF16), wd16[...], preferred_element_type=F32) + bd_ref[0]


def _experts(blk_e, xs, wu, bu, wd, bd, *, bm):
    n_rows, d = xs.shape
    ne, _, dff2 = wu.shape
    dff = wd.shape[1]
    grid_spec = pltpu.PrefetchScalarGridSpec(
        num_scalar_prefetch=1,
        grid=(n_rows // bm,),
        in_specs=[pl.BlockSpec((bm, d), lambda i, e: (i, 0)),
                  pl.BlockSpec((1, d, dff2), lambda i, e: (e[i], 0, 0)),
                  pl.BlockSpec((1, 1, dff2), lambda i, e: (e[i], 0, 0)),
                  pl.BlockSpec((1, dff, d), lambda i, e: (e[i], 0, 0)),
                  pl.BlockSpec((1, 1, d), lambda i, e: (e[i], 0, 0))],
        out_specs=pl.BlockSpec((bm, d), lambda i, e: (i, 0)),
        scratch_shapes=[pltpu.VMEM((d, dff2), BF16), pltpu.VMEM((dff, d), BF16)],
    )
    return pl.pallas_call(
        _experts_kernel,
        grid_spec=grid_spec,
        out_shape=jax.ShapeDtypeStruct((n_rows, d), F32),
        compiler_params=_cparams(("arbitrary",)),
        name="experts",
    )(blk_e, xs, wu, bu.reshape(ne, 1, dff2), wd, bd.reshape(ne, 1, d))


def _combine_kernel(dest_ref, ys_ref, gate_ref, x1_ref, g2_ref, b2_ref, out_ref, buf, sem, *, alpha):
    tm = x1_ref.shape[0]

    def row_copy(t, k):
        d = dest_ref[0, k, t]
        return pltpu.make_async_copy(ys_ref.at[pl.ds(d, 1), :], buf.at[k, pl.ds(t, 1), :], sem)

    def start(t, carry):
        for k in range(TOP_K):
            row_copy(t, k).start()
        return carry

    def wait(t, carry):
        for k in range(TOP_K):
            row_copy(t, k).wait()
        return carry

    lax.fori_loop(0, tm, start, 0)
    lax.fori_loop(0, tm, wait, 0)

    gate = gate_ref[...].T
    y = buf[0] * gate[:, 0:1]
    for k in range(1, TOP_K):
        y = y + buf[k] * gate[:, k:k + 1]
    out_ref[...] = _layer_norm(alpha * x1_ref[...] + y, g2_ref[...], b2_ref[...])


def _combine(dest, ys, gate, x1, g2, b2, *, tm, alpha):
    n, d = x1.shape
    return pl.pallas_call(
        functools.partial(_combine_kernel, alpha=alpha),
        grid=(n // tm,),
        in_specs=[pl.BlockSpec((1, TOP_K, tm), lambda i: (i, 0, 0), memory_space=pltpu.SMEM),
                  pl.BlockSpec(memory_space=pl.ANY),
                  pl.BlockSpec((SUBLANES, tm), lambda i: (0, i)),
                  pl.BlockSpec((tm, d), lambda i: (i, 0)),
                  pl.BlockSpec((1, d), lambda i: (0, 0)),
                  pl.BlockSpec((1, d), lambda i: (0, 0))],
        out_specs=pl.BlockSpec((tm, d), lambda i: (i, 0)),
        out_shape=jax.ShapeDtypeStruct((n, d), F32),
        scratch_shapes=[pltpu.VMEM((TOP_K, tm, d), F32), pltpu.SemaphoreType.DMA(())],
        compiler_params=_cparams(("arbitrary",)),
        name="combine",
    )(dest, ys, gate, x1, g2, b2)


PROJ_TM = 256
HGRN_TL = 256
DSA_TQ = 128
MIX_TM = 256
ROUTE_TL = 1280
MOE_TM = 256
MOE_BM = 256


def kernel(x_prompt, x_sample, cache_k, cache_v, cache_kidx, state_hgrn, w_in, lb_logits, gn_a, w_oa, w_ob,
           w_out, ln1_g, ln1_b, w_router, b_router, w_up, b_up, w_down, b_down, ln2_g, ln2_b):
    depth = w_in.shape[0]
    assert depth == 1
    bsz, seq, d = x_prompt.shape
    dbsz, dseq, _ = x_sample.shape
    past = cache_k.shape[2]
    n_p, n_s = bsz * seq, dbsz * dseq
    n = n_p + n_s
    alpha = (2 * depth) ** 0.25

    sizes = (H_A * DK_A, H_A * DK_A, H_A * DV_A, H_A * DV_A, H_B * DH_B, DH_B, DH_B, H_I * D_I, H_I, D_I, d, d)
    offs = np.concatenate([[0], np.cumsum(sizes)])
    w = w_in[0].astype(BF16)
    grp = lambda g: w[:, offs[g]:offs[g + 1]]
    wqa, wfa, wia, wg, wqb, wkb, wvb, wqi, wwi, wki, wgta, wgtb = (grp(g) for g in range(12))
    wsm = jnp.concatenate([wkb, wvb, wqi, wki, wwi, jnp.zeros((d, SM_WIDTH - SM_USED), BF16)], axis=1)

    x_all = jnp.concatenate([x_prompt.reshape(n_p, d), x_sample.reshape(n_s, d)], axis=0)
    qa, f, ia, sg, qb, sm = _proj(x_all, wqa, wfa, wia, wg, wqb, wsm, lb_logits, PROJ_TM)

    gn = gn_a[0].reshape(1, DV_A)
    oa_p, sfin_p = _hgrn(qa, f, ia, sg, gn, jnp.zeros((bsz, H_A, DK_A, DV_A), F32),
                         bsz=bsz, seq=seq, row0=0, tl=HGRN_TL)
    oa_s, sfin_s = _hgrn(qa, f, ia, sg, gn, state_hgrn[0], bsz=dbsz, seq=dseq, row0=n_p, tl=dseq)

    kb_p = sm[:n_p, SM_KB:SM_KB + DH_B].reshape(bsz, seq, DH_B)
    vb_p = sm[:n_p, SM_VB:SM_VB + DH_B].reshape(bsz, seq, DH_B)
    ki_p = sm[:n_p, SM_KI:SM_KI + D_I].reshape(bsz, seq, D_I)
    kb_s = sm[n_p:, SM_KB:SM_KB + DH_B].reshape(dbsz, dseq, DH_B)
    vb_s = sm[n_p:, SM_VB:SM_VB + DH_B].reshape(dbsz, dseq, DH_B)
    ki_s = sm[n_p:, SM_KI:SM_KI + D_I].reshape(dbsz, dseq, D_I)
    lim_p = (np.arange(seq) // CHUNK + 1) * CHUNK
    ob_p = _dsa(qb, sm, kb_p.astype(BF16), vb_p.astype(BF16), ki_p.astype(BF16), lim_p,
                bsz=bsz, seq=seq, row0=0, tq=DSA_TQ, n_sel=min(TOPK_MAX, seq // 4))
    n_keys = past + dseq
    key_pad = -(-n_keys // LANES) * LANES - n_keys
    keys_s = lambda c, new: jnp.pad(jnp.concatenate([c[0].astype(BF16), new.astype(BF16)], axis=1),
                                    ((0, 0), (0, key_pad), (0, 0)))
    lim_s = np.full((dseq,), n_keys)
    ob_s = _dsa(qb, sm, keys_s(cache_k, kb_s), keys_s(cache_v, vb_s), keys_s(cache_kidx, ki_s), lim_s,
                bsz=dbsz, seq=dseq, row0=n_p, tq=dseq, n_sel=min(TOPK_MAX, n_keys // 4))

    oa = jnp.concatenate([oa_p, oa_s], axis=0)
    ob = jnp.concatenate([ob_p, ob_s], axis=0)
    wrh, wrl = _split_bf16(w_router[0].T)
    x1, eidx, gate, counts = _mix(
        oa, ob, x_all, w_oa[0].astype(BF16), w_ob[0].astype(BF16), wgta, wgtb, w_out[0].astype(BF16),
        ln1_g[0].reshape(1, d), ln1_b[0].reshape(1, d), wrh, wrl, b_router[0].reshape(N_EXPERTS, 1),
        tm=MIX_TM, alpha=alpha)

    n_rows = -(-(n * TOP_K + N_EXPERTS * (MOE_BM - 1)) // MOE_BM) * MOE_BM
    dest, blk_e = _route(eidx, counts, tl=ROUTE_TL, sub=MOE_TM, bm=MOE_BM, nb=n_rows // MOE_BM)
    xs = _dispatch(dest, x1, n_rows, tm=MOE_TM)
    ys = _experts(blk_e[0, :n_rows // MOE_BM], xs, w_up[0], b_up[0], w_down[0], b_down[0], bm=MOE_BM)
    out = _combine(dest, ys, gate, x1, ln2_g[0].reshape(1, d), ln2_b[0].reshape(1, d), tm=MOE_TM, alpha=alpha)

    return (out[:n_p].reshape(bsz, seq, d), out[n_p:].reshape(dbsz, dseq, d),
            kb_p[None], vb_p[None], ki_p[None], sfin_p[None],
            kb_s[None], vb_s[None], ki_s[None], sfin_s[None])
```

```python
import functools

import jax
import jax.numpy as jnp
import numpy as np
from jax import lax
from jax.experimental import pallas as pl
from jax.experimental.pallas import tpu as pltpu

F32 = jnp.float32
BF16 = jnp.bfloat16
I32 = jnp.int32

CHUNK = 64
H_A = 8
DK_A = 128
DV_A = 128
HGRN_BLOCK = 16
H_B = 8
DH_B = 128
H_I = 8
D_I = 64
TOPK_MAX = 256
N_EXPERTS = 32
TOP_K = 4
SWIGLU_ALPHA = 1.702
SWIGLU_LIMIT = 7.0
EPS = 1e-5

LANES = 128
SUBLANES = 8
VMEM_LIMIT_BYTES = 56 * 1024 * 1024

INT_MIN = -(2 ** 31)


def _cparams(sem):
    return pltpu.CompilerParams(dimension_semantics=sem, vmem_limit_bytes=VMEM_LIMIT_BYTES)


def _sigmoid(x):
    return 1.0 / (1.0 + jnp.exp(-x))


def _full_spec(shape):
    nd = len(shape)
    return pl.BlockSpec(shape, lambda *_: (0,) * nd)


def _layer_norm(x, g, b):
    mu = jnp.mean(x, axis=-1, keepdims=True)
    xc = x - mu
    var = jnp.mean(xc * xc, axis=-1, keepdims=True)
    return xc * lax.rsqrt(var + EPS) * g + b


def _proj_kernel(x_ref, wqa_ref, wfa_ref, wia_ref, wga_ref, wqb_ref, wsm_ref, lbl_ref,
                 qa_ref, f_ref, ia_ref, sg_ref, qb_ref, sm_ref):
    xb = x_ref[...].astype(BF16)
    dot = lambda w_ref: jnp.dot(xb, w_ref[...], preferred_element_type=F32)
    qa_ref[...] = dot(wqa_ref)
    lbl = lbl_ref[...]
    lbe = jnp.exp(lbl - jnp.max(lbl, axis=0, keepdims=True))
    lb = lbe[0:1, :] / jnp.sum(lbe, axis=0, keepdims=True)
    f_ref[...] = lb + (1.0 - lb) * _sigmoid(dot(wfa_ref))
    ia_ref[...] = dot(wia_ref)
    g = dot(wga_ref)
    sg_ref[...] = g * _sigmoid(g)
    qb_ref[...] = dot(wqb_ref).astype(BF16)
    sm_ref[...] = dot(wsm_ref)


def _proj(x, wqa, wfa, wia, wga, wqb, wsm, lb_logits, tm):
    n, d = x.shape
    nsm = wsm.shape[1]
    row = lambda w: pl.BlockSpec((tm, w), lambda i: (i, 0))
    wspec = lambda w: pl.BlockSpec(w.shape, lambda i: (0, 0))
    return pl.pallas_call(
        _proj_kernel,
        grid=(n // tm,),
        in_specs=[row(d), wspec(wqa), wspec(wfa), wspec(wia), wspec(wga), wspec(wqb), wspec(wsm),
                  wspec(lb_logits)],
        out_specs=[row(d), row(d), row(d), row(d), row(d), row(nsm)],
        out_shape=[jax.ShapeDtypeStruct((n, d), F32)] * 4
        + [jax.ShapeDtypeStruct((n, d), BF16), jax.ShapeDtypeStruct((n, nsm), F32)],
        compiler_params=_cparams(("arbitrary",)),
        name="proj",
    )(x, wqa, wfa, wia, wga, wqb, wsm, lb_logits)


def _block_cumsum(a, row_in_block):
    s = 1
    while s < HGRN_BLOCK:
        a = a + jnp.where(row_in_block >= s, pltpu.roll(a, s, 0), 0.0)
        s *= 2
    return a


def _hgrn_kernel(qa_ref, f_ref, ia_ref, sg_ref, gn_ref, s0_ref, o_ref, sfin_ref,
                 s_scr, qd_scr, qm_scr, km_scr, kl_scr, v_scr, dl_scr, o_scr):
    j = pl.program_id(1)
    tl = qa_ref.shape[0]
    nblk = tl // HGRN_BLOCK
    half = HGRN_BLOCK // 2

    @pl.when(j == 0)
    def _():
        for h in range(H_A):
            s_scr[h] = s0_ref[0, h].T

    f = f_ref[...]
    q = qa_ref[...]
    d = f.shape[1]
    row_in_block = lax.broadcasted_iota(I32, (tl, d), 0) % HGRN_BLOCK
    cum = _block_cumsum(jnp.log(f), row_in_block)
    cum3 = cum.reshape(nblk, HGRN_BLOCK, d)
    mid = jnp.broadcast_to(cum3[:, half:half + 1, :], cum3.shape).reshape(tl, d)
    last3 = cum3[:, HGRN_BLOCK - 1:HGRN_BLOCK, :]
    last = jnp.broadcast_to(last3, cum3.shape).reshape(tl, d)
    k = 1.0 - f
    qd_scr[...] = (q * jnp.exp(cum)).astype(BF16)
    qm_scr[...] = (q * jnp.exp(cum - mid)).astype(BF16)
    km_scr[...] = (k * jnp.exp(mid - cum)).astype(BF16)
    kl_scr[...] = (k * jnp.exp(last - cum)).astype(BF16)
    dl_scr[...] = jnp.exp(last3.reshape(nblk, d))
    v_scr[...] = ia_ref[...].astype(BF16)

    nt = (((1,), (1,)), ((), ()))
    ri = lax.broadcasted_iota(I32, (tl, tl), 0)
    ci = lax.broadcasted_iota(I32, (tl, tl), 1)
    keep = (ri // HGRN_BLOCK == ci // HGRN_BLOCK) & (ri >= ci)
    for h in range(H_A):
        cols = slice(h * DK_A, (h + 1) * DK_A)
        vcols = slice(h * DV_A, (h + 1) * DV_A)
        att = lax.dot_general(qm_scr[:, cols], km_scr[:, cols], nt, preferred_element_type=F32)
        att = jnp.where(keep, att, 0.0).astype(BF16)
        o_scr[:, vcols] = jnp.dot(att, v_scr[:, vcols], preferred_element_type=F32)

    def block(b, carry):
        r0 = pl.multiple_of(b * HGRN_BLOCK, HGRN_BLOCK)
        rows = pl.ds(r0, HGRN_BLOCK)
        dl = dl_scr[pl.ds(b, 1), :]
        for h in range(H_A):
            cols = slice(h * DK_A, (h + 1) * DK_A)
            vcols = slice(h * DV_A, (h + 1) * DV_A)
            st_h = s_scr[h]
            o_scr[rows, vcols] += lax.dot_general(qd_scr[rows, cols], st_h.astype(BF16), nt,
                                                  preferred_element_type=F32)
            upd_t = lax.dot_general(v_scr[rows, vcols], kl_scr[rows, cols], (((0,), (0,)), ((), ())),
                                    preferred_element_type=F32)
            s_scr[h] = dl[:, cols] * st_h + upd_t
        return carry

    lax.fori_loop(0, nblk, block, 0, unroll=2 if nblk % 2 == 0 else 1)

    gn = gn_ref[...]
    for h in range(H_A):
        vcols = slice(h * DV_A, (h + 1) * DV_A)
        o = o_scr[:, vcols]
        o = o * lax.rsqrt(jnp.mean(o * o, axis=-1, keepdims=True) + EPS) * gn
        o_ref[:, vcols] = (o * sg_ref[:, vcols]).astype(BF16)

    @pl.when(j == pl.num_programs(1) - 1)
    def _():
        for h in range(H_A):
            sfin_ref[0, h] = s_scr[h].T


def _hgrn(qa, f, ia, sg, gn, s0, *, bsz, seq, row0, tl):
    d = qa.shape[1]
    nt = seq // tl
    blk0 = row0 // tl
    row = pl.BlockSpec((tl, d), lambda b, j: (blk0 + b * nt + j, 0))
    orow = pl.BlockSpec((tl, d), lambda b, j: (b * nt + j, 0))
    sspec = pl.BlockSpec((1, H_A, DK_A, DV_A), lambda b, j: (b, 0, 0, 0))
    return pl.pallas_call(
        _hgrn_kernel,
        grid=(bsz, nt),
        in_specs=[row, row, row, row, pl.BlockSpec((1, DV_A), lambda b, j: (0, 0)), sspec],
        out_specs=[orow, sspec],
        out_shape=[jax.ShapeDtypeStruct((bsz * seq, d), BF16),
                   jax.ShapeDtypeStruct((bsz, H_A, DK_A, DV_A), F32)],
        scratch_shapes=[pltpu.VMEM((H_A, DK_A, DV_A), F32)]
        + [pltpu.VMEM((tl, d), BF16)] * 5
        + [pltpu.VMEM((tl // HGRN_BLOCK, d), F32), pltpu.VMEM((tl, d), F32)],
        compiler_params=_cparams(("arbitrary", "arbitrary")),
        name="hgrn",
    )(qa, f, ia, sg, gn, s0)


SM_KB, SM_VB, SM_QI, SM_KI, SM_WI = 0, DH_B, 2 * DH_B, 2 * DH_B + H_I * D_I, 2 * DH_B + H_I * D_I + D_I
SM_USED = SM_WI + H_I
SM_WIDTH = -(-SM_USED // LANES) * LANES
INDEX_SCALE = (H_I * D_I) ** -0.5
NEG_INF = float("-inf")
LOG2_E = 1.4426950408889634


F32_EXP_MASK = 0x7F800000
F32_MIN_NORMAL = 0x00800000
SEARCH_UNROLL = 4


def _key_to_float(u):
    key = u ^ jnp.int32(INT_MIN)
    bits = jnp.where(key < 0, key ^ jnp.int32(0x7FFFFFFF), key)
    below_neg_inf = (bits < 0) & ((bits & jnp.int32(0x7FFFFFFF)) > jnp.int32(F32_EXP_MASK))
    return jnp.where(below_neg_inf, NEG_INF, lax.bitcast_convert_type(bits, F32))


def _positive_bits_to_float(bits):
    return jnp.where(bits < jnp.int32(F32_MIN_NORMAL), 0.0, lax.bitcast_convert_type(bits, F32))


def _count(mask):
    return jnp.sum(jnp.where(mask, 1.0, 0.0), axis=1, keepdims=True)


def _select_bias(score_scr, bias_scr, adm, s, n_sel):
    tq = score_scr.shape[0]

    def resolve_bit(i, t_u, cnt_t):
        cand_u = t_u | lax.shift_left(jnp.int32(1), 31 - i)
        cnt = _count(score_scr[:, :s] >= _key_to_float(cand_u))
        ok = cnt >= n_sel
        return jnp.where(ok, cand_u, t_u), jnp.where(ok, cnt, cnt_t)

    def unresolved(c):
        i, _, cnt_t = c
        return jnp.logical_and(i < 32, jnp.any(cnt_t != n_sel))

    def resolve_bits(c):
        i, t_u, cnt_t = c
        for b in range(SEARCH_UNROLL):
            t_u, cnt_t = resolve_bit(i + b, t_u, cnt_t)
        return i + SEARCH_UNROLL, t_u, cnt_t

    _, t_u, cnt_t = lax.while_loop(unresolved, resolve_bits,
                                   (jnp.int32(0), jnp.zeros((tq, 1), I32), jnp.full((tq, 1), float(s), F32)))
    t_f = _key_to_float(t_u)
    score = score_scr[:, :s]
    bias_scr[:, :s] = jnp.where((score >= t_f) & adm, 0.0, NEG_INF)

    @pl.when(jnp.any(cnt_t > n_sel))
    def _():
        above = score >= _key_to_float(t_u + 1)
        bucket = (score >= t_f) & jnp.logical_not(above)
        need = n_sel - _count(above)
        off = jnp.where(bucket, score - jnp.where(t_f == NEG_INF, 0.0, t_f), -1.0)

        def resolve_offset_bit(i, r_bits):
            cand = r_bits | lax.shift_left(jnp.int32(1), 30 - i)
            ok = _count(off >= _positive_bits_to_float(cand)) >= need
            return jnp.where(ok, cand, r_bits)

        r_bits = lax.fori_loop(0, 31, resolve_offset_bit, jnp.zeros((tq, 1), I32))
        above2 = off >= _positive_bits_to_float(r_bits + 1)
        tie = (off >= _positive_bits_to_float(r_bits)) & jnp.logical_not(above2)
        need2 = need - _count(above2)
        upper = (lax.broadcasted_iota(I32, (LANES, LANES), 0)
                 <= lax.broadcasted_iota(I32, (LANES, LANES), 1)).astype(BF16)
        carry = jnp.zeros((tq, 1), F32)
        for c in range(s // LANES):
            cs = slice(c * LANES, (c + 1) * LANES)
            rank = jnp.dot(jnp.where(tie[:, cs], 1.0, 0.0).astype(BF16), upper, preferred_element_type=F32) + carry
            sel = (above[:, cs] | above2[:, cs] | (tie[:, cs] & (rank <= need2))) & adm[:, cs]
            bias_scr[:, cs] = jnp.where(sel, 0.0, NEG_INF)
            carry = rank[:, LANES - 1:LANES]


def _dsa_tile(qb_ref, sm_ref, k_ref, v_ref, ki_ref, lim_ref, o_ref, score_scr, bias_scr, *, s, search, n_sel):
    tq = qb_ref.shape[0]
    nt = (((1,), (1,)), ((), ()))
    adm = lax.broadcasted_iota(I32, (tq, s), 1) < lim_ref[...]

    if search:
        kidx = ki_ref[0, :s, :]
        score = jnp.zeros((tq, s), F32)
        for h in range(H_I):
            qih = sm_ref[:, SM_QI + h * D_I:SM_QI + (h + 1) * D_I].astype(BF16)
            sh = lax.dot_general(qih, kidx, nt, preferred_element_type=F32)
            wih = sm_ref[:, SM_WI + h:SM_WI + h + 1] * INDEX_SCALE
            score = score + jnp.maximum(sh, 0.0) * wih
        score_scr[:, :s] = jnp.where(adm, score, NEG_INF)
        _select_bias(score_scr, bias_scr, adm, s, n_sel)
    else:
        bias_scr[:, :s] = jnp.where(adm, 0.0, NEG_INF)

    kk = k_ref[0, :s, :]
    vv = v_ref[0, :s, :]
    for h in range(H_B):
        hs = slice(h * DH_B, (h + 1) * DH_B)
        logits = lax.dot_general(qb_ref[:, hs], kk, nt, preferred_element_type=F32) + bias_scr[:, :s]
        p = jnp.exp2((logits - jnp.max(logits, axis=1, keepdims=True)) * (DH_B ** -0.5 * LOG2_E))
        denom = jnp.sum(p, axis=1, keepdims=True)
        o = jnp.dot(p.astype(BF16), vv, preferred_element_type=F32)
        o_ref[:, hs] = (o / denom).astype(BF16)


DSA_KEY_GRAN = 256


def _dsa(qb, sm, k16, v16, ki16, limits, *, bsz, seq, row0, tq, n_sel):
    d = qb.shape[1]
    n_keys = k16.shape[1]
    nt = seq // tq
    blk0 = row0 // tq
    limits = np.asarray(limits, np.int32)
    tile_max = limits.reshape(nt, tq).max(axis=1)
    per_tile = [(int(min(n_keys, -(-m // DSA_KEY_GRAN) * DSA_KEY_GRAN)), bool(m > n_sel)) for m in tile_max]
    lim2d = jnp.asarray(limits.reshape(seq, 1))
    outs, lo = [], 0
    while lo < nt:
        hi = lo
        while hi + 1 < nt and per_tile[hi + 1] == per_tile[lo]:
            hi += 1
        s, search = per_tile[lo]
        ntv = hi - lo + 1
        row = lambda w, lo=lo: pl.BlockSpec((tq, w), lambda b, j: (blk0 + b * nt + lo + j, 0))
        keys = lambda w, s=s: pl.BlockSpec((1, s, w), lambda b, j: (b, 0, 0))
        out = pl.pallas_call(
            functools.partial(_dsa_tile, s=s, search=search, n_sel=n_sel),
            grid=(bsz, ntv),
            in_specs=[row(d), row(sm.shape[1]), keys(DH_B), keys(DH_B), keys(D_I),
                      pl.BlockSpec((tq, 1), lambda b, j, lo=lo: (lo + j, 0))],
            out_specs=pl.BlockSpec((tq, d), lambda b, j, ntv=ntv: (b * ntv + j, 0)),
            out_shape=jax.ShapeDtypeStruct((bsz * ntv * tq, d), BF16),
            scratch_shapes=[pltpu.VMEM((tq, s), F32), pltpu.VMEM((tq, s), F32)],
            compiler_params=_cparams(("arbitrary", "arbitrary")),
            name="dsa",
        )(qb, sm, k16, v16, ki16, lim2d)
        outs.append(out.reshape(bsz, ntv * tq, d))
        lo = hi + 1
    return jnp.concatenate(outs, axis=1).reshape(bsz * seq, d)


def _split_bf16(a):
    hi = a.astype(BF16)
    return hi, (a - hi.astype(F32)).astype(BF16)


def _mix_kernel(oa_ref, ob_ref, x_ref, woa_ref, wob_ref, wga_ref, wgb_ref, wout_ref, g1_ref, b1_ref,
                wrh_ref, wrl_ref, br_ref, x1_ref, eidx_ref, gate_ref, cnt_ref, *, alpha):
    i = pl.program_id(0)
    x = x_ref[...]
    xb = x.astype(BF16)
    dot = lambda a, w_ref: jnp.dot(a, w_ref[...], preferred_element_type=F32)
    y_a = dot(oa_ref[...], woa_ref)
    y_b = dot(ob_ref[...], wob_ref)
    merged = _sigmoid(dot(xb, wga_ref)) * y_a + _sigmoid(dot(xb, wgb_ref)) * y_b
    mixed = dot(merged.astype(BF16), wout_ref)
    x1 = _layer_norm(alpha * x + mixed, g1_ref[...], b1_ref[...])
    x1_ref[...] = x1

    nt = (((1,), (1,)), ((), ()))
    xh, xl = _split_bf16(x1)
    wh, wl = wrh_ref[...], wrl_ref[...]
    logits = (lax.dot_general(wh, xh, nt, preferred_element_type=F32)
              + lax.dot_general(wh, xl, nt, preferred_element_type=F32)
              + lax.dot_general(wl, xh, nt, preferred_element_type=F32)) + br_ref[...]
    ne, tm = logits.shape
    erow = lax.broadcasted_iota(I32, (ne, tm), 0)
    vals, idxs = [], []
    for _ in range(TOP_K):
        m = jnp.max(logits, axis=0, keepdims=True)
        idx = jnp.min(jnp.where(logits == m, erow, ne), axis=0, keepdims=True)
        vals.append(m)
        idxs.append(idx)
        logits = jnp.where(erow == idx, NEG_INF, logits)
    ex = [jnp.exp(v - vals[0]) for v in vals]
    denom = ex[0] + ex[1] + ex[2] + ex[3]
    pad = SUBLANES - TOP_K
    eidx = jnp.concatenate(idxs + [jnp.zeros((pad, tm), I32)], axis=0)
    eidx_ref[...] = eidx
    gate_ref[...] = jnp.concatenate([e / denom for e in ex] + [jnp.zeros((pad, tm), F32)], axis=0)

    onehot = jnp.zeros((ne, tm), F32)
    for idx in idxs:
        onehot = onehot + jnp.where(erow == idx, 1.0, 0.0)
    tile_cnt = jnp.broadcast_to(jnp.sum(onehot, axis=1, keepdims=True), cnt_ref.shape)

    @pl.when(i == 0)
    def _():
        cnt_ref[...] = tile_cnt

    @pl.when(i > 0)
    def _():
        cnt_ref[...] = cnt_ref[...] + tile_cnt


def _mix(oa, ob, x, woa, wob, wga, wgb, wout, g1, b1, wrh, wrl, br, *, tm, alpha):
    n, d = x.shape
    ne = wrh.shape[0]
    row = lambda w: pl.BlockSpec((tm, w), lambda i: (i, 0))
    col = pl.BlockSpec((SUBLANES, tm), lambda i: (0, i))
    full = lambda a: pl.BlockSpec(a.shape, lambda i: (0,) * a.ndim)
    return pl.pallas_call(
        functools.partial(_mix_kernel, alpha=alpha),
        grid=(n // tm,),
        in_specs=[row(d), row(d), row(d)] + [full(a) for a in (woa, wob, wga, wgb, wout, g1, b1, wrh, wrl, br)],
        out_specs=[row(d), col, col, pl.BlockSpec((ne, LANES), lambda i: (0, 0))],
        out_shape=[jax.ShapeDtypeStruct((n, d), F32), jax.ShapeDtypeStruct((SUBLANES, n), I32),
                   jax.ShapeDtypeStruct((SUBLANES, n), F32), jax.ShapeDtypeStruct((ne, LANES), F32)],
        compiler_params=_cparams(("arbitrary",)),
        name="mix",
    )(oa, ob, x, woa, wob, wga, wgb, wout, g1, b1, wrh, wrl, br)


def _sublane_cumsum(a):
    n = a.shape[0]
    row = lax.broadcasted_iota(I32, a.shape, 0)
    s = 1
    while s < n:
        a = a + jnp.where(row >= s, pltpu.roll(a, s, 0), 0.0)
        s *= 2
    return a


def _route_kernel(eidx_ref, cnt_ref, dest_ref, blke_ref, meta_ref, carry_scr, *, bm, sub):
    i = pl.program_id(0)
    ne = cnt_ref.shape[0]
    tl = eidx_ref.shape[1]

    @pl.when(i == 0)
    def _():
        carry_scr[...] = jnp.zeros_like(carry_scr)

    counts = cnt_ref[...]
    padded = jnp.ceil(counts / bm) * bm
    pend = _sublane_cumsum(padded)
    pstart = (pend - padded)[:, 0:1]

    @pl.when(i == 0)
    def _():
        nb = blke_ref.shape[1]
        first_row = (lax.broadcasted_iota(I32, (ne, nb), 1) * bm).astype(F32)
        below = jnp.sum(jnp.where(pend[:, 0:1] <= first_row, 1.0, 0.0), axis=0, keepdims=True)
        blke_ref[...] = jnp.minimum(below, ne - 1.0).astype(I32)
        pend_lanes = jnp.concatenate([pend, jnp.zeros((LANES - ne, LANES), F32)], axis=0).T[0:1, :]
        n_used = jnp.max(pend, axis=0, keepdims=True) / bm
        meta_ref[...] = jnp.concatenate([pend_lanes, n_used, jnp.zeros((SUBLANES - 2, LANES), F32)],
                                        axis=0).astype(I32)

    eidx = eidx_ref[...]
    erow = lax.broadcasted_iota(I32, (ne, tl), 0)
    hot = [jnp.where(erow == eidx[k:k + 1, :], 1.0, 0.0) for k in range(TOP_K)]
    onehot = hot[0] + hot[1] + hot[2] + hot[3]
    before = (lax.broadcasted_iota(I32, (tl, tl), 0) < lax.broadcasted_iota(I32, (tl, tl), 1)).astype(BF16)
    base = jnp.dot(onehot.astype(BF16), before, preferred_element_type=F32) + carry_scr[:, 0:1] + pstart
    dest = jnp.concatenate([jnp.sum(hk * base, axis=0, keepdims=True) for hk in hot], axis=0).astype(I32)
    for c in range(tl // sub):
        dest_ref[c] = dest[:, c * sub:(c + 1) * sub]
    carry_scr[...] = carry_scr[...] + jnp.sum(onehot, axis=1, keepdims=True)


def _route(eidx, counts, *, tl, sub, bm, nb):
    n = eidx.shape[1]
    ne = counts.shape[0]
    nb_pad = -(-nb // LANES) * LANES
    return pl.pallas_call(
        functools.partial(_route_kernel, bm=bm, sub=sub),
        grid=(n // tl,),
        in_specs=[pl.BlockSpec((SUBLANES, tl), lambda i: (0, i)), pl.BlockSpec(counts.shape, lambda i: (0, 0))],
        out_specs=[pl.BlockSpec((tl // sub, TOP_K, sub), lambda i: (i, 0, 0)),
                   pl.BlockSpec((1, nb_pad), lambda i: (0, 0)),
                   pl.BlockSpec((SUBLANES, LANES), lambda i: (0, 0))],
        out_shape=[jax.ShapeDtypeStruct((n // sub, TOP_K, sub), I32), jax.ShapeDtypeStruct((1, nb_pad), I32),
                   jax.ShapeDtypeStruct((SUBLANES, LANES), I32)],
        scratch_shapes=[pltpu.VMEM((ne, LANES), F32)],
        compiler_params=_cparams(("arbitrary",)),
        name="route",
    )(eidx, counts)


SRC_TOKEN_BITS = 15
SRC_SLOT_BITS = 3
SRC_TOKEN_MASK = (1 << SRC_TOKEN_BITS) - 1
SRC_SLOT_MASK = (1 << SRC_SLOT_BITS) - 1


def _invperm_kernel(dest_ref, meta_ref, src_ref, *, bm, ne):
    i = pl.program_id(0)
    tm = dest_ref.shape[2]

    @pl.when(i == 0)
    def _():
        def clear(r, carry):
            src_ref[r] = 0
            return carry
        lax.fori_loop(meta_ref[0, ne - 1], src_ref.shape[0], clear, 0)

        for e in range(ne):
            seg_end = meta_ref[0, e]
            seg_start = meta_ref[0, e - 1] if e > 0 else 0

            @pl.when(seg_end > seg_start)
            def _(e=e, seg_end=seg_end):
                def fill(j, carry):
                    src_ref[seg_end - bm + j] = ((TOP_K << SRC_TOKEN_BITS)
                                                 | ((e * bm + j) << (SRC_TOKEN_BITS + SRC_SLOT_BITS)))
                    return carry
                lax.fori_loop(0, bm, fill, 0)

    def invert(t, carry):
        for k in range(TOP_K):
            src_ref[dest_ref[0, k, t]] = (k << SRC_TOKEN_BITS) | (i * tm + t)
        return carry
    lax.fori_loop(0, tm, invert, 0)


def _invperm(dest, meta, n_rows, *, bm, ne):
    nt, _, tm = dest.shape
    assert nt * tm <= SRC_TOKEN_MASK + 1 and ne * bm <= (1 << (31 - SRC_TOKEN_BITS - SRC_SLOT_BITS))
    return pl.pallas_call(
        functools.partial(_invperm_kernel, bm=bm, ne=ne),
        grid=(nt,),
        in_specs=[pl.BlockSpec((1, TOP_K, tm), lambda i: (i, 0, 0), memory_space=pltpu.SMEM),
                  pl.BlockSpec(meta.shape, lambda i: (0, 0), memory_space=pltpu.SMEM)],
        out_specs=pl.BlockSpec((n_rows,), lambda i: (0,), memory_space=pltpu.SMEM),
        out_shape=jax.ShapeDtypeStruct((n_rows,), I32),
        compiler_params=_cparams(("arbitrary",)),
        name="invperm",
    )(dest, meta)


def _experts_kernel(blke_ref, nused_ref, src_ref, src_next_ref, x_ref, wu_ref, bu_ref, wd_ref, bd_ref, out_ref,
                    xbuf, ybuf, wu16, wd16, gsem, ssem, *, n_tokens):
    i = pl.program_id(0)
    bm = xbuf.shape[1]
    dff = wd_ref.shape[1]
    n_used = nused_ref[0]
    slot = i % 2

    def gather(codes_ref, to_slot):
        for r in range(bm):
            t = codes_ref[0, 0, r] & SRC_TOKEN_MASK
            pltpu.make_async_copy(x_ref.at[pl.ds(t, 1), :], xbuf.at[to_slot, pl.ds(r, 1), :], gsem.at[to_slot]).start()

    def wait_rows(sem):
        for _ in range(bm):
            pltpu.make_async_copy(x_ref.at[pl.ds(0, 1), :], xbuf.at[0, pl.ds(0, 1), :], sem).wait()

    @pl.when(i < n_used)
    def _():
        @pl.when(i == 0)
        def _():
            gather(src_ref, 0)
            ybuf[1] = jnp.zeros(ybuf.shape[1:], ybuf.dtype)
            n_dump = out_ref.shape[0] - TOP_K * n_tokens
            dump_copy = lambda b: pltpu.make_async_copy(
                ybuf.at[1], out_ref.at[pl.ds(TOP_K * n_tokens + b * bm, bm), :], ssem.at[1])
            for b in range(n_dump // bm):
                dump_copy(b).start()
            for b in range(n_dump // bm):
                dump_copy(b).wait()

        wait_rows(gsem.at[slot])

        @pl.when(i >= 2)
        def _():
            wait_rows(ssem.at[slot])

        @pl.when(jnp.logical_or(i == 0, blke_ref[i] != blke_ref[jnp.maximum(i - 1, 0)]))
        def _():
            wu16[...] = wu_ref[0].astype(BF16)
            wd16[...] = wd_ref[0].astype(BF16)

        @pl.when(i + 1 < n_used)
        def _():
            gather(src_next_ref, 1 - slot)

        h = jnp.dot(xbuf[slot].astype(BF16), wu16[...], preferred_element_type=F32) + bu_ref[0]
        glu = jnp.minimum(h[:, :dff], SWIGLU_LIMIT)
        lin = jnp.clip(h[:, dff:], -SWIGLU_LIMIT, SWIGLU_LIMIT)
        act = glu * _sigmoid(SWIGLU_ALPHA * glu) * (lin + 1.0)
        ybuf[slot] = jnp.dot(act.astype(BF16), wd16[...], preferred_element_type=F32) + bd_ref[0]

        for r in range(bm):
            code = src_ref[0, 0, r]
            row = (((code >> SRC_TOKEN_BITS) & SRC_SLOT_MASK) * n_tokens + (code & SRC_TOKEN_MASK)
                   + (code >> (SRC_TOKEN_BITS + SRC_SLOT_BITS)))
            pltpu.make_async_copy(ybuf.at[slot, pl.ds(r, 1), :], out_ref.at[pl.ds(row, 1), :], ssem.at[slot]).start()

        @pl.when(i == n_used - 1)
        def _():
            wait_rows(ssem.at[slot])

            @pl.when(i >= 1)
            def _():
                wait_rows(ssem.at[1 - slot])


def _experts(blk_e, n_used, src, x1, wu, bu, wd, bd, *, bm, ne_pad_rows):
    n, d = x1.shape
    ne, _, dff2 = wu.shape
    dff = wd.shape[1]
    nb = src.shape[0] // bm
    src3 = src.reshape(nb, 1, bm)
    last = lambda i, nu: jnp.minimum(i, nu[0] - 1)
    grid_spec = pltpu.PrefetchScalarGridSpec(
        num_scalar_prefetch=2,
        grid=(nb,),
        in_specs=[pl.BlockSpec((1, 1, bm), lambda i, e, nu: (last(i, nu), 0, 0), memory_space=pltpu.SMEM),
                  pl.BlockSpec((1, 1, bm), lambda i, e, nu: (last(i + 1, nu), 0, 0), memory_space=pltpu.SMEM),
                  pl.BlockSpec(memory_space=pl.ANY),
                  pl.BlockSpec((1, d, dff2), lambda i, e, nu: (e[i], 0, 0)),
                  pl.BlockSpec((1, 1, dff2), lambda i, e, nu: (e[i], 0, 0)),
                  pl.BlockSpec((1, dff, d), lambda i, e, nu: (e[i], 0, 0)),
                  pl.BlockSpec((1, 1, d), lambda i, e, nu: (e[i], 0, 0))],
        out_specs=pl.BlockSpec(memory_space=pl.ANY),
        scratch_shapes=[pltpu.VMEM((2, bm, d), F32), pltpu.VMEM((2, bm, d), F32),
                        pltpu.VMEM((d, dff2), BF16), pltpu.VMEM((dff, d), BF16),
                        pltpu.SemaphoreType.DMA((2,)), pltpu.SemaphoreType.DMA((2,))],
    )
    return pl.pallas_call(
        functools.partial(_experts_kernel, n_tokens=n),
        grid_spec=grid_spec,
        out_shape=jax.ShapeDtypeStruct((TOP_K * n + ne_pad_rows, d), F32),
        compiler_params=_cparams(("arbitrary",)),
        name="experts",
    )(blk_e, n_used, src3, src3, x1, wu, bu.reshape(ne, 1, dff2), wd, bd.reshape(ne, 1, d))


def _combine_kernel(y0_ref, y1_ref, y2_ref, y3_ref, gate_ref, x1_ref, g2_ref, b2_ref, out_ref, *, alpha):
    gate = gate_ref[...].T
    y = y0_ref[...] * gate[:, 0:1]
    for k, yk_ref in enumerate((y1_ref, y2_ref, y3_ref), start=1):
        y = y + yk_ref[...] * gate[:, k:k + 1]
    out_ref[...] = _layer_norm(alpha * x1_ref[...] + y, g2_ref[...], b2_ref[...])


def _combine(ys, gate, x1, g2, b2, *, tm, alpha):
    n, d = x1.shape
    nt = n // tm
    expert_rows = lambda k: pl.BlockSpec((tm, d), lambda i, k=k: (k * nt + i, 0))
    return pl.pallas_call(
        functools.partial(_combine_kernel, alpha=alpha),
        grid=(nt,),
        in_specs=[expert_rows(k) for k in range(TOP_K)]
        + [pl.BlockSpec((SUBLANES, tm), lambda i: (0, i)),
           pl.BlockSpec((tm, d), lambda i: (i, 0)),
           pl.BlockSpec((1, d), lambda i: (0, 0)),
           pl.BlockSpec((1, d), lambda i: (0, 0))],
        out_specs=pl.BlockSpec((tm, d), lambda i: (i, 0)),
        out_shape=jax.ShapeDtypeStruct((n, d), F32),
        compiler_params=_cparams(("arbitrary",)),
        name="combine",
    )(ys, ys, ys, ys, gate, x1, g2, b2)


PROJ_TM = 256
HGRN_TL = 256
DSA_TQ = 128
MIX_TM = 256
ROUTE_TL = 1280
MOE_TM = 256
MOE_BM = 256


def kernel(x_prompt, x_sample, cache_k, cache_v, cache_kidx, state_hgrn, w_in, lb_logits, gn_a, w_oa, w_ob,
           w_out, ln1_g, ln1_b, w_router, b_router, w_up, b_up, w_down, b_down, ln2_g, ln2_b):
    depth = w_in.shape[0]
    assert depth == 1
    bsz, seq, d = x_prompt.shape
    dbsz, dseq, _ = x_sample.shape
    past = cache_k.shape[2]
    n_p, n_s = bsz * seq, dbsz * dseq
    n = n_p + n_s
    alpha = (2 * depth) ** 0.25

    sizes = (H_A * DK_A, H_A * DK_A, H_A * DV_A, H_A * DV_A, H_B * DH_B, DH_B, DH_B, H_I * D_I, H_I, D_I, d, d)
    offs = np.concatenate([[0], np.cumsum(sizes)])
    w = w_in[0].astype(BF16)
    grp = lambda g: w[:, offs[g]:offs[g + 1]]
    wqa, wfa, wia, wg, wqb, wkb, wvb, wqi, wwi, wki, wgta, wgtb = (grp(g) for g in range(12))
    wsm = jnp.concatenate([wkb, wvb, wqi, wki, wwi, jnp.zeros((d, SM_WIDTH - SM_USED), BF16)], axis=1)

    x_all = jnp.concatenate([x_prompt.reshape(n_p, d), x_sample.reshape(n_s, d)], axis=0)
    qa, f, ia, sg, qb, sm = _proj(x_all, wqa, wfa, wia, wg, wqb, wsm, lb_logits, PROJ_TM)

    gn = gn_a[0].reshape(1, DV_A)
    oa_p, sfin_p = _hgrn(qa, f, ia, sg, gn, jnp.zeros((bsz, H_A, DK_A, DV_A), F32),
                         bsz=bsz, seq=seq, row0=0, tl=HGRN_TL)
    oa_s, sfin_s = _hgrn(qa, f, ia, sg, gn, state_hgrn[0], bsz=dbsz, seq=dseq, row0=n_p, tl=dseq)

    kb_p = sm[:n_p, SM_KB:SM_KB + DH_B].reshape(bsz, seq, DH_B)
    vb_p = sm[:n_p, SM_VB:SM_VB + DH_B].reshape(bsz, seq, DH_B)
    ki_p = sm[:n_p, SM_KI:SM_KI + D_I].reshape(bsz, seq, D_I)
    kb_s = sm[n_p:, SM_KB:SM_KB + DH_B].reshape(dbsz, dseq, DH_B)
    vb_s = sm[n_p:, SM_VB:SM_VB + DH_B].reshape(dbsz, dseq, DH_B)
    ki_s = sm[n_p:, SM_KI:SM_KI + D_I].reshape(dbsz, dseq, D_I)
    lim_p = (np.arange(seq) // CHUNK + 1) * CHUNK
    ob_p = _dsa(qb, sm, kb_p.astype(BF16), vb_p.astype(BF16), ki_p.astype(BF16), lim_p,
                bsz=bsz, seq=seq, row0=0, tq=DSA_TQ, n_sel=min(TOPK_MAX, seq // 4))
    n_keys = past + dseq
    key_pad = -(-n_keys // LANES) * LANES - n_keys
    keys_s = lambda c, new: jnp.pad(jnp.concatenate([c[0].astype(BF16), new.astype(BF16)], axis=1),
                                    ((0, 0), (0, key_pad), (0, 0)))
    lim_s = np.full((dseq,), n_keys)
    ob_s = _dsa(qb, sm, keys_s(cache_k, kb_s), keys_s(cache_v, vb_s), keys_s(cache_kidx, ki_s), lim_s,
                bsz=dbsz, seq=dseq, row0=n_p, tq=dseq, n_sel=min(TOPK_MAX, n_keys // 4))

    oa = jnp.concatenate([oa_p, oa_s], axis=0)
    ob = jnp.concatenate([ob_p, ob_s], axis=0)
    wrh, wrl = _split_bf16(w_router[0].T)
    x1, eidx, gate, counts = _mix(
        oa, ob, x_all, w_oa[0].astype(BF16), w_ob[0].astype(BF16), wgta, wgtb, w_out[0].astype(BF16),
        ln1_g[0].reshape(1, d), ln1_b[0].reshape(1, d), wrh, wrl, b_router[0].reshape(N_EXPERTS, 1),
        tm=MIX_TM, alpha=alpha)

    n_rows = -(-(n * TOP_K + N_EXPERTS * (MOE_BM - 1)) // MOE_BM) * MOE_BM
    dest, blk_e, meta = _route(eidx, counts, tl=ROUTE_TL, sub=MOE_TM, bm=MOE_BM, nb=n_rows // MOE_BM)
    src = _invperm(dest, meta, n_rows, bm=MOE_BM, ne=N_EXPERTS)
    ys = _experts(blk_e[0, :n_rows // MOE_BM], meta[1, :1], src, x1, w_up[0], b_up[0], w_down[0], b_down[0],
                  bm=MOE_BM, ne_pad_rows=N_EXPERTS * MOE_BM)
    out = _combine(ys, gate, x1, ln2_g[0].reshape(1, d), ln2_b[0].reshape(1, d), tm=MOE_TM, alpha=alpha)

    return (out[:n_p].reshape(bsz, seq, d), out[n_p:].reshape(dbsz, dseq, d),
            kb_p[None], vb_p[None], ki_p[None], sfin_p[None],
            kb_s[None], vb_s[None], ki_s[None], sfin_s[None])
```

```python
import functools

import jax
import jax.numpy as jnp
import numpy as np
from jax import lax
from jax.experimental import pallas as pl
from jax.experimental.pallas import tpu as pltpu

F32 = jnp.float32
BF16 = jnp.bfloat16
I32 = jnp.int32

CHUNK = 64
H_A = 8
DK_A = 128
DV_A = 128
HGRN_BLOCK = 16
H_B = 8
DH_B = 128
H_I = 8
D_I = 64
TOPK_MAX = 256
N_EXPERTS = 32
TOP_K = 4
SWIGLU_ALPHA = 1.702
SWIGLU_LIMIT = 7.0
EPS = 1e-5

LANES = 128
SUBLANES = 8
VMEM_LIMIT_BYTES = 56 * 1024 * 1024

INT_MIN = -(2 ** 31)


def _cparams(sem):
    return pltpu.CompilerParams(dimension_semantics=sem, vmem_limit_bytes=VMEM_LIMIT_BYTES)


def _sigmoid(x):
    return 1.0 / (1.0 + jnp.exp(-x))


def _full_spec(shape):
    nd = len(shape)
    return pl.BlockSpec(shape, lambda *_: (0,) * nd)


def _layer_norm(x, g, b):
    mu = jnp.mean(x, axis=-1, keepdims=True)
    xc = x - mu
    var = jnp.mean(xc * xc, axis=-1, keepdims=True)
    return xc * lax.rsqrt(var + EPS) * g + b


def _proj_kernel(x_ref, wqa_ref, wfa_ref, wia_ref, wga_ref, wqb_ref, wsm_ref, lbl_ref,
                 qa_ref, f_ref, ia_ref, sg_ref, qb_ref, sm_ref):
    xb = x_ref[...].astype(BF16)
    dot = lambda w_ref: jnp.dot(xb, w_ref[...], preferred_element_type=F32)
    qa_ref[...] = dot(wqa_ref)
    lbl = lbl_ref[...]
    lbe = jnp.exp(lbl - jnp.max(lbl, axis=0, keepdims=True))
    lb = lbe[0:1, :] / jnp.sum(lbe, axis=0, keepdims=True)
    f_ref[...] = lb + (1.0 - lb) * _sigmoid(dot(wfa_ref))
    ia_ref[...] = dot(wia_ref)
    g = dot(wga_ref)
    sg_ref[...] = g * _sigmoid(g)
    qb_ref[...] = dot(wqb_ref).astype(BF16)
    sm_ref[...] = dot(wsm_ref)


def _proj(x, wqa, wfa, wia, wga, wqb, wsm, lb_logits, tm):
    n, d = x.shape
    nsm = wsm.shape[1]
    row = lambda w: pl.BlockSpec((tm, w), lambda i: (i, 0))
    wspec = lambda w: pl.BlockSpec(w.shape, lambda i: (0, 0))
    return pl.pallas_call(
        _proj_kernel,
        grid=(n // tm,),
        in_specs=[row(d), wspec(wqa), wspec(wfa), wspec(wia), wspec(wga), wspec(wqb), wspec(wsm),
                  wspec(lb_logits)],
        out_specs=[row(d), row(d), row(d), row(d), row(d), row(nsm)],
        out_shape=[jax.ShapeDtypeStruct((n, d), F32)] * 4
        + [jax.ShapeDtypeStruct((n, d), BF16), jax.ShapeDtypeStruct((n, nsm), F32)],
        compiler_params=_cparams(("arbitrary",)),
        name="proj",
    )(x, wqa, wfa, wia, wga, wqb, wsm, lb_logits)


def _block_cumsum(a, row_in_block):
    s = 1
    while s < HGRN_BLOCK:
        a = a + jnp.where(row_in_block >= s, pltpu.roll(a, s, 0), 0.0)
        s *= 2
    return a


def _hgrn_kernel(qa_ref, f_ref, ia_ref, sg_ref, gn_ref, s0_ref, o_ref, sfin_ref,
                 s_scr, qd_scr, qm_scr, km_scr, kl_scr, v_scr, dl_scr, o_scr):
    j = pl.program_id(1)
    tl = qa_ref.shape[0]
    nblk = tl // HGRN_BLOCK
    half = HGRN_BLOCK // 2

    @pl.when(j == 0)
    def _():
        for h in range(H_A):
            s_scr[h] = s0_ref[0, h].T

    f = f_ref[...]
    q = qa_ref[...]
    d = f.shape[1]
    row_in_block = lax.broadcasted_iota(I32, (tl, d), 0) % HGRN_BLOCK
    cum = _block_cumsum(jnp.log(f), row_in_block)
    cum3 = cum.reshape(nblk, HGRN_BLOCK, d)
    mid = jnp.broadcast_to(cum3[:, half:half + 1, :], cum3.shape).reshape(tl, d)
    last3 = cum3[:, HGRN_BLOCK - 1:HGRN_BLOCK, :]
    last = jnp.broadcast_to(last3, cum3.shape).reshape(tl, d)
    k = 1.0 - f
    qd_scr[...] = (q * jnp.exp(cum)).astype(BF16)
    qm_scr[...] = (q * jnp.exp(cum - mid)).astype(BF16)
    km_scr[...] = (k * jnp.exp(mid - cum)).astype(BF16)
    kl_scr[...] = (k * jnp.exp(last - cum)).astype(BF16)
    dl_scr[...] = jnp.exp(last3.reshape(nblk, d))
    v_scr[...] = ia_ref[...].astype(BF16)

    nt = (((1,), (1,)), ((), ()))
    ri = lax.broadcasted_iota(I32, (tl, tl), 0)
    ci = lax.broadcasted_iota(I32, (tl, tl), 1)
    keep = (ri // HGRN_BLOCK == ci // HGRN_BLOCK) & (ri >= ci)
    for h in range(H_A):
        cols = slice(h * DK_A, (h + 1) * DK_A)
        vcols = slice(h * DV_A, (h + 1) * DV_A)
        att = lax.dot_general(qm_scr[:, cols], km_scr[:, cols], nt, preferred_element_type=F32)
        att = jnp.where(keep, att, 0.0).astype(BF16)
        o_scr[:, vcols] = jnp.dot(att, v_scr[:, vcols], preferred_element_type=F32)

    def block(b, carry):
        r0 = pl.multiple_of(b * HGRN_BLOCK, HGRN_BLOCK)
        rows = pl.ds(r0, HGRN_BLOCK)
        dl = dl_scr[pl.ds(b, 1), :]
        for h in range(H_A):
            cols = slice(h * DK_A, (h + 1) * DK_A)
            vcols = slice(h * DV_A, (h + 1) * DV_A)
            st_h = s_scr[h]
            o_scr[rows, vcols] += lax.dot_general(qd_scr[rows, cols], st_h.astype(BF16), nt,
                                                  preferred_element_type=F32)
            upd_t = lax.dot_general(v_scr[rows, vcols], kl_scr[rows, cols], (((0,), (0,)), ((), ())),
                                    preferred_element_type=F32)
            s_scr[h] = dl[:, cols] * st_h + upd_t
        return carry

    lax.fori_loop(0, nblk, block, 0, unroll=2 if nblk % 2 == 0 else 1)

    gn = gn_ref[...]
    for h in range(H_A):
        vcols = slice(h * DV_A, (h + 1) * DV_A)
        o = o_scr[:, vcols]
        o = o * lax.rsqrt(jnp.mean(o * o, axis=-1, keepdims=True) + EPS) * gn
        o_ref[:, vcols] = (o * sg_ref[:, vcols]).astype(BF16)

    @pl.when(j == pl.num_programs(1) - 1)
    def _():
        for h in range(H_A):
            sfin_ref[0, h] = s_scr[h].T


def _hgrn(qa, f, ia, sg, gn, s0, *, bsz, seq, row0, tl):
    d = qa.shape[1]
    nt = seq // tl
    blk0 = row0 // tl
    row = pl.BlockSpec((tl, d), lambda b, j: (blk0 + b * nt + j, 0))
    orow = pl.BlockSpec((tl, d), lambda b, j: (b * nt + j, 0))
    sspec = pl.BlockSpec((1, H_A, DK_A, DV_A), lambda b, j: (b, 0, 0, 0))
    return pl.pallas_call(
        _hgrn_kernel,
        grid=(bsz, nt),
        in_specs=[row, row, row, row, pl.BlockSpec((1, DV_A), lambda b, j: (0, 0)), sspec],
        out_specs=[orow, sspec],
        out_shape=[jax.ShapeDtypeStruct((bsz * seq, d), BF16),
                   jax.ShapeDtypeStruct((bsz, H_A, DK_A, DV_A), F32)],
        scratch_shapes=[pltpu.VMEM((H_A, DK_A, DV_A), F32)]
        + [pltpu.VMEM((tl, d), BF16)] * 5
        + [pltpu.VMEM((tl // HGRN_BLOCK, d), F32), pltpu.VMEM((tl, d), F32)],
        compiler_params=_cparams(("arbitrary", "arbitrary")),
        name="hgrn",
    )(qa, f, ia, sg, gn, s0)


SM_KB, SM_VB, SM_QI, SM_KI, SM_WI = 0, DH_B, 2 * DH_B, 2 * DH_B + H_I * D_I, 2 * DH_B + H_I * D_I + D_I
SM_USED = SM_WI + H_I
SM_WIDTH = -(-SM_USED // LANES) * LANES
INDEX_SCALE = (H_I * D_I) ** -0.5
NEG_INF = float("-inf")
LOG2_E = 1.4426950408889634


F32_EXP_MASK = 0x7F800000
F32_MIN_NORMAL = 0x00800000
SEARCH_UNROLL = 4


def _key_to_float(u):
    key = u ^ jnp.int32(INT_MIN)
    bits = jnp.where(key < 0, key ^ jnp.int32(0x7FFFFFFF), key)
    below_neg_inf = (bits < 0) & ((bits & jnp.int32(0x7FFFFFFF)) > jnp.int32(F32_EXP_MASK))
    return jnp.where(below_neg_inf, NEG_INF, lax.bitcast_convert_type(bits, F32))


def _positive_bits_to_float(bits):
    return jnp.where(bits < jnp.int32(F32_MIN_NORMAL), 0.0, lax.bitcast_convert_type(bits, F32))


def _count(mask):
    return jnp.sum(jnp.where(mask, 1.0, 0.0), axis=1, keepdims=True)


def _select_bias(score_scr, bias_scr, adm, s, n_sel):
    tq = score_scr.shape[0]

    def resolve_bit(i, t_u, cnt_t):
        cand_u = t_u | lax.shift_left(jnp.int32(1), 31 - i)
        cnt = _count(score_scr[:, :s] >= _key_to_float(cand_u))
        ok = cnt >= n_sel
        return jnp.where(ok, cand_u, t_u), jnp.where(ok, cnt, cnt_t)

    def unresolved(c):
        i, _, cnt_t = c
        return jnp.logical_and(i < 32, jnp.any(cnt_t != n_sel))

    def resolve_bits(c):
        i, t_u, cnt_t = c
        for b in range(SEARCH_UNROLL):
            t_u, cnt_t = resolve_bit(i + b, t_u, cnt_t)
        return i + SEARCH_UNROLL, t_u, cnt_t

    _, t_u, cnt_t = lax.while_loop(unresolved, resolve_bits,
                                   (jnp.int32(0), jnp.zeros((tq, 1), I32), jnp.full((tq, 1), float(s), F32)))
    t_f = _key_to_float(t_u)
    score = score_scr[:, :s]
    bias_scr[:, :s] = jnp.where((score >= t_f) & adm, 0.0, NEG_INF)

    @pl.when(jnp.any(cnt_t > n_sel))
    def _():
        above = score >= _key_to_float(t_u + 1)
        bucket = (score >= t_f) & jnp.logical_not(above)
        need = n_sel - _count(above)
        off = jnp.where(bucket, score - jnp.where(t_f == NEG_INF, 0.0, t_f), -1.0)

        def resolve_offset_bit(i, r_bits):
            cand = r_bits | lax.shift_left(jnp.int32(1), 30 - i)
            ok = _count(off >= _positive_bits_to_float(cand)) >= need
            return jnp.where(ok, cand, r_bits)

        ranked = jnp.any(off > 0.0)
        r_bits = lax.fori_loop(0, jnp.where(ranked, 31, 0), resolve_offset_bit,
                               jnp.full((tq, 1), jnp.where(ranked, 0, F32_MIN_NORMAL - 1), I32))
        above2 = off >= _positive_bits_to_float(r_bits + 1)
        tie = (off >= _positive_bits_to_float(r_bits)) & jnp.logical_not(above2)
        need2 = need - _count(above2)
        upper = (lax.broadcasted_iota(I32, (LANES, LANES), 0)
                 <= lax.broadcasted_iota(I32, (LANES, LANES), 1)).astype(BF16)
        carry = jnp.zeros((tq, 1), F32)
        for c in range(s // LANES):
            cs = slice(c * LANES, (c + 1) * LANES)
            rank = jnp.dot(jnp.where(tie[:, cs], 1.0, 0.0).astype(BF16), upper, preferred_element_type=F32) + carry
            sel = (above[:, cs] | above2[:, cs] | (tie[:, cs] & (rank <= need2))) & adm[:, cs]
            bias_scr[:, cs] = jnp.where(sel, 0.0, NEG_INF)
            carry = rank[:, LANES - 1:LANES]


def _dsa_tile(qb_ref, sm_ref, *refs, segs, search, n_sel):
    key_refs, (lim_ref, o_ref, score_scr, bias_scr) = refs[:3 * len(segs)], refs[3 * len(segs):]
    tq = qb_ref.shape[0]
    s = sum(segs)
    nt = (((1,), (1,)), ((), ()))
    adm = lax.broadcasted_iota(I32, (tq, s), 1) < lim_ref[...]
    seg_keys = [[key_refs[3 * g + j][0, :sg, :].astype(BF16) for j in range(3)] for g, sg in enumerate(segs)]
    starts = [sum(segs[:g]) for g in range(len(segs))]
    against_keys = lambda lhs, j: jnp.concatenate(
        [lax.dot_general(lhs, kv[j], nt, preferred_element_type=F32) for kv in seg_keys], axis=1)

    if search:
        score = jnp.zeros((tq, s), F32)
        for h in range(H_I):
            qih = sm_ref[:, SM_QI + h * D_I:SM_QI + (h + 1) * D_I].astype(BF16)
            wih = sm_ref[:, SM_WI + h:SM_WI + h + 1] * INDEX_SCALE
            score = score + jnp.maximum(against_keys(qih, 2), 0.0) * wih
        score_scr[:, :s] = jnp.where(adm, score, NEG_INF)
        _select_bias(score_scr, bias_scr, adm, s, n_sel)
    else:
        bias_scr[:, :s] = jnp.where(adm, 0.0, NEG_INF)

    for h in range(H_B):
        hs = slice(h * DH_B, (h + 1) * DH_B)
        logits = against_keys(qb_ref[:, hs], 0) + bias_scr[:, :s]
        p = jnp.exp2((logits - jnp.max(logits, axis=1, keepdims=True)) * (DH_B ** -0.5 * LOG2_E))
        denom = jnp.sum(p, axis=1, keepdims=True)
        pb = p.astype(BF16)
        o = sum(jnp.dot(pb[:, st:st + sg], kv[1], preferred_element_type=F32)
                for st, sg, kv in zip(starts, segs, seg_keys))
        o_ref[:, hs] = (o / denom).astype(BF16)


DSA_KEY_GRAN = 256


def _dsa(qb, sm, key_segments, limits, *, bsz, seq, row0, tq, n_sel):
    d = qb.shape[1]
    seg_rows = [seg[0].shape[1] for seg in key_segments]
    n_keys = sum(seg_rows)
    nt = seq // tq
    blk0 = row0 // tq
    limits = np.asarray(limits, np.int32)
    tile_max = limits.reshape(nt, tq).max(axis=1)
    per_tile = [(int(min(n_keys, -(-m // DSA_KEY_GRAN) * DSA_KEY_GRAN)), bool(m > n_sel)) for m in tile_max]
    lim2d = jnp.asarray(limits.reshape(seq, 1))
    outs, lo = [], 0
    while lo < nt:
        hi = lo
        while hi + 1 < nt and per_tile[hi + 1] == per_tile[lo]:
            hi += 1
        s, search = per_tile[lo]
        assert s == n_keys or len(key_segments) == 1
        segs = tuple(seg_rows) if s == n_keys else (s,)
        ntv = hi - lo + 1
        row = lambda w, lo=lo: pl.BlockSpec((tq, w), lambda b, j: (blk0 + b * nt + lo + j, 0))
        keys = lambda rows, w: pl.BlockSpec((1, rows, w), lambda b, j: (b, 0, 0))
        out = pl.pallas_call(
            functools.partial(_dsa_tile, segs=segs, search=search, n_sel=n_sel),
            grid=(bsz, ntv),
            in_specs=[row(d), row(sm.shape[1])]
            + [keys(rows, a.shape[2]) for rows, seg in zip(segs, key_segments) for a in seg]
            + [pl.BlockSpec((tq, 1), lambda b, j, lo=lo: (lo + j, 0))],
            out_specs=pl.BlockSpec((tq, d), lambda b, j, ntv=ntv: (b * ntv + j, 0)),
            out_shape=jax.ShapeDtypeStruct((bsz * ntv * tq, d), BF16),
            scratch_shapes=[pltpu.VMEM((tq, s), F32), pltpu.VMEM((tq, s), F32)],
            compiler_params=_cparams(("arbitrary", "arbitrary")),
            name="dsa",
        )(qb, sm, *[a for seg in key_segments for a in seg], lim2d)
        outs.append(out.reshape(bsz, ntv * tq, d))
        lo = hi + 1
    return jnp.concatenate(outs, axis=1).reshape(bsz * seq, d)


def _split_bf16(a):
    hi = a.astype(BF16)
    return hi, (a - hi.astype(F32)).astype(BF16)


def _mix_kernel(oa_ref, ob_ref, x_ref, woa_ref, wob_ref, wga_ref, wgb_ref, wout_ref, g1_ref, b1_ref,
                wrh_ref, wrl_ref, br_ref, x1_ref, eidx_ref, gate_ref, cnt_ref, *, alpha):
    i = pl.program_id(0)
    x = x_ref[...]
    xb = x.astype(BF16)
    dot = lambda a, w_ref: jnp.dot(a, w_ref[...], preferred_element_type=F32)
    y_a = dot(oa_ref[...], woa_ref)
    y_b = dot(ob_ref[...], wob_ref)
    merged = _sigmoid(dot(xb, wga_ref)) * y_a + _sigmoid(dot(xb, wgb_ref)) * y_b
    mixed = dot(merged.astype(BF16), wout_ref)
    x1 = _layer_norm(alpha * x + mixed, g1_ref[...], b1_ref[...])
    x1_ref[...] = x1

    nt = (((1,), (1,)), ((), ()))
    xh, xl = _split_bf16(x1)
    wh, wl = wrh_ref[...], wrl_ref[...]
    logits = (lax.dot_general(wh, xh, nt, preferred_element_type=F32)
              + lax.dot_general(wh, xl, nt, preferred_element_type=F32)
              + lax.dot_general(wl, xh, nt, preferred_element_type=F32)) + br_ref[...]
    ne, tm = logits.shape
    erow = lax.broadcasted_iota(I32, (ne, tm), 0)
    vals, idxs = [], []
    for _ in range(TOP_K):
        m = jnp.max(logits, axis=0, keepdims=True)
        idx = jnp.min(jnp.where(logits == m, erow, ne), axis=0, keepdims=True)
        vals.append(m)
        idxs.append(idx)
        logits = jnp.where(erow == idx, NEG_INF, logits)
    ex = [jnp.exp(v - vals[0]) for v in vals]
    denom = ex[0] + ex[1] + ex[2] + ex[3]
    pad = SUBLANES - TOP_K
    eidx = jnp.concatenate(idxs + [jnp.zeros((pad, tm), I32)], axis=0)
    eidx_ref[...] = eidx
    gate_ref[...] = jnp.concatenate([e / denom for e in ex] + [jnp.zeros((pad, tm), F32)], axis=0)

    onehot = jnp.zeros((ne, tm), F32)
    for idx in idxs:
        onehot = onehot + jnp.where(erow == idx, 1.0, 0.0)
    tile_cnt = jnp.broadcast_to(jnp.sum(onehot, axis=1, keepdims=True), cnt_ref.shape)

    @pl.when(i == 0)
    def _():
        cnt_ref[...] = tile_cnt

    @pl.when(i > 0)
    def _():
        cnt_ref[...] = cnt_ref[...] + tile_cnt


def _mix(oa, ob, x, woa, wob, wga, wgb, wout, g1, b1, wrh, wrl, br, *, tm, alpha):
    n, d = x.shape
    ne = wrh.shape[0]
    row = lambda w: pl.BlockSpec((tm, w), lambda i: (i, 0))
    col = pl.BlockSpec((SUBLANES, tm), lambda i: (0, i))
    full = lambda a: pl.BlockSpec(a.shape, lambda i: (0,) * a.ndim)
    return pl.pallas_call(
        functools.partial(_mix_kernel, alpha=alpha),
        grid=(n // tm,),
        in_specs=[row(d), row(d), row(d)] + [full(a) for a in (woa, wob, wga, wgb, wout, g1, b1, wrh, wrl, br)],
        out_specs=[row(d), col, col, pl.BlockSpec((ne, LANES), lambda i: (0, 0))],
        out_shape=[jax.ShapeDtypeStruct((n, d), F32), jax.ShapeDtypeStruct((SUBLANES, n), I32),
                   jax.ShapeDtypeStruct((SUBLANES, n), F32), jax.ShapeDtypeStruct((ne, LANES), F32)],
        compiler_params=_cparams(("arbitrary",)),
        name="mix",
    )(oa, ob, x, woa, wob, wga, wgb, wout, g1, b1, wrh, wrl, br)


def _sublane_cumsum(a):
    n = a.shape[0]
    row = lax.broadcasted_iota(I32, a.shape, 0)
    s = 1
    while s < n:
        a = a + jnp.where(row >= s, pltpu.roll(a, s, 0), 0.0)
        s *= 2
    return a


def _route_kernel(eidx_ref, cnt_ref, dest_ref, blke_ref, meta_ref, carry_scr, *, bm, sub):
    i = pl.program_id(0)
    ne = cnt_ref.shape[0]
    tl = eidx_ref.shape[1]

    @pl.when(i == 0)
    def _():
        carry_scr[...] = jnp.zeros_like(carry_scr)

    counts = cnt_ref[...]
    padded = jnp.ceil(counts / bm) * bm
    pend = _sublane_cumsum(padded)
    pstart = (pend - padded)[:, 0:1]

    @pl.when(i == 0)
    def _():
        nb = blke_ref.shape[1]
        first_row = (lax.broadcasted_iota(I32, (ne, nb), 1) * bm).astype(F32)
        below = jnp.sum(jnp.where(pend[:, 0:1] <= first_row, 1.0, 0.0), axis=0, keepdims=True)
        blke_ref[...] = jnp.minimum(below, ne - 1.0).astype(I32)
        pend_lanes = jnp.concatenate([pend, jnp.zeros((LANES - ne, LANES), F32)], axis=0).T[0:1, :]
        n_used = jnp.max(pend, axis=0, keepdims=True) / bm
        meta_ref[...] = jnp.concatenate([pend_lanes, n_used, jnp.zeros((SUBLANES - 2, LANES), F32)],
                                        axis=0).astype(I32)

    eidx = eidx_ref[...]
    erow = lax.broadcasted_iota(I32, (ne, tl), 0)
    hot = [jnp.where(erow == eidx[k:k + 1, :], 1.0, 0.0) for k in range(TOP_K)]
    onehot = hot[0] + hot[1] + hot[2] + hot[3]
    before = (lax.broadcasted_iota(I32, (tl, tl), 0) < lax.broadcasted_iota(I32, (tl, tl), 1)).astype(BF16)
    base = jnp.dot(onehot.astype(BF16), before, preferred_element_type=F32) + carry_scr[:, 0:1] + pstart
    dest = jnp.concatenate([jnp.sum(hk * base, axis=0, keepdims=True) for hk in hot], axis=0).astype(I32)
    for c in range(tl // sub):
        dest_ref[c] = dest[:, c * sub:(c + 1) * sub]
    carry_scr[...] = carry_scr[...] + jnp.sum(onehot, axis=1, keepdims=True)


def _route(eidx, counts, *, tl, sub, bm, nb):
    n = eidx.shape[1]
    ne = counts.shape[0]
    nb_pad = -(-nb // LANES) * LANES
    return pl.pallas_call(
        functools.partial(_route_kernel, bm=bm, sub=sub),
        grid=(n // tl,),
        in_specs=[pl.BlockSpec((SUBLANES, tl), lambda i: (0, i)), pl.BlockSpec(counts.shape, lambda i: (0, 0))],
        out_specs=[pl.BlockSpec((tl // sub, TOP_K, sub), lambda i: (i, 0, 0)),
                   pl.BlockSpec((1, nb_pad), lambda i: (0, 0)),
                   pl.BlockSpec((SUBLANES, LANES), lambda i: (0, 0))],
        out_shape=[jax.ShapeDtypeStruct((n // sub, TOP_K, sub), I32), jax.ShapeDtypeStruct((1, nb_pad), I32),
                   jax.ShapeDtypeStruct((SUBLANES, LANES), I32)],
        scratch_shapes=[pltpu.VMEM((ne, LANES), F32)],
        compiler_params=_cparams(("arbitrary",)),
        name="route",
    )(eidx, counts)


def _dispatch_kernel(dest_ref, meta_ref, x_ref, xs_ref, zbuf, sem, zsem, *, bm, ne, nb):
    i = pl.program_id(0)
    tm = x_ref.shape[0]

    @pl.when(i == 0)
    def _():
        zbuf[...] = jnp.zeros_like(zbuf)
        zero_block = lambda row0: pltpu.make_async_copy(zbuf, xs_ref.at[pl.ds(row0, bm), :], zsem)
        for e in range(ne):
            seg_end = meta_ref[0, e]
            seg_start = meta_ref[0, e - 1] if e > 0 else 0

            @pl.when(seg_end > seg_start)
            def _(seg_end=seg_end):
                last = zero_block(pl.multiple_of(seg_end - bm, bm))
                last.start()
                last.wait()

        def zero_tail(b, carry):
            tail = zero_block(pl.multiple_of(b * bm, bm))
            tail.start()
            tail.wait()
            return carry
        lax.fori_loop(meta_ref[1, 0], nb, zero_tail, 0)

    def row_copy(t, k):
        d = dest_ref[0, k, t]
        return pltpu.make_async_copy(x_ref.at[pl.ds(t, 1), :], xs_ref.at[pl.ds(d, 1), :], sem)

    def start(t, carry):
        for k in range(TOP_K):
            row_copy(t, k).start()
        return carry

    def wait(t, carry):
        for k in range(TOP_K):
            row_copy(t, k).wait()
        return carry

    lax.fori_loop(0, tm, start, 0)
    lax.fori_loop(0, tm, wait, 0)


def _dispatch(dest, meta, x1, n_rows, *, tm, bm, ne):
    n, d = x1.shape
    return pl.pallas_call(
        functools.partial(_dispatch_kernel, bm=bm, ne=ne, nb=n_rows // bm),
        grid=(n // tm,),
        in_specs=[pl.BlockSpec((1, TOP_K, tm), lambda i: (i, 0, 0), memory_space=pltpu.SMEM),
                  pl.BlockSpec(meta.shape, lambda i: (0, 0), memory_space=pltpu.SMEM),
                  pl.BlockSpec((tm, d), lambda i: (i, 0))],
        out_specs=pl.BlockSpec(memory_space=pl.ANY),
        out_shape=jax.ShapeDtypeStruct((n_rows, d), x1.dtype),
        scratch_shapes=[pltpu.VMEM((bm, d), x1.dtype), pltpu.SemaphoreType.DMA(()), pltpu.SemaphoreType.DMA(())],
        compiler_params=_cparams(("arbitrary",)),
        name="dispatch",
    )(dest, meta, x1)


def _experts_kernel(blke_ref, nused_ref, xs_ref, wu_ref, bu_ref, wd_ref, bd_ref, ys_ref, wu16, wd16):
    i = pl.program_id(0)
    dff = wd_ref.shape[1]

    @pl.when(i < nused_ref[0])
    def _():
        @pl.when(jnp.logical_or(i == 0, blke_ref[i] != blke_ref[jnp.maximum(i - 1, 0)]))
        def _():
            wu16[...] = wu_ref[0].astype(BF16)
            wd16[...] = wd_ref[0].astype(BF16)

        h = jnp.dot(xs_ref[...].astype(BF16), wu16[...], preferred_element_type=F32) + bu_ref[0]
        glu = jnp.minimum(h[:, :dff], SWIGLU_LIMIT)
        lin = jnp.clip(h[:, dff:], -SWIGLU_LIMIT, SWIGLU_LIMIT)
        act = glu * _sigmoid(SWIGLU_ALPHA * glu) * (lin + 1.0)
        ys_ref[...] = jnp.dot(act.astype(BF16), wd16[...], preferred_element_type=F32) + bd_ref[0]

    @pl.when(i >= nused_ref[0])
    def _():
        ys_ref[...] = jnp.zeros_like(ys_ref)


def _experts(blk_e, n_used, xs, wu, bu, wd, bd, *, bm):
    n_rows, d = xs.shape
    ne, _, dff2 = wu.shape
    dff = wd.shape[1]
    used = lambda i, nu: jnp.minimum(i, nu[0] - 1)
    grid_spec = pltpu.PrefetchScalarGridSpec(
        num_scalar_prefetch=2,
        grid=(n_rows // bm,),
        in_specs=[pl.BlockSpec((bm, d), lambda i, e, nu: (used(i, nu), 0)),
                  pl.BlockSpec((1, d, dff2), lambda i, e, nu: (e[i], 0, 0)),
                  pl.BlockSpec((1, 1, dff2), lambda i, e, nu: (e[i], 0, 0)),
                  pl.BlockSpec((1, dff, d), lambda i, e, nu: (e[i], 0, 0)),
                  pl.BlockSpec((1, 1, d), lambda i, e, nu: (e[i], 0, 0))],
        out_specs=pl.BlockSpec((bm, d), lambda i, e, nu: (i, 0)),
        scratch_shapes=[pltpu.VMEM((d, dff2), BF16), pltpu.VMEM((dff, d), BF16)],
    )
    return pl.pallas_call(
        _experts_kernel,
        grid_spec=grid_spec,
        out_shape=jax.ShapeDtypeStruct((n_rows, d), F32),
        compiler_params=_cparams(("arbitrary",)),
        name="experts",
    )(blk_e, n_used, xs, wu, bu.reshape(ne, 1, dff2), wd, bd.reshape(ne, 1, d))


def _combine_kernel(dest_ref, ys_ref, gate_ref, x1_ref, g2_ref, b2_ref, outp_ref, outs_ref, buf, sem, *, alpha, nt_p):
    i = pl.program_id(0)
    tm = x1_ref.shape[0]

    def row_copy(t, k):
        d = dest_ref[0, k, t]
        return pltpu.make_async_copy(ys_ref.at[pl.ds(d, 1), :], buf.at[k, pl.ds(t, 1), :], sem)

    def start(t, carry):
        for k in range(TOP_K):
            row_copy(t, k).start()
        return carry

    def wait(t, carry):
        for k in range(TOP_K):
            row_copy(t, k).wait()
        return carry

    lax.fori_loop(0, tm, start, 0)
    lax.fori_loop(0, tm, wait, 0)

    gate = gate_ref[...].T
    y = buf[0] * gate[:, 0:1]
    for k in range(1, TOP_K):
        y = y + buf[k] * gate[:, k:k + 1]
    out = _layer_norm(alpha * x1_ref[...] + y, g2_ref[...], b2_ref[...])

    @pl.when(i < nt_p)
    def _():
        outp_ref[...] = out

    @pl.when(i >= nt_p)
    def _():
        outs_ref[...] = out


def _combine(dest, ys, gate, x1, g2, b2, *, n_p, tm, alpha):
    n, d = x1.shape
    nt_p = n_p // tm
    return pl.pallas_call(
        functools.partial(_combine_kernel, alpha=alpha, nt_p=nt_p),
        grid=(n // tm,),
        in_specs=[pl.BlockSpec((1, TOP_K, tm), lambda i: (i, 0, 0), memory_space=pltpu.SMEM),
                  pl.BlockSpec(memory_space=pl.ANY),
                  pl.BlockSpec((SUBLANES, tm), lambda i: (0, i)),
                  pl.BlockSpec((tm, d), lambda i: (i, 0)),
                  pl.BlockSpec((1, d), lambda i: (0, 0)),
                  pl.BlockSpec((1, d), lambda i: (0, 0))],
        out_specs=[pl.BlockSpec((tm, d), lambda i: (jnp.minimum(i, nt_p - 1), 0)),
                   pl.BlockSpec((tm, d), lambda i: (jnp.maximum(i - nt_p, 0), 0))],
        out_shape=[jax.ShapeDtypeStruct((n_p, d), F32), jax.ShapeDtypeStruct((n - n_p, d), F32)],
        scratch_shapes=[pltpu.VMEM((TOP_K, tm, d), F32), pltpu.SemaphoreType.DMA(())],
        compiler_params=_cparams(("arbitrary",)),
        name="combine",
    )(dest, ys, gate, x1, g2, b2)


PROJ_TM = 256
HGRN_TL = 256
DSA_TQ = 128
MIX_TM = 256
ROUTE_TL = 1280
MOE_TM = 256
MOE_BM = 256


def kernel(x_prompt, x_sample, cache_k, cache_v, cache_kidx, state_hgrn, w_in, lb_logits, gn_a, w_oa, w_ob,
           w_out, ln1_g, ln1_b, w_router, b_router, w_up, b_up, w_down, b_down, ln2_g, ln2_b):
    depth = w_in.shape[0]
    assert depth == 1
    bsz, seq, d = x_prompt.shape
    dbsz, dseq, _ = x_sample.shape
    past = cache_k.shape[2]
    n_p, n_s = bsz * seq, dbsz * dseq
    n = n_p + n_s
    alpha = (2 * depth) ** 0.25

    sizes = (H_A * DK_A, H_A * DK_A, H_A * DV_A, H_A * DV_A, H_B * DH_B, DH_B, DH_B, H_I * D_I, H_I, D_I, d, d)
    offs = np.concatenate([[0], np.cumsum(sizes)])
    w = w_in[0].astype(BF16)
    grp = lambda g: w[:, offs[g]:offs[g + 1]]
    wqa, wfa, wia, wg, wqb, wkb, wvb, wqi, wwi, wki, wgta, wgtb = (grp(g) for g in range(12))
    wsm = jnp.concatenate([wkb, wvb, wqi, wki, wwi, jnp.zeros((d, SM_WIDTH - SM_USED), BF16)], axis=1)

    x_all = jnp.concatenate([x_prompt.reshape(n_p, d), x_sample.reshape(n_s, d)], axis=0)
    qa, f, ia, sg, qb, sm = _proj(x_all, wqa, wfa, wia, wg, wqb, wsm, lb_logits, PROJ_TM)

    gn = gn_a[0].reshape(1, DV_A)
    oa_p, sfin_p = _hgrn(qa, f, ia, sg, gn, jnp.zeros((bsz, H_A, DK_A, DV_A), F32),
                         bsz=bsz, seq=seq, row0=0, tl=HGRN_TL)
    oa_s, sfin_s = _hgrn(qa, f, ia, sg, gn, state_hgrn[0], bsz=dbsz, seq=dseq, row0=n_p, tl=dseq)

    kb_p = sm[:n_p, SM_KB:SM_KB + DH_B].reshape(bsz, seq, DH_B)
    vb_p = sm[:n_p, SM_VB:SM_VB + DH_B].reshape(bsz, seq, DH_B)
    ki_p = sm[:n_p, SM_KI:SM_KI + D_I].reshape(bsz, seq, D_I)
    kb_s = sm[n_p:, SM_KB:SM_KB + DH_B].reshape(dbsz, dseq, DH_B)
    vb_s = sm[n_p:, SM_VB:SM_VB + DH_B].reshape(dbsz, dseq, DH_B)
    ki_s = sm[n_p:, SM_KI:SM_KI + D_I].reshape(dbsz, dseq, D_I)
    lim_p = (np.arange(seq) // CHUNK + 1) * CHUNK
    ob_p = _dsa(qb, sm, [(kb_p.astype(BF16), vb_p.astype(BF16), ki_p.astype(BF16))], lim_p,
                bsz=bsz, seq=seq, row0=0, tq=DSA_TQ, n_sel=min(TOPK_MAX, seq // 4))
    n_keys = past + dseq
    new_pad = -(-dseq // LANES) * LANES - dseq
    new_keys = tuple(jnp.pad(a.astype(BF16), ((0, 0), (0, new_pad), (0, 0))) for a in (kb_s, vb_s, ki_s))
    lim_s = np.full((dseq,), n_keys)
    ob_s = _dsa(qb, sm, [(cache_k[0], cache_v[0], cache_kidx[0]), new_keys], lim_s,
                bsz=dbsz, seq=dseq, row0=n_p, tq=dseq, n_sel=min(TOPK_MAX, n_keys // 4))

    oa = jnp.concatenate([oa_p, oa_s], axis=0)
    ob = jnp.concatenate([ob_p, ob_s], axis=0)
    wrh, wrl = _split_bf16(w_router[0].T)
    x1, eidx, gate, counts = _mix(
        oa, ob, x_all, w_oa[0].astype(BF16), w_ob[0].astype(BF16), wgta, wgtb, w_out[0].astype(BF16),
        ln1_g[0].reshape(1, d), ln1_b[0].reshape(1, d), wrh, wrl, b_router[0].reshape(N_EXPERTS, 1),
        tm=MIX_TM, alpha=alpha)

    n_rows = -(-(n * TOP_K + N_EXPERTS * (MOE_BM - 1)) // MOE_BM) * MOE_BM
    dest, blk_e, meta = _route(eidx, counts, tl=ROUTE_TL, sub=MOE_TM, bm=MOE_BM, nb=n_rows // MOE_BM)
    xs = _dispatch(dest, meta, x1, n_rows, tm=MOE_TM, bm=MOE_BM, ne=N_EXPERTS)
    ys = _experts(blk_e[0, :n_rows // MOE_BM], meta[1, :1], xs, w_up[0], b_up[0], w_down[0], b_down[0], bm=MOE_BM)
    out_p, out_s = _combine(dest, ys, gate, x1, ln2_g[0].reshape(1, d), ln2_b[0].reshape(1, d),
                            n_p=n_p, tm=MOE_TM, alpha=alpha)

    return (out_p.reshape(bsz, seq, d), out_s.reshape(dbsz, dseq, d),
            kb_p[None], vb_p[None], ki_p[None], sfin_p[None],
            kb_s[None], vb_s[None], ki_s[None], sfin_s[None])
```

```python
import functools

import jax
import jax.numpy as jnp
import numpy as np
from jax import lax
from jax.experimental import pallas as pl
from jax.experimental.pallas import tpu as pltpu

F32 = jnp.float32
BF16 = jnp.bfloat16
I32 = jnp.int32

CHUNK = 64
H_A = 8
DK_A = 128
DV_A = 128
HGRN_BLOCK = 16
H_B = 8
DH_B = 128
H_I = 8
D_I = 64
TOPK_MAX = 256
N_EXPERTS = 32
TOP_K = 4
SWIGLU_ALPHA = 1.702
SWIGLU_LIMIT = 7.0
EPS = 1e-5

LANES = 128
SUBLANES = 8
VMEM_LIMIT_BYTES = 56 * 1024 * 1024

INT_MIN = -(2 ** 31)


def _cparams(sem):
    return pltpu.CompilerParams(dimension_semantics=sem, vmem_limit_bytes=VMEM_LIMIT_BYTES)


def _sigmoid(x):
    return 1.0 / (1.0 + jnp.exp(-x))


def _full_spec(shape):
    nd = len(shape)
    return pl.BlockSpec(shape, lambda *_: (0,) * nd)


def _layer_norm(x, g, b):
    mu = jnp.mean(x, axis=-1, keepdims=True)
    xc = x - mu
    var = jnp.mean(xc * xc, axis=-1, keepdims=True)
    return xc * lax.rsqrt(var + EPS) * g + b


def _two_group_specs(tm, width, nt_p):
    return (pl.BlockSpec((tm, width), lambda i: (jnp.minimum(i, nt_p - 1), 0)),
            pl.BlockSpec((tm, width), lambda i: (jnp.maximum(i - nt_p, 0), 0)))


def _group_tile(p_ref, s_ref, nt_p):
    return jnp.where(pl.program_id(0) < nt_p, p_ref[...], s_ref[...])


def _proj_kernel(xp_ref, xs_ref, wqa_ref, wfa_ref, wia_ref, wga_ref, wqb_ref, wsm_ref, lbl_ref,
                 qa_ref, f_ref, ia_ref, sg_ref, qb_ref, sm_ref, *, nt_p):
    xb = _group_tile(xp_ref, xs_ref, nt_p).astype(BF16)
    dot = lambda w_ref: jnp.dot(xb, w_ref[...], preferred_element_type=F32)
    qa_ref[...] = dot(wqa_ref)
    lbl = lbl_ref[...]
    lbe = jnp.exp(lbl - jnp.max(lbl, axis=0, keepdims=True))
    lb = lbe[0:1, :] / jnp.sum(lbe, axis=0, keepdims=True)
    f_ref[...] = lb + (1.0 - lb) * _sigmoid(dot(wfa_ref))
    ia_ref[...] = dot(wia_ref)
    g = dot(wga_ref)
    sg_ref[...] = g * _sigmoid(g)
    qb_ref[...] = dot(wqb_ref).astype(BF16)
    sm_ref[...] = dot(wsm_ref)


def _proj(x_p, x_s, wqa, wfa, wia, wga, wqb, wsm, lb_logits, tm):
    d = x_p.shape[1]
    n = x_p.shape[0] + x_s.shape[0]
    nt_p = x_p.shape[0] // tm
    nsm = wsm.shape[1]
    row = lambda w: pl.BlockSpec((tm, w), lambda i: (i, 0))
    wspec = lambda w: pl.BlockSpec(w.shape, lambda i: (0, 0))
    return pl.pallas_call(
        functools.partial(_proj_kernel, nt_p=nt_p),
        grid=(n // tm,),
        in_specs=[*_two_group_specs(tm, d, nt_p), wspec(wqa), wspec(wfa), wspec(wia), wspec(wga), wspec(wqb),
                  wspec(wsm), wspec(lb_logits)],
        out_specs=[row(d), row(d), row(d), row(d), row(d), row(nsm)],
        out_shape=[jax.ShapeDtypeStruct((n, d), F32)] * 4
        + [jax.ShapeDtypeStruct((n, d), BF16), jax.ShapeDtypeStruct((n, nsm), F32)],
        compiler_params=_cparams(("arbitrary",)),
        name="proj",
    )(x_p, x_s, wqa, wfa, wia, wga, wqb, wsm, lb_logits)


HGRN_UNROLL = 8


def _block_cumsum(a, row_in_block):
    s = 1
    while s < HGRN_BLOCK:
        a = a + jnp.where(row_in_block >= s, pltpu.roll(a, s, 0), 0.0)
        s *= 2
    return a


def _hgrn_kernel(qa_ref, f_ref, ia_ref, sg_ref, gn_ref, s0_ref, o_ref, sfin_ref,
                 s_scr, qd_scr, qm_scr, km_scr, kl_scr, v_scr, dl_scr, o_scr):
    j = pl.program_id(1)
    tl = qa_ref.shape[0]
    nblk = tl // HGRN_BLOCK
    half = HGRN_BLOCK // 2

    @pl.when(j == 0)
    def _():
        for h in range(H_A):
            s_scr[h] = s0_ref[0, h].T

    f = f_ref[...]
    q = qa_ref[...]
    d = f.shape[1]
    row_in_block = lax.broadcasted_iota(I32, (tl, d), 0) % HGRN_BLOCK
    cum = _block_cumsum(jnp.log(f), row_in_block)
    cum3 = cum.reshape(nblk, HGRN_BLOCK, d)
    mid = jnp.broadcast_to(cum3[:, half:half + 1, :], cum3.shape).reshape(tl, d)
    last3 = cum3[:, HGRN_BLOCK - 1:HGRN_BLOCK, :]
    last = jnp.broadcast_to(last3, cum3.shape).reshape(tl, d)
    k = 1.0 - f
    qd_scr[...] = (q * jnp.exp(cum)).astype(BF16)
    qm_scr[...] = (q * jnp.exp(cum - mid)).astype(BF16)
    km_scr[...] = (k * jnp.exp(mid - cum)).astype(BF16)
    kl_scr[...] = (k * jnp.exp(last - cum)).astype(BF16)
    dl_scr[...] = jnp.exp(last3.reshape(nblk, d))
    v_scr[...] = ia_ref[...].astype(BF16)

    nt = (((1,), (1,)), ((), ()))
    ri = lax.broadcasted_iota(I32, (tl, tl), 0)
    ci = lax.broadcasted_iota(I32, (tl, tl), 1)
    keep = (ri // HGRN_BLOCK == ci // HGRN_BLOCK) & (ri >= ci)
    for h in range(H_A):
        cols = slice(h * DK_A, (h + 1) * DK_A)
        vcols = slice(h * DV_A, (h + 1) * DV_A)
        att = lax.dot_general(qm_scr[:, cols], km_scr[:, cols], nt, preferred_element_type=F32)
        att = jnp.where(keep, att, 0.0).astype(BF16)
        o_scr[:, vcols] = jnp.dot(att, v_scr[:, vcols], preferred_element_type=F32)

    def block(b, carry):
        r0 = pl.multiple_of(b * HGRN_BLOCK, HGRN_BLOCK)
        rows = pl.ds(r0, HGRN_BLOCK)
        dl = dl_scr[pl.ds(b, 1), :]
        for h in range(H_A):
            cols = slice(h * DK_A, (h + 1) * DK_A)
            vcols = slice(h * DV_A, (h + 1) * DV_A)
            st_h = s_scr[h]
            o_scr[rows, vcols] += lax.dot_general(qd_scr[rows, cols], st_h.astype(BF16), nt,
                                                  preferred_element_type=F32)
            upd_t = lax.dot_general(v_scr[rows, vcols], kl_scr[rows, cols], (((0,), (0,)), ((), ())),
                                    preferred_element_type=F32)
            s_scr[h] = dl[:, cols] * st_h + upd_t
        return carry

    lax.fori_loop(0, nblk, block, 0, unroll=HGRN_UNROLL if nblk % HGRN_UNROLL == 0 else 1)

    gn = gn_ref[...]
    for h in range(H_A):
        vcols = slice(h * DV_A, (h + 1) * DV_A)
        o = o_scr[:, vcols]
        o = o * lax.rsqrt(jnp.mean(o * o, axis=-1, keepdims=True) + EPS) * gn
        o_ref[:, vcols] = (o * sg_ref[:, vcols]).astype(BF16)

    @pl.when(j == pl.num_programs(1) - 1)
    def _():
        for h in range(H_A):
            sfin_ref[0, h] = s_scr[h].T


def _hgrn(qa, f, ia, sg, gn, s0, *, bsz, seq, row0, tl):
    d = qa.shape[1]
    nt = seq // tl
    blk0 = row0 // tl
    row = pl.BlockSpec((tl, d), lambda b, j: (blk0 + b * nt + j, 0))
    orow = pl.BlockSpec((tl, d), lambda b, j: (b * nt + j, 0))
    sspec = pl.BlockSpec((1, H_A, DK_A, DV_A), lambda b, j: (b, 0, 0, 0))
    return pl.pallas_call(
        _hgrn_kernel,
        grid=(bsz, nt),
        in_specs=[row, row, row, row, pl.BlockSpec((1, DV_A), lambda b, j: (0, 0)), sspec],
        out_specs=[orow, sspec],
        out_shape=[jax.ShapeDtypeStruct((bsz * seq, d), BF16),
                   jax.ShapeDtypeStruct((bsz, H_A, DK_A, DV_A), F32)],
        scratch_shapes=[pltpu.VMEM((H_A, DK_A, DV_A), F32)]
        + [pltpu.VMEM((tl, d), BF16)] * 5
        + [pltpu.VMEM((tl // HGRN_BLOCK, d), F32), pltpu.VMEM((tl, d), F32)],
        compiler_params=_cparams(("arbitrary", "arbitrary")),
        name="hgrn",
    )(qa, f, ia, sg, gn, s0)


SM_KB, SM_VB, SM_QI, SM_KI, SM_WI = 0, DH_B, 2 * DH_B, 2 * DH_B + H_I * D_I, 2 * DH_B + H_I * D_I + D_I
SM_USED = SM_WI + H_I
SM_WIDTH = -(-SM_USED // LANES) * LANES
INDEX_SCALE = (H_I * D_I) ** -0.5
NEG_INF = float("-inf")
LOG2_E = 1.4426950408889634


F32_EXP_MASK = 0x7F800000
F32_MIN_NORMAL = 0x00800000
SEARCH_UNROLL = 4


def _key_to_float(u):
    key = u ^ jnp.int32(INT_MIN)
    bits = jnp.where(key < 0, key ^ jnp.int32(0x7FFFFFFF), key)
    below_neg_inf = (bits < 0) & ((bits & jnp.int32(0x7FFFFFFF)) > jnp.int32(F32_EXP_MASK))
    return jnp.where(below_neg_inf, NEG_INF, lax.bitcast_convert_type(bits, F32))


def _positive_bits_to_float(bits):
    return jnp.where(bits < jnp.int32(F32_MIN_NORMAL), 0.0, lax.bitcast_convert_type(bits, F32))


def _count(mask):
    return jnp.sum(jnp.where(mask, 1.0, 0.0), axis=1, keepdims=True)


def _select_bias(score_scr, bias_scr, adm, s, n_sel):
    tq = score_scr.shape[0]

    def resolve_bit(i, t_u, cnt_t):
        cand_u = t_u | lax.shift_left(jnp.int32(1), 31 - i)
        cnt = _count(score_scr[:, :s] >= _key_to_float(cand_u))
        ok = cnt >= n_sel
        return jnp.where(ok, cand_u, t_u), jnp.where(ok, cnt, cnt_t)

    def unresolved(c):
        i, _, cnt_t = c
        return jnp.logical_and(i < 32, jnp.any(cnt_t != n_sel))

    def resolve_bits(c):
        i, t_u, cnt_t = c
        for b in range(SEARCH_UNROLL):
            t_u, cnt_t = resolve_bit(i + b, t_u, cnt_t)
        return i + SEARCH_UNROLL, t_u, cnt_t

    _, t_u, cnt_t = lax.while_loop(unresolved, resolve_bits,
                                   (jnp.int32(0), jnp.zeros((tq, 1), I32), jnp.full((tq, 1), float(s), F32)))
    t_f = _key_to_float(t_u)
    score = score_scr[:, :s]
    bias_scr[:, :s] = jnp.where((score >= t_f) & adm, 0.0, NEG_INF)

    @pl.when(jnp.any(cnt_t > n_sel))
    def _():
        above = score >= _key_to_float(t_u + 1)
        bucket = (score >= t_f) & jnp.logical_not(above)
        need = n_sel - _count(above)
        off = jnp.where(bucket, score - jnp.where(t_f == NEG_INF, 0.0, t_f), -1.0)

        def resolve_offset_bit(i, r_bits):
            cand = r_bits | lax.shift_left(jnp.int32(1), 30 - i)
            ok = _count(off >= _positive_bits_to_float(cand)) >= need
            return jnp.where(ok, cand, r_bits)

        ranked = jnp.any(off > 0.0)
        r_bits = lax.fori_loop(0, jnp.where(ranked, 31, 0), resolve_offset_bit,
                               jnp.full((tq, 1), jnp.where(ranked, 0, F32_MIN_NORMAL - 1), I32))
        above2 = off >= _positive_bits_to_float(r_bits + 1)
        tie = (off >= _positive_bits_to_float(r_bits)) & jnp.logical_not(above2)
        need2 = need - _count(above2)
        upper = (lax.broadcasted_iota(I32, (LANES, LANES), 0)
                 <= lax.broadcasted_iota(I32, (LANES, LANES), 1)).astype(BF16)
        carry = jnp.zeros((tq, 1), F32)
        for c in range(s // LANES):
            cs = slice(c * LANES, (c + 1) * LANES)
            rank = jnp.dot(jnp.where(tie[:, cs], 1.0, 0.0).astype(BF16), upper, preferred_element_type=F32) + carry
            sel = (above[:, cs] | above2[:, cs] | (tie[:, cs] & (rank <= need2))) & adm[:, cs]
            bias_scr[:, cs] = jnp.where(sel, 0.0, NEG_INF)
            carry = rank[:, LANES - 1:LANES]


def _dsa_tile(qb_ref, sm_ref, *refs, segs, search, n_sel):
    key_refs, (lim_ref, o_ref, score_scr, bias_scr) = refs[:3 * len(segs)], refs[3 * len(segs):]
    tq = qb_ref.shape[0]
    s = sum(segs)
    nt = (((1,), (1,)), ((), ()))
    adm = lax.broadcasted_iota(I32, (tq, s), 1) < lim_ref[...]
    seg_keys = [[key_refs[3 * g + j][0, :sg, :].astype(BF16) for j in range(3)] for g, sg in enumerate(segs)]
    starts = [sum(segs[:g]) for g in range(len(segs))]
    against_keys = lambda lhs, j: jnp.concatenate(
        [lax.dot_general(lhs, kv[j], nt, preferred_element_type=F32) for kv in seg_keys], axis=1)

    if search:
        score = jnp.zeros((tq, s), F32)
        for h in range(H_I):
            qih = sm_ref[:, SM_QI + h * D_I:SM_QI + (h + 1) * D_I].astype(BF16)
            wih = sm_ref[:, SM_WI + h:SM_WI + h + 1] * INDEX_SCALE
            score = score + jnp.maximum(against_keys(qih, 2), 0.0) * wih
        score_scr[:, :s] = jnp.where(adm, score, NEG_INF)
        _select_bias(score_scr, bias_scr, adm, s, n_sel)
    else:
        bias_scr[:, :s] = jnp.where(adm, 0.0, NEG_INF)

    for h in range(H_B):
        hs = slice(h * DH_B, (h + 1) * DH_B)
        logits = against_keys(qb_ref[:, hs], 0) + bias_scr[:, :s]
        p = jnp.exp2((logits - jnp.max(logits, axis=1, keepdims=True)) * (DH_B ** -0.5 * LOG2_E))
        denom = jnp.sum(p, axis=1, keepdims=True)
        pb = p.astype(BF16)
        o = sum(jnp.dot(pb[:, st:st + sg], kv[1], preferred_element_type=F32)
                for st, sg, kv in zip(starts, segs, seg_keys))
        o_ref[:, hs] = (o / denom).astype(BF16)


DSA_KEY_GRAN = 256


def _dsa(qb, sm, key_segments, limits, *, bsz, seq, row0, tq, n_sel):
    d = qb.shape[1]
    seg_rows = [seg[0].shape[1] for seg in key_segments]
    n_keys = sum(seg_rows)
    nt = seq // tq
    blk0 = row0 // tq
    limits = np.asarray(limits, np.int32)
    tile_max = limits.reshape(nt, tq).max(axis=1)
    per_tile = [(int(min(n_keys, -(-m // DSA_KEY_GRAN) * DSA_KEY_GRAN)), bool(m > n_sel)) for m in tile_max]
    lim2d = jnp.asarray(limits.reshape(seq, 1))
    outs, lo = [], 0
    while lo < nt:
        hi = lo
        while hi + 1 < nt and per_tile[hi + 1] == per_tile[lo]:
            hi += 1
        s, search = per_tile[lo]
        assert s == n_keys or len(key_segments) == 1
        segs = tuple(seg_rows) if s == n_keys else (s,)
        ntv = hi - lo + 1
        row = lambda w, lo=lo: pl.BlockSpec((tq, w), lambda b, j: (blk0 + b * nt + lo + j, 0))
        keys = lambda rows, w: pl.BlockSpec((1, rows, w), lambda b, j: (b, 0, 0))
        out = pl.pallas_call(
            functools.partial(_dsa_tile, segs=segs, search=search, n_sel=n_sel),
            grid=(bsz, ntv),
            in_specs=[row(d), row(sm.shape[1])]
            + [keys(rows, a.shape[2]) for rows, seg in zip(segs, key_segments) for a in seg]
            + [pl.BlockSpec((tq, 1), lambda b, j, lo=lo: (lo + j, 0))],
            out_specs=pl.BlockSpec((tq, d), lambda b, j, ntv=ntv: (b * ntv + j, 0)),
            out_shape=jax.ShapeDtypeStruct((bsz * ntv * tq, d), BF16),
            scratch_shapes=[pltpu.VMEM((tq, s), F32), pltpu.VMEM((tq, s), F32)],
            compiler_params=_cparams(("arbitrary", "arbitrary")),
            name="dsa",
        )(qb, sm, *[a for seg in key_segments for a in seg], lim2d)
        outs.append(out.reshape(bsz, ntv * tq, d))
        lo = hi + 1
    return jnp.concatenate(outs, axis=1).reshape(bsz * seq, d)


def _split_bf16(a):
    hi = a.astype(BF16)
    return hi, (a - hi.astype(F32)).astype(BF16)


def _mix_kernel(oap_ref, oas_ref, obp_ref, obs_ref, xp_ref, xs_ref, woa_ref, wob_ref, wga_ref, wgb_ref, wout_ref,
                g1_ref, b1_ref, wrh_ref, wrl_ref, br_ref, x1_ref, eidx_ref, gate_ref, cnt_ref, *, alpha, nt_p):
    i = pl.program_id(0)
    x = _group_tile(xp_ref, xs_ref, nt_p)
    xb = x.astype(BF16)
    dot = lambda a, w_ref: jnp.dot(a, w_ref[...], preferred_element_type=F32)
    y_a = dot(_group_tile(oap_ref, oas_ref, nt_p), woa_ref)
    y_b = dot(_group_tile(obp_ref, obs_ref, nt_p), wob_ref)
    merged = _sigmoid(dot(xb, wga_ref)) * y_a + _sigmoid(dot(xb, wgb_ref)) * y_b
    mixed = dot(merged.astype(BF16), wout_ref)
    x1 = _layer_norm(alpha * x + mixed, g1_ref[...], b1_ref[...])
    x1_ref[...] = x1

    nt = (((1,), (1,)), ((), ()))
    xh, xl = _split_bf16(x1)
    wh, wl = wrh_ref[...], wrl_ref[...]
    logits = (lax.dot_general(wh, xh, nt, preferred_element_type=F32)
              + lax.dot_general(wh, xl, nt, preferred_element_type=F32)
              + lax.dot_general(wl, xh, nt, preferred_element_type=F32)) + br_ref[...]
    ne, tm = logits.shape
    erow = lax.broadcasted_iota(I32, (ne, tm), 0)
    vals, idxs = [], []
    for _ in range(TOP_K):
        m = jnp.max(logits, axis=0, keepdims=True)
        idx = jnp.min(jnp.where(logits == m, erow, ne), axis=0, keepdims=True)
        vals.append(m)
        idxs.append(idx)
        logits = jnp.where(erow == idx, NEG_INF, logits)
    ex = [jnp.exp(v - vals[0]) for v in vals]
    denom = ex[0] + ex[1] + ex[2] + ex[3]
    pad = SUBLANES - TOP_K
    eidx = jnp.concatenate(idxs + [jnp.zeros((pad, tm), I32)], axis=0)
    eidx_ref[...] = eidx
    gate_ref[...] = jnp.concatenate([e / denom for e in ex] + [jnp.zeros((pad, tm), F32)], axis=0)

    onehot = jnp.zeros((ne, tm), F32)
    for idx in idxs:
        onehot = onehot + jnp.where(erow == idx, 1.0, 0.0)
    tile_cnt = jnp.broadcast_to(jnp.sum(onehot, axis=1, keepdims=True), cnt_ref.shape)

    @pl.when(i == 0)
    def _():
        cnt_ref[...] = tile_cnt

    @pl.when(i > 0)
    def _():
        cnt_ref[...] = cnt_ref[...] + tile_cnt


def _mix(oa, ob, x, woa, wob, wga, wgb, wout, g1, b1, wrh, wrl, br, *, tm, alpha):
    d = x[0].shape[1]
    n = x[0].shape[0] + x[1].shape[0]
    nt_p = x[0].shape[0] // tm
    ne = wrh.shape[0]
    row = lambda w: pl.BlockSpec((tm, w), lambda i: (i, 0))
    col = pl.BlockSpec((SUBLANES, tm), lambda i: (0, i))
    full = lambda a: pl.BlockSpec(a.shape, lambda i: (0,) * a.ndim)
    return pl.pallas_call(
        functools.partial(_mix_kernel, alpha=alpha, nt_p=nt_p),
        grid=(n // tm,),
        in_specs=[*_two_group_specs(tm, d, nt_p)] * 3
        + [full(a) for a in (woa, wob, wga, wgb, wout, g1, b1, wrh, wrl, br)],
        out_specs=[row(d), col, col, pl.BlockSpec((ne, LANES), lambda i: (0, 0))],
        out_shape=[jax.ShapeDtypeStruct((n, d), F32), jax.ShapeDtypeStruct((SUBLANES, n), I32),
                   jax.ShapeDtypeStruct((SUBLANES, n), F32), jax.ShapeDtypeStruct((ne, LANES), F32)],
        compiler_params=_cparams(("arbitrary",)),
        name="mix",
    )(*oa, *ob, *x, woa, wob, wga, wgb, wout, g1, b1, wrh, wrl, br)


def _sublane_cumsum(a):
    n = a.shape[0]
    row = lax.broadcasted_iota(I32, a.shape, 0)
    s = 1
    while s < n:
        a = a + jnp.where(row >= s, pltpu.roll(a, s, 0), 0.0)
        s *= 2
    return a


def _route_kernel(eidx_ref, cnt_ref, dest_ref, blke_ref, meta_ref, carry_scr, *, bm, sub):
    i = pl.program_id(0)
    ne = cnt_ref.shape[0]
    tl = eidx_ref.shape[1]

    @pl.when(i == 0)
    def _():
        carry_scr[...] = jnp.zeros_like(carry_scr)

    counts = cnt_ref[...]
    padded = jnp.ceil(counts / bm) * bm
    pend = _sublane_cumsum(padded)
    pstart = (pend - padded)[:, 0:1]

    @pl.when(i == 0)
    def _():
        nb = blke_ref.shape[1]
        first_row = (lax.broadcasted_iota(I32, (ne, nb), 1) * bm).astype(F32)
        below = jnp.sum(jnp.where(pend[:, 0:1] <= first_row, 1.0, 0.0), axis=0, keepdims=True)
        blke_ref[...] = jnp.minimum(below, ne - 1.0).astype(I32)
        pend_lanes = jnp.concatenate([pend, jnp.zeros((LANES - ne, LANES), F32)], axis=0).T[0:1, :]
        n_used = jnp.max(pend, axis=0, keepdims=True) / bm
        meta_ref[...] = jnp.concatenate([pend_lanes, n_used, jnp.zeros((SUBLANES - 2, LANES), F32)],
                                        axis=0).astype(I32)

    eidx = eidx_ref[...]
    erow = lax.broadcasted_iota(I32, (ne, tl), 0)
    hot = [jnp.where(erow == eidx[k:k + 1, :], 1.0, 0.0) for k in range(TOP_K)]
    onehot = hot[0] + hot[1] + hot[2] + hot[3]
    before = (lax.broadcasted_iota(I32, (tl, tl), 0) < lax.broadcasted_iota(I32, (tl, tl), 1)).astype(BF16)
    base = jnp.dot(onehot.astype(BF16), before, preferred_element_type=F32) + carry_scr[:, 0:1] + pstart
    dest = jnp.concatenate([jnp.sum(hk * base, axis=0, keepdims=True) for hk in hot], axis=0).astype(I32)
    for c in range(tl // sub):
        dest_ref[c] = dest[:, c * sub:(c + 1) * sub]
    carry_scr[...] = carry_scr[...] + jnp.sum(onehot, axis=1, keepdims=True)


def _route(eidx, counts, *, tl, sub, bm, nb):
    n = eidx.shape[1]
    ne = counts.shape[0]
    nb_pad = -(-nb // LANES) * LANES
    return pl.pallas_call(
        functools.partial(_route_kernel, bm=bm, sub=sub),
        grid=(n // tl,),
        in_specs=[pl.BlockSpec((SUBLANES, tl), lambda i: (0, i)), pl.BlockSpec(counts.shape, lambda i: (0, 0))],
        out_specs=[pl.BlockSpec((tl // sub, TOP_K, sub), lambda i: (i, 0, 0)),
                   pl.BlockSpec((1, nb_pad), lambda i: (0, 0)),
                   pl.BlockSpec((SUBLANES, LANES), lambda i: (0, 0))],
        out_shape=[jax.ShapeDtypeStruct((n // sub, TOP_K, sub), I32), jax.ShapeDtypeStruct((1, nb_pad), I32),
                   jax.ShapeDtypeStruct((SUBLANES, LANES), I32)],
        scratch_shapes=[pltpu.VMEM((ne, LANES), F32)],
        compiler_params=_cparams(("arbitrary",)),
        name="route",
    )(eidx, counts)


def _dispatch_kernel(dest_ref, meta_ref, x_ref, xs_ref, zbuf, sem, zsem, *, bm, ne, nb):
    i = pl.program_id(0)
    tm = x_ref.shape[0]

    @pl.when(i == 0)
    def _():
        zbuf[...] = jnp.zeros_like(zbuf)
        zero_block = lambda row0: pltpu.make_async_copy(zbuf, xs_ref.at[pl.ds(row0, bm), :], zsem)
        for e in range(ne):
            seg_end = meta_ref[0, e]
            seg_start = meta_ref[0, e - 1] if e > 0 else 0

            @pl.when(seg_end > seg_start)
            def _(seg_end=seg_end):
                last = zero_block(pl.multiple_of(seg_end - bm, bm))
                last.start()
                last.wait()

        def zero_tail(b, carry):
            tail = zero_block(pl.multiple_of(b * bm, bm))
            tail.start()
            tail.wait()
            return carry
        lax.fori_loop(meta_ref[1, 0], nb, zero_tail, 0)

    def row_copy(t, k):
        d = dest_ref[0, k, t]
        return pltpu.make_async_copy(x_ref.at[pl.ds(t, 1), :], xs_ref.at[pl.ds(d, 1), :], sem)

    def start(t, carry):
        for k in range(TOP_K):
            row_copy(t, k).start()
        return carry

    def wait(t, carry):
        for k in range(TOP_K):
            row_copy(t, k).wait()
        return carry

    lax.fori_loop(0, tm, start, 0)
    lax.fori_loop(0, tm, wait, 0)


def _dispatch(dest, meta, x1, n_rows, *, tm, bm, ne):
    n, d = x1.shape
    return pl.pallas_call(
        functools.partial(_dispatch_kernel, bm=bm, ne=ne, nb=n_rows // bm),
        grid=(n // tm,),
        in_specs=[pl.BlockSpec((1, TOP_K, tm), lambda i: (i, 0, 0), memory_space=pltpu.SMEM),
                  pl.BlockSpec(meta.shape, lambda i: (0, 0), memory_space=pltpu.SMEM),
                  pl.BlockSpec((tm, d), lambda i: (i, 0))],
        out_specs=pl.BlockSpec(memory_space=pl.ANY),
        out_shape=jax.ShapeDtypeStruct((n_rows, d), x1.dtype),
        scratch_shapes=[pltpu.VMEM((bm, d), x1.dtype), pltpu.SemaphoreType.DMA(()), pltpu.SemaphoreType.DMA(())],
        compiler_params=_cparams(("arbitrary",)),
        name="dispatch",
    )(dest, meta, x1)


def _experts_kernel(blke_ref, nused_ref, xs_ref, wu_ref, bu_ref, wd_ref, bd_ref, ys_ref, wu16, wd16):
    i = pl.program_id(0)
    dff = wd_ref.shape[1]

    @pl.when(i < nused_ref[0])
    def _():
        @pl.when(jnp.logical_or(i == 0, blke_ref[i] != blke_ref[jnp.maximum(i - 1, 0)]))
        def _():
            wu16[...] = wu_ref[0].astype(BF16)
            wd16[...] = wd_ref[0].astype(BF16)

        h = jnp.dot(xs_ref[...].astype(BF16), wu16[...], preferred_element_type=F32) + bu_ref[0]
        glu = jnp.minimum(h[:, :dff], SWIGLU_LIMIT)
        lin = jnp.clip(h[:, dff:], -SWIGLU_LIMIT, SWIGLU_LIMIT)
        act = glu * _sigmoid(SWIGLU_ALPHA * glu) * (lin + 1.0)
        ys_ref[...] = jnp.dot(act.astype(BF16), wd16[...], preferred_element_type=F32) + bd_ref[0]

    @pl.when(i >= nused_ref[0])
    def _():
        ys_ref[...] = jnp.zeros_like(ys_ref)


def _experts(blk_e, n_used, xs, wu, bu, wd, bd, *, bm):
    n_rows, d = xs.shape
    ne, _, dff2 = wu.shape
    dff = wd.shape[1]
    used = lambda i, nu: jnp.minimum(i, nu[0] - 1)
    grid_spec = pltpu.PrefetchScalarGridSpec(
        num_scalar_prefetch=2,
        grid=(n_rows // bm,),
        in_specs=[pl.BlockSpec((bm, d), lambda i, e, nu: (used(i, nu), 0)),
                  pl.BlockSpec((1, d, dff2), lambda i, e, nu: (e[i], 0, 0)),
                  pl.BlockSpec((1, 1, dff2), lambda i, e, nu: (e[i], 0, 0)),
                  pl.BlockSpec((1, dff, d), lambda i, e, nu: (e[i], 0, 0)),
                  pl.BlockSpec((1, 1, d), lambda i, e, nu: (e[i], 0, 0))],
        out_specs=pl.BlockSpec((bm, d), lambda i, e, nu: (i, 0)),
        scratch_shapes=[pltpu.VMEM((d, dff2), BF16), pltpu.VMEM((dff, d), BF16)],
    )
    return pl.pallas_call(
        _experts_kernel,
        grid_spec=grid_spec,
        out_shape=jax.ShapeDtypeStruct((n_rows, d), F32),
        compiler_params=_cparams(("arbitrary",)),
        name="experts",
    )(blk_e, n_used, xs, wu, bu.reshape(ne, 1, dff2), wd, bd.reshape(ne, 1, d))


def _combine_kernel(dest_ref, ys_ref, gate_ref, x1_ref, g2_ref, b2_ref, outp_ref, outs_ref, buf, sem, *, alpha, nt_p):
    i = pl.program_id(0)
    tm = x1_ref.shape[0]

    def row_copy(t, k):
        d = dest_ref[0, k, t]
        return pltpu.make_async_copy(ys_ref.at[pl.ds(d, 1), :], buf.at[k, pl.ds(t, 1), :], sem)

    def start(t, carry):
        for k in range(TOP_K):
            row_copy(t, k).start()
        return carry

    def wait(t, carry):
        for k in range(TOP_K):
            row_copy(t, k).wait()
        return carry

    lax.fori_loop(0, tm, start, 0)
    lax.fori_loop(0, tm, wait, 0)

    gate = gate_ref[...].T
    y = buf[0] * gate[:, 0:1]
    for k in range(1, TOP_K):
        y = y + buf[k] * gate[:, k:k + 1]
    out = _layer_norm(alpha * x1_ref[...] + y, g2_ref[...], b2_ref[...])

    @pl.when(i < nt_p)
    def _():
        outp_ref[...] = out

    @pl.when(i >= nt_p)
    def _():
        outs_ref[...] = out


def _combine(dest, ys, gate, x1, g2, b2, *, n_p, tm, alpha):
    n, d = x1.shape
    nt_p = n_p // tm
    return pl.pallas_call(
        functools.partial(_combine_kernel, alpha=alpha, nt_p=nt_p),
        grid=(n // tm,),
        in_specs=[pl.BlockSpec((1, TOP_K, tm), lambda i: (i, 0, 0), memory_space=pltpu.SMEM),
                  pl.BlockSpec(memory_space=pl.ANY),
                  pl.BlockSpec((SUBLANES, tm), lambda i: (0, i)),
                  pl.BlockSpec((tm, d), lambda i: (i, 0)),
                  pl.BlockSpec((1, d), lambda i: (0, 0)),
                  pl.BlockSpec((1, d), lambda i: (0, 0))],
        out_specs=[pl.BlockSpec((tm, d), lambda i: (jnp.minimum(i, nt_p - 1), 0)),
                   pl.BlockSpec((tm, d), lambda i: (jnp.maximum(i - nt_p, 0), 0))],
        out_shape=[jax.ShapeDtypeStruct((n_p, d), F32), jax.ShapeDtypeStruct((n - n_p, d), F32)],
        scratch_shapes=[pltpu.VMEM((TOP_K, tm, d), F32), pltpu.SemaphoreType.DMA(())],
        compiler_params=_cparams(("arbitrary",)),
        name="combine",
    )(dest, ys, gate, x1, g2, b2)


PROJ_TM = 256
HGRN_TL = 256
DSA_TQ = 256
MIX_TM = 256
ROUTE_TL = 1280
MOE_TM = 256
MOE_BM = 256


def kernel(x_prompt, x_sample, cache_k, cache_v, cache_kidx, state_hgrn, w_in, lb_logits, gn_a, w_oa, w_ob,
           w_out, ln1_g, ln1_b, w_router, b_router, w_up, b_up, w_down, b_down, ln2_g, ln2_b):
    depth = w_in.shape[0]
    assert depth == 1
    bsz, seq, d = x_prompt.shape
    dbsz, dseq, _ = x_sample.shape
    past = cache_k.shape[2]
    n_p, n_s = bsz * seq, dbsz * dseq
    n = n_p + n_s
    alpha = (2 * depth) ** 0.25

    sizes = (H_A * DK_A, H_A * DK_A, H_A * DV_A, H_A * DV_A, H_B * DH_B, DH_B, DH_B, H_I * D_I, H_I, D_I, d, d)
    offs = np.concatenate([[0], np.cumsum(sizes)])
    w = w_in[0].astype(BF16)
    grp = lambda g: w[:, offs[g]:offs[g + 1]]
    wqa, wfa, wia, wg, wqb, wkb, wvb, wqi, wwi, wki, wgta, wgtb = (grp(g) for g in range(12))
    wsm = jnp.concatenate([wkb, wvb, wqi, wki, wwi, jnp.zeros((d, SM_WIDTH - SM_USED), BF16)], axis=1)

    x_groups = (x_prompt.reshape(n_p, d), x_sample.reshape(n_s, d))
    qa, f, ia, sg, qb, sm = _proj(*x_groups, wqa, wfa, wia, wg, wqb, wsm, lb_logits, PROJ_TM)

    gn = gn_a[0].reshape(1, DV_A)
    oa_p, sfin_p = _hgrn(qa, f, ia, sg, gn, jnp.zeros((bsz, H_A, DK_A, DV_A), F32),
                         bsz=bsz, seq=seq, row0=0, tl=HGRN_TL)
    oa_s, sfin_s = _hgrn(qa, f, ia, sg, gn, state_hgrn[0], bsz=dbsz, seq=dseq, row0=n_p, tl=dseq)

    kb_p = sm[:n_p, SM_KB:SM_KB + DH_B].reshape(bsz, seq, DH_B)
    vb_p = sm[:n_p, SM_VB:SM_VB + DH_B].reshape(bsz, seq, DH_B)
    ki_p = sm[:n_p, SM_KI:SM_KI + D_I].reshape(bsz, seq, D_I)
    kb_s = sm[n_p:, SM_KB:SM_KB + DH_B].reshape(dbsz, dseq, DH_B)
    vb_s = sm[n_p:, SM_VB:SM_VB + DH_B].reshape(dbsz, dseq, DH_B)
    ki_s = sm[n_p:, SM_KI:SM_KI + D_I].reshape(dbsz, dseq, D_I)
    lim_p = (np.arange(seq) // CHUNK + 1) * CHUNK
    ob_p = _dsa(qb, sm, [(kb_p.astype(BF16), vb_p.astype(BF16), ki_p.astype(BF16))], lim_p,
                bsz=bsz, seq=seq, row0=0, tq=DSA_TQ, n_sel=min(TOPK_MAX, seq // 4))
    n_keys = past + dseq
    new_pad = -(-dseq // LANES) * LANES - dseq
    new_keys = tuple(jnp.pad(a.astype(BF16), ((0, 0), (0, new_pad), (0, 0))) for a in (kb_s, vb_s, ki_s))
    lim_s = np.full((dseq,), n_keys)
    ob_s = _dsa(qb, sm, [(cache_k[0], cache_v[0], cache_kidx[0]), new_keys], lim_s,
                bsz=dbsz, seq=dseq, row0=n_p, tq=dseq, n_sel=min(TOPK_MAX, n_keys // 4))

    wrh, wrl = _split_bf16(w_router[0].T)
    x1, eidx, gate, counts = _mix(
        (oa_p, oa_s), (ob_p, ob_s), x_groups, w_oa[0].astype(BF16), w_ob[0].astype(BF16), wgta, wgtb, w_out[0].astype(BF16),
        ln1_g[0].reshape(1, d), ln1_b[0].reshape(1, d), wrh, wrl, b_router[0].reshape(N_EXPERTS, 1),
        tm=MIX_TM, alpha=alpha)

    n_rows = -(-(n * TOP_K + N_EXPERTS * (MOE_BM - 1)) // MOE_BM) * MOE_BM
    dest, blk_e, meta = _route(eidx, counts, tl=ROUTE_TL, sub=MOE_TM, bm=MOE_BM, nb=n_rows // MOE_BM)
    xs = _dispatch(dest, meta, x1, n_rows, tm=MOE_TM, bm=MOE_BM, ne=N_EXPERTS)
    ys = _experts(blk_e[0, :n_rows // MOE_BM], meta[1, :1], xs, w_up[0], b_up[0], w_down[0], b_down[0], bm=MOE_BM)
    out_p, out_s = _combine(dest, ys, gate, x1, ln2_g[0].reshape(1, d), ln2_b[0].reshape(1, d),
                            n_p=n_p, tm=MOE_TM, alpha=alpha)

    return (out_p.reshape(bsz, seq, d), out_s.reshape(dbsz, dseq, d),
            kb_p[None], vb_p[None], ki_p[None], sfin_p[None],
            kb_s[None], vb_s[None], ki_s[None], sfin_s[None])
```

```python
import functools

import jax
import jax.numpy as jnp
import numpy as np
from jax import lax
from jax.experimental import pallas as pl
from jax.experimental.pallas import tpu as pltpu

F32 = jnp.float32
BF16 = jnp.bfloat16
I32 = jnp.int32

CHUNK = 64
H_A = 8
DK_A = 128
DV_A = 128
HGRN_BLOCK = 16
H_B = 8
DH_B = 128
H_I = 8
D_I = 64
TOPK_MAX = 256
N_EXPERTS = 32
TOP_K = 4
SWIGLU_ALPHA = 1.702
SWIGLU_LIMIT = 7.0
EPS = 1e-5

LANES = 128
SUBLANES = 8
VMEM_LIMIT_BYTES = 56 * 1024 * 1024

INT_MIN = -(2 ** 31)


def _cparams(sem):
    return pltpu.CompilerParams(dimension_semantics=sem, vmem_limit_bytes=VMEM_LIMIT_BYTES)


def _sigmoid(x):
    return 1.0 / (1.0 + jnp.exp(-x))


def _full_spec(shape):
    nd = len(shape)
    return pl.BlockSpec(shape, lambda *_: (0,) * nd)


def _layer_norm(x, g, b):
    mu = jnp.mean(x, axis=-1, keepdims=True)
    xc = x - mu
    var = jnp.mean(xc * xc, axis=-1, keepdims=True)
    return xc * lax.rsqrt(var + EPS) * g + b


def _two_group_specs(tm, width, nt_p):
    return (pl.BlockSpec((tm, width), lambda i: (jnp.minimum(i, nt_p - 1), 0)),
            pl.BlockSpec((tm, width), lambda i: (jnp.maximum(i - nt_p, 0), 0)))


def _group_tile(p_ref, s_ref, nt_p):
    return jnp.where(pl.program_id(0) < nt_p, p_ref[...], s_ref[...])


def _proj_kernel(xp_ref, xs_ref, wqa_ref, wfa_ref, wia_ref, wga_ref, wqb_ref, wsm_ref, lbl_ref,
                 qa_ref, f_ref, ia_ref, sg_ref, qb_ref, sm_ref, *, nt_p):
    xb = _group_tile(xp_ref, xs_ref, nt_p).astype(BF16)
    dot = lambda w_ref: jnp.dot(xb, w_ref[...], preferred_element_type=F32)
    qa_ref[...] = dot(wqa_ref)
    lbl = lbl_ref[...]
    lbe = jnp.exp(lbl - jnp.max(lbl, axis=0, keepdims=True))
    lb = lbe[0:1, :] / jnp.sum(lbe, axis=0, keepdims=True)
    f_ref[...] = lb + (1.0 - lb) * _sigmoid(dot(wfa_ref))
    ia_ref[...] = dot(wia_ref)
    g = dot(wga_ref)
    sg_ref[...] = g * _sigmoid(g)
    qb_ref[...] = dot(wqb_ref).astype(BF16)
    sm_ref[...] = dot(wsm_ref)


def _proj(x_p, x_s, wqa, wfa, wia, wga, wqb, wsm, lb_logits, tm):
    d = x_p.shape[1]
    n = x_p.shape[0] + x_s.shape[0]
    nt_p = x_p.shape[0] // tm
    nsm = wsm.shape[1]
    row = lambda w: pl.BlockSpec((tm, w), lambda i: (i, 0))
    wspec = lambda w: pl.BlockSpec(w.shape, lambda i: (0, 0))
    return pl.pallas_call(
        functools.partial(_proj_kernel, nt_p=nt_p),
        grid=(n // tm,),
        in_specs=[*_two_group_specs(tm, d, nt_p), wspec(wqa), wspec(wfa), wspec(wia), wspec(wga), wspec(wqb),
                  wspec(wsm), wspec(lb_logits)],
        out_specs=[row(d), row(d), row(d), row(d), row(d), row(nsm)],
        out_shape=[jax.ShapeDtypeStruct((n, d), F32)] * 4
        + [jax.ShapeDtypeStruct((n, d), BF16), jax.ShapeDtypeStruct((n, nsm), F32)],
        compiler_params=_cparams(("arbitrary",)),
        name="proj",
    )(x_p, x_s, wqa, wfa, wia, wga, wqb, wsm, lb_logits)


HGRN_UNROLL = 8


def _block_cumsum(a, row_in_block):
    s = 1
    while s < HGRN_BLOCK:
        a = a + jnp.where(row_in_block >= s, pltpu.roll(a, s, 0), 0.0)
        s *= 2
    return a


def _hgrn_kernel(qa_ref, f_ref, ia_ref, sg_ref, gn_ref, s0_ref, o_ref, sfin_ref,
                 s_scr, qd_scr, qm_scr, km_scr, kl_scr, v_scr, dl_scr, o_scr):
    j = pl.program_id(1)
    tl = qa_ref.shape[0]
    nblk = tl // HGRN_BLOCK
    half = HGRN_BLOCK // 2

    @pl.when(j == 0)
    def _():
        for h in range(H_A):
            s_scr[h] = s0_ref[0, h].T

    f = f_ref[...]
    q = qa_ref[...]
    d = f.shape[1]
    row_in_block = lax.broadcasted_iota(I32, (tl, d), 0) % HGRN_BLOCK
    cum = _block_cumsum(jnp.log(f), row_in_block)
    cum3 = cum.reshape(nblk, HGRN_BLOCK, d)
    mid = jnp.broadcast_to(cum3[:, half:half + 1, :], cum3.shape).reshape(tl, d)
    last3 = cum3[:, HGRN_BLOCK - 1:HGRN_BLOCK, :]
    last = jnp.broadcast_to(last3, cum3.shape).reshape(tl, d)
    k = 1.0 - f
    qd_scr[...] = (q * jnp.exp(cum)).astype(BF16)
    qm_scr[...] = (q * jnp.exp(cum - mid)).astype(BF16)
    km_scr[...] = (k * jnp.exp(mid - cum)).astype(BF16)
    kl_scr[...] = (k * jnp.exp(last - cum)).astype(BF16)
    dl_scr[...] = jnp.exp(last3.reshape(nblk, d))
    v_scr[...] = ia_ref[...].astype(BF16)

    nt = (((1,), (1,)), ((), ()))
    ri = lax.broadcasted_iota(I32, (tl, tl), 0)
    ci = lax.broadcasted_iota(I32, (tl, tl), 1)
    keep = (ri // HGRN_BLOCK == ci // HGRN_BLOCK) & (ri >= ci)
    for h in range(H_A):
        cols = slice(h * DK_A, (h + 1) * DK_A)
        vcols = slice(h * DV_A, (h + 1) * DV_A)
        att = lax.dot_general(qm_scr[:, cols], km_scr[:, cols], nt, preferred_element_type=F32)
        att = jnp.where(keep, att, 0.0).astype(BF16)
        o_scr[:, vcols] = jnp.dot(att, v_scr[:, vcols], preferred_element_type=F32)

    def block(b, carry):
        r0 = pl.multiple_of(b * HGRN_BLOCK, HGRN_BLOCK)
        rows = pl.ds(r0, HGRN_BLOCK)
        dl = dl_scr[pl.ds(b, 1), :]
        for h in range(H_A):
            cols = slice(h * DK_A, (h + 1) * DK_A)
            vcols = slice(h * DV_A, (h + 1) * DV_A)
            st_h = s_scr[h]
            o_scr[rows, vcols] += lax.dot_general(qd_scr[rows, cols], st_h.astype(BF16), nt,
                                                  preferred_element_type=F32)
            upd_t = lax.dot_general(v_scr[rows, vcols], kl_scr[rows, cols], (((0,), (0,)), ((), ())),
                                    preferred_element_type=F32)
            s_scr[h] = dl[:, cols] * st_h + upd_t
        return carry

    lax.fori_loop(0, nblk, block, 0, unroll=HGRN_UNROLL if nblk % HGRN_UNROLL == 0 else 1)

    gn = gn_ref[...]
    for h in range(H_A):
        vcols = slice(h * DV_A, (h + 1) * DV_A)
        o = o_scr[:, vcols]
        o = o * lax.rsqrt(jnp.mean(o * o, axis=-1, keepdims=True) + EPS) * gn
        o_ref[:, vcols] = (o * sg_ref[:, vcols]).astype(BF16)

    @pl.when(j == pl.num_programs(1) - 1)
    def _():
        for h in range(H_A):
            sfin_ref[0, h] = s_scr[h].T


def _hgrn(qa, f, ia, sg, gn, s0, *, bsz, seq, row0, tl):
    d = qa.shape[1]
    nt = seq // tl
    blk0 = row0 // tl
    row = pl.BlockSpec((tl, d), lambda b, j: (blk0 + b * nt + j, 0))
    orow = pl.BlockSpec((tl, d), lambda b, j: (b * nt + j, 0))
    sspec = pl.BlockSpec((1, H_A, DK_A, DV_A), lambda b, j: (b, 0, 0, 0))
    return pl.pallas_call(
        _hgrn_kernel,
        grid=(bsz, nt),
        in_specs=[row, row, row, row, pl.BlockSpec((1, DV_A), lambda b, j: (0, 0)), sspec],
        out_specs=[orow, sspec],
        out_shape=[jax.ShapeDtypeStruct((bsz * seq, d), BF16),
                   jax.ShapeDtypeStruct((bsz, H_A, DK_A, DV_A), F32)],
        scratch_shapes=[pltpu.VMEM((H_A, DK_A, DV_A), F32)]
        + [pltpu.VMEM((tl, d), BF16)] * 5
        + [pltpu.VMEM((tl // HGRN_BLOCK, d), F32), pltpu.VMEM((tl, d), F32)],
        compiler_params=_cparams(("arbitrary", "arbitrary")),
        name="hgrn",
    )(qa, f, ia, sg, gn, s0)


SM_KB, SM_VB, SM_QI, SM_KI, SM_WI = 0, DH_B, 2 * DH_B, 2 * DH_B + H_I * D_I, 2 * DH_B + H_I * D_I + D_I
SM_USED = SM_WI + H_I
SM_WIDTH = -(-SM_USED // LANES) * LANES
INDEX_SCALE = (H_I * D_I) ** -0.5
NEG_INF = float("-inf")
LOG2_E = 1.4426950408889634


F32_EXP_MASK = 0x7F800000
F32_MIN_NORMAL = 0x00800000
SEARCH_UNROLL = 4


def _key_to_float(u):
    key = u ^ jnp.int32(INT_MIN)
    bits = jnp.where(key < 0, key ^ jnp.int32(0x7FFFFFFF), key)
    below_neg_inf = (bits < 0) & ((bits & jnp.int32(0x7FFFFFFF)) > jnp.int32(F32_EXP_MASK))
    return jnp.where(below_neg_inf, NEG_INF, lax.bitcast_convert_type(bits, F32))


def _positive_bits_to_float(bits):
    return jnp.where(bits < jnp.int32(F32_MIN_NORMAL), 0.0, lax.bitcast_convert_type(bits, F32))


def _count(mask):
    return jnp.sum(jnp.where(mask, 1.0, 0.0), axis=1, keepdims=True)


def _select_bias(score_scr, bias_scr, adm, s, n_sel):
    tq = score_scr.shape[0]

    def resolve_bit(i, t_u, cnt_t):
        cand_u = t_u | lax.shift_left(jnp.int32(1), 31 - i)
        cnt = _count(score_scr[:, :s] >= _key_to_float(cand_u))
        ok = cnt >= n_sel
        return jnp.where(ok, cand_u, t_u), jnp.where(ok, cnt, cnt_t)

    def unresolved(c):
        i, _, cnt_t = c
        return jnp.logical_and(i < 32, jnp.any(cnt_t != n_sel))

    def resolve_bits(c):
        i, t_u, cnt_t = c
        for b in range(SEARCH_UNROLL):
            t_u, cnt_t = resolve_bit(i + b, t_u, cnt_t)
        return i + SEARCH_UNROLL, t_u, cnt_t

    _, t_u, cnt_t = lax.while_loop(unresolved, resolve_bits,
                                   (jnp.int32(0), jnp.zeros((tq, 1), I32), jnp.full((tq, 1), float(s), F32)))
    t_f = _key_to_float(t_u)
    score = score_scr[:, :s]
    bias_scr[:, :s] = jnp.where((score >= t_f) & adm, 0.0, NEG_INF)

    @pl.when(jnp.any(cnt_t > n_sel))
    def _():
        above = score >= _key_to_float(t_u + 1)
        bucket = (score >= t_f) & jnp.logical_not(above)
        need = n_sel - _count(above)
        off = jnp.where(bucket, score - jnp.where(t_f == NEG_INF, 0.0, t_f), -1.0)

        def resolve_offset_bit(i, r_bits):
            cand = r_bits | lax.shift_left(jnp.int32(1), 30 - i)
            ok = _count(off >= _positive_bits_to_float(cand)) >= need
            return jnp.where(ok, cand, r_bits)

        ranked = jnp.any(off > 0.0)
        r_bits = lax.fori_loop(0, jnp.where(ranked, 31, 0), resolve_offset_bit,
                               jnp.full((tq, 1), jnp.where(ranked, 0, F32_MIN_NORMAL - 1), I32))
        above2 = off >= _positive_bits_to_float(r_bits + 1)
        tie = (off >= _positive_bits_to_float(r_bits)) & jnp.logical_not(above2)
        need2 = need - _count(above2)
        upper = (lax.broadcasted_iota(I32, (LANES, LANES), 0)
                 <= lax.broadcasted_iota(I32, (LANES, LANES), 1)).astype(BF16)
        carry = jnp.zeros((tq, 1), F32)
        for c in range(s // LANES):
            cs = slice(c * LANES, (c + 1) * LANES)
            rank = jnp.dot(jnp.where(tie[:, cs], 1.0, 0.0).astype(BF16), upper, preferred_element_type=F32) + carry
            sel = (above[:, cs] | above2[:, cs] | (tie[:, cs] & (rank <= need2))) & adm[:, cs]
            bias_scr[:, cs] = jnp.where(sel, 0.0, NEG_INF)
            carry = rank[:, LANES - 1:LANES]


def _dsa_tile(qb_ref, sm_ref, *refs, segs, search, n_sel):
    key_refs, (lim_ref, o_ref, score_scr, bias_scr) = refs[:3 * len(segs)], refs[3 * len(segs):]
    tq = qb_ref.shape[0]
    s = sum(segs)
    nt = (((1,), (1,)), ((), ()))
    adm = lax.broadcasted_iota(I32, (tq, s), 1) < lim_ref[...]
    seg_keys = [[key_refs[3 * g + j][0, :sg, :].astype(BF16) for j in range(3)] for g, sg in enumerate(segs)]
    starts = [sum(segs[:g]) for g in range(len(segs))]
    against_keys = lambda lhs, j: jnp.concatenate(
        [lax.dot_general(lhs, kv[j], nt, preferred_element_type=F32) for kv in seg_keys], axis=1)

    if search:
        score = jnp.zeros((tq, s), F32)
        for h in range(H_I):
            qih = sm_ref[:, SM_QI + h * D_I:SM_QI + (h + 1) * D_I].astype(BF16)
            wih = sm_ref[:, SM_WI + h:SM_WI + h + 1] * INDEX_SCALE
            score = score + jnp.maximum(against_keys(qih, 2), 0.0) * wih
        score_scr[:, :s] = jnp.where(adm, score, NEG_INF)
        _select_bias(score_scr, bias_scr, adm, s, n_sel)
    else:
        bias_scr[:, :s] = jnp.where(adm, 0.0, NEG_INF)

    for h in range(H_B):
        hs = slice(h * DH_B, (h + 1) * DH_B)
        logits = against_keys(qb_ref[:, hs], 0) + bias_scr[:, :s]
        p = jnp.exp2((logits - jnp.max(logits, axis=1, keepdims=True)) * (DH_B ** -0.5 * LOG2_E))
        denom = jnp.sum(p, axis=1, keepdims=True)
        pb = p.astype(BF16)
        o = sum(jnp.dot(pb[:, st:st + sg], kv[1], preferred_element_type=F32)
                for st, sg, kv in zip(starts, segs, seg_keys))
        o_ref[:, hs] = (o / denom).astype(BF16)


DSA_KEY_GRAN = 256


def _dsa(qb, sm, key_segments, limits, *, bsz, seq, row0, tq, n_sel):
    d = qb.shape[1]
    seg_rows = [seg[0].shape[1] for seg in key_segments]
    n_keys = sum(seg_rows)
    nt = seq // tq
    blk0 = row0 // tq
    limits = np.asarray(limits, np.int32)
    tile_max = limits.reshape(nt, tq).max(axis=1)
    per_tile = [(int(min(n_keys, -(-m // DSA_KEY_GRAN) * DSA_KEY_GRAN)), bool(m > n_sel)) for m in tile_max]
    lim2d = jnp.asarray(limits.reshape(seq, 1))
    outs, lo = [], 0
    while lo < nt:
        hi = lo
        while hi + 1 < nt and per_tile[hi + 1] == per_tile[lo]:
            hi += 1
        s, search = per_tile[lo]
        assert s == n_keys or len(key_segments) == 1
        segs = tuple(seg_rows) if s == n_keys else (s,)
        ntv = hi - lo + 1
        row = lambda w, lo=lo: pl.BlockSpec((tq, w), lambda b, j: (blk0 + b * nt + lo + j, 0))
        keys = lambda rows, w: pl.BlockSpec((1, rows, w), lambda b, j: (b, 0, 0))
        out = pl.pallas_call(
            functools.partial(_dsa_tile, segs=segs, search=search, n_sel=n_sel),
            grid=(bsz, ntv),
            in_specs=[row(d), row(sm.shape[1])]
            + [keys(rows, a.shape[2]) for rows, seg in zip(segs, key_segments) for a in seg]
            + [pl.BlockSpec((tq, 1), lambda b, j, lo=lo: (lo + j, 0))],
            out_specs=pl.BlockSpec((tq, d), lambda b, j, ntv=ntv: (b * ntv + j, 0)),
            out_shape=jax.ShapeDtypeStruct((bsz * ntv * tq, d), BF16),
            scratch_shapes=[pltpu.VMEM((tq, s), F32), pltpu.VMEM((tq, s), F32)],
            compiler_params=_cparams(("arbitrary", "arbitrary")),
            name="dsa",
        )(qb, sm, *[a for seg in key_segments for a in seg], lim2d)
        outs.append(out.reshape(bsz, ntv * tq, d))
        lo = hi + 1
    return jnp.concatenate(outs, axis=1).reshape(bsz * seq, d)


def _split_bf16(a):
    hi = a.astype(BF16)
    return hi, (a - hi.astype(F32)).astype(BF16)


def _mix_kernel(oap_ref, oas_ref, obp_ref, obs_ref, xp_ref, xs_ref, woa_ref, wob_ref, wga_ref, wgb_ref, wout_ref,
                g1_ref, b1_ref, wrh_ref, wrl_ref, br_ref, x1_ref, eidx_ref, gate_ref, cnt_ref, *, alpha, nt_p):
    i = pl.program_id(0)
    x = _group_tile(xp_ref, xs_ref, nt_p)
    xb = x.astype(BF16)
    dot = lambda a, w_ref: jnp.dot(a, w_ref[...], preferred_element_type=F32)
    y_a = dot(_group_tile(oap_ref, oas_ref, nt_p), woa_ref)
    y_b = dot(_group_tile(obp_ref, obs_ref, nt_p), wob_ref)
    merged = _sigmoid(dot(xb, wga_ref)) * y_a + _sigmoid(dot(xb, wgb_ref)) * y_b
    mixed = dot(merged.astype(BF16), wout_ref)
    x1 = _layer_norm(alpha * x + mixed, g1_ref[...], b1_ref[...])
    x1_ref[...] = x1

    nt = (((1,), (1,)), ((), ()))
    xh, xl = _split_bf16(x1)
    wh, wl = wrh_ref[...], wrl_ref[...]
    logits = (lax.dot_general(wh, xh, nt, preferred_element_type=F32)
              + lax.dot_general(wh, xl, nt, preferred_element_type=F32)
              + lax.dot_general(wl, xh, nt, preferred_element_type=F32)) + br_ref[...]
    ne, tm = logits.shape
    erow = lax.broadcasted_iota(I32, (ne, tm), 0)
    vals, idxs = [], []
    for _ in range(TOP_K):
        m = jnp.max(logits, axis=0, keepdims=True)
        idx = jnp.min(jnp.where(logits == m, erow, ne), axis=0, keepdims=True)
        vals.append(m)
        idxs.append(idx)
        logits = jnp.where(erow == idx, NEG_INF, logits)
    ex = [jnp.exp(v - vals[0]) for v in vals]
    denom = ex[0] + ex[1] + ex[2] + ex[3]
    pad = SUBLANES - TOP_K
    eidx = jnp.concatenate(idxs + [jnp.zeros((pad, tm), I32)], axis=0)
    eidx_ref[...] = eidx
    gate_ref[...] = jnp.concatenate([e / denom for e in ex] + [jnp.zeros((pad, tm), F32)], axis=0)

    onehot = jnp.zeros((ne, tm), F32)
    for idx in idxs:
        onehot = onehot + jnp.where(erow == idx, 1.0, 0.0)
    tile_cnt = jnp.broadcast_to(jnp.sum(onehot, axis=1, keepdims=True), cnt_ref.shape)

    @pl.when(i == 0)
    def _():
        cnt_ref[...] = tile_cnt

    @pl.when(i > 0)
    def _():
        cnt_ref[...] = cnt_ref[...] + tile_cnt


def _mix(oa, ob, x, woa, wob, wga, wgb, wout, g1, b1, wrh, wrl, br, *, tm, alpha):
    d = x[0].shape[1]
    n = x[0].shape[0] + x[1].shape[0]
    nt_p = x[0].shape[0] // tm
    ne = wrh.shape[0]
    row = lambda w: pl.BlockSpec((tm, w), lambda i: (i, 0))
    col = pl.BlockSpec((SUBLANES, tm), lambda i: (0, i))
    full = lambda a: pl.BlockSpec(a.shape, lambda i: (0,) * a.ndim)
    return pl.pallas_call(
        functools.partial(_mix_kernel, alpha=alpha, nt_p=nt_p),
        grid=(n // tm,),
        in_specs=[*_two_group_specs(tm, d, nt_p)] * 3
        + [full(a) for a in (woa, wob, wga, wgb, wout, g1, b1, wrh, wrl, br)],
        out_specs=[row(d), col, col, pl.BlockSpec((ne, LANES), lambda i: (0, 0))],
        out_shape=[jax.ShapeDtypeStruct((n, d), F32), jax.ShapeDtypeStruct((SUBLANES, n), I32),
                   jax.ShapeDtypeStruct((SUBLANES, n), F32), jax.ShapeDtypeStruct((ne, LANES), F32)],
        compiler_params=_cparams(("arbitrary",)),
        name="mix",
    )(*oa, *ob, *x, woa, wob, wga, wgb, wout, g1, b1, wrh, wrl, br)


def _sublane_cumsum(a):
    n = a.shape[0]
    row = lax.broadcasted_iota(I32, a.shape, 0)
    s = 1
    while s < n:
        a = a + jnp.where(row >= s, pltpu.roll(a, s, 0), 0.0)
        s *= 2
    return a


def _route_kernel(eidx_ref, cnt_ref, dest_ref, blke_ref, meta_ref, carry_scr, *, bm, sub):
    i = pl.program_id(0)
    ne = cnt_ref.shape[0]
    tl = eidx_ref.shape[1]

    @pl.when(i == 0)
    def _():
        carry_scr[...] = jnp.zeros_like(carry_scr)

    counts = cnt_ref[...]
    padded = jnp.ceil(counts / bm) * bm
    pend = _sublane_cumsum(padded)
    pstart = (pend - padded)[:, 0:1]

    @pl.when(i == 0)
    def _():
        nb = blke_ref.shape[1]
        first_row = (lax.broadcasted_iota(I32, (ne, nb), 1) * bm).astype(F32)
        below = jnp.sum(jnp.where(pend[:, 0:1] <= first_row, 1.0, 0.0), axis=0, keepdims=True)
        blke_ref[...] = jnp.minimum(below, ne - 1.0).astype(I32)
        pend_lanes = jnp.concatenate([pend, jnp.zeros((LANES - ne, LANES), F32)], axis=0).T[0:1, :]
        n_used = jnp.max(pend, axis=0, keepdims=True) / bm
        meta_ref[...] = jnp.concatenate([pend_lanes, n_used, jnp.zeros((SUBLANES - 2, LANES), F32)],
                                        axis=0).astype(I32)

    eidx = eidx_ref[...]
    erow = lax.broadcasted_iota(I32, (ne, tl), 0)
    hot = [jnp.where(erow == eidx[k:k + 1, :], 1.0, 0.0) for k in range(TOP_K)]
    onehot = hot[0] + hot[1] + hot[2] + hot[3]
    before = (lax.broadcasted_iota(I32, (tl, tl), 0) < lax.broadcasted_iota(I32, (tl, tl), 1)).astype(BF16)
    base = jnp.dot(onehot.astype(BF16), before, preferred_element_type=F32) + carry_scr[:, 0:1] + pstart
    dest = jnp.concatenate([jnp.sum(hk * base, axis=0, keepdims=True) for hk in hot], axis=0).astype(I32)
    for c in range(tl // sub):
        dest_ref[c] = dest[:, c * sub:(c + 1) * sub]
    carry_scr[...] = carry_scr[...] + jnp.sum(onehot, axis=1, keepdims=True)


def _route(eidx, counts, *, tl, sub, bm, nb):
    n = eidx.shape[1]
    ne = counts.shape[0]
    nb_pad = -(-nb // LANES) * LANES
    return pl.pallas_call(
        functools.partial(_route_kernel, bm=bm, sub=sub),
        grid=(n // tl,),
        in_specs=[pl.BlockSpec((SUBLANES, tl), lambda i: (0, i)), pl.BlockSpec(counts.shape, lambda i: (0, 0))],
        out_specs=[pl.BlockSpec((tl // sub, TOP_K, sub), lambda i: (i, 0, 0)),
                   pl.BlockSpec((1, nb_pad), lambda i: (0, 0)),
                   pl.BlockSpec((SUBLANES, LANES), lambda i: (0, 0))],
        out_shape=[jax.ShapeDtypeStruct((n // sub, TOP_K, sub), I32), jax.ShapeDtypeStruct((1, nb_pad), I32),
                   jax.ShapeDtypeStruct((SUBLANES, LANES), I32)],
        scratch_shapes=[pltpu.VMEM((ne, LANES), F32)],
        compiler_params=_cparams(("arbitrary",)),
        name="route",
    )(eidx, counts)


ROW_COPY_UNROLL = 4


def _dispatch_kernel(dest_ref, meta_ref, x_ref, xs_ref, zbuf, sem, zsem, *, bm, ne, nb):
    i = pl.program_id(0)
    tm = x_ref.shape[0]

    @pl.when(i == 0)
    def _():
        zbuf[...] = jnp.zeros_like(zbuf)
        zero_block = lambda row0: pltpu.make_async_copy(zbuf, xs_ref.at[pl.ds(row0, bm), :], zsem)

        def each_padding_block(action):
            for e in range(ne):
                seg_end = meta_ref[0, e]
                seg_start = meta_ref[0, e - 1] if e > 0 else 0

                @pl.when(seg_end > seg_start)
                def _(seg_end=seg_end):
                    action(zero_block(pl.multiple_of(seg_end - bm, bm)))

            def tail_block(b, carry):
                action(zero_block(pl.multiple_of(b * bm, bm)))
                return carry
            lax.fori_loop(meta_ref[1, 0], nb, tail_block, 0)

        each_padding_block(lambda copy: copy.start())
        each_padding_block(lambda copy: copy.wait())

    def start(t, carry):
        for k in range(TOP_K):
            pltpu.make_async_copy(x_ref.at[pl.ds(t, 1), :], xs_ref.at[pl.ds(dest_ref[0, k, t], 1), :], sem).start()
        return carry
    lax.fori_loop(0, tm, start, 0, unroll=ROW_COPY_UNROLL)

    for _ in range(TOP_K):
        pltpu.make_async_copy(x_ref, xs_ref.at[pl.ds(0, tm), :], sem).wait()


def _dispatch(dest, meta, x1, n_rows, *, tm, bm, ne):
    n, d = x1.shape
    return pl.pallas_call(
        functools.partial(_dispatch_kernel, bm=bm, ne=ne, nb=n_rows // bm),
        grid=(n // tm,),
        in_specs=[pl.BlockSpec((1, TOP_K, tm), lambda i: (i, 0, 0), memory_space=pltpu.SMEM),
                  pl.BlockSpec(meta.shape, lambda i: (0, 0), memory_space=pltpu.SMEM),
                  pl.BlockSpec((tm, d), lambda i: (i, 0))],
        out_specs=pl.BlockSpec(memory_space=pl.ANY),
        out_shape=jax.ShapeDtypeStruct((n_rows, d), x1.dtype),
        scratch_shapes=[pltpu.VMEM((bm, d), x1.dtype), pltpu.SemaphoreType.DMA(()), pltpu.SemaphoreType.DMA(())],
        compiler_params=_cparams(("arbitrary",)),
        name="dispatch",
    )(dest, meta, x1)


def _experts_kernel(blke_ref, nused_ref, xs_ref, wu_ref, bu_ref, wd_ref, bd_ref, ys_ref, wu16, wd16):
    i = pl.program_id(0)
    dff = wd_ref.shape[1]

    @pl.when(i < nused_ref[0])
    def _():
        @pl.when(jnp.logical_or(i == 0, blke_ref[i] != blke_ref[jnp.maximum(i - 1, 0)]))
        def _():
            wu16[...] = wu_ref[0].astype(BF16)
            wd16[...] = wd_ref[0].astype(BF16)

        h = jnp.dot(xs_ref[...].astype(BF16), wu16[...], preferred_element_type=F32) + bu_ref[0]
        glu = jnp.minimum(h[:, :dff], SWIGLU_LIMIT)
        lin = jnp.clip(h[:, dff:], -SWIGLU_LIMIT, SWIGLU_LIMIT)
        act = glu * _sigmoid(SWIGLU_ALPHA * glu) * (lin + 1.0)
        ys_ref[...] = jnp.dot(act.astype(BF16), wd16[...], preferred_element_type=F32) + bd_ref[0]

    @pl.when(i >= nused_ref[0])
    def _():
        ys_ref[...] = jnp.zeros_like(ys_ref)


def _experts(blk_e, n_used, xs, wu, bu, wd, bd, *, bm):
    n_rows, d = xs.shape
    ne, _, dff2 = wu.shape
    dff = wd.shape[1]
    used = lambda i, nu: jnp.minimum(i, nu[0] - 1)
    grid_spec = pltpu.PrefetchScalarGridSpec(
        num_scalar_prefetch=2,
        grid=(n_rows // bm,),
        in_specs=[pl.BlockSpec((bm, d), lambda i, e, nu: (used(i, nu), 0)),
                  pl.BlockSpec((1, d, dff2), lambda i, e, nu: (e[i], 0, 0)),
                  pl.BlockSpec((1, 1, dff2), lambda i, e, nu: (e[i], 0, 0)),
                  pl.BlockSpec((1, dff, d), lambda i, e, nu: (e[i], 0, 0)),
                  pl.BlockSpec((1, 1, d), lambda i, e, nu: (e[i], 0, 0))],
        out_specs=pl.BlockSpec((bm, d), lambda i, e, nu: (i, 0)),
        scratch_shapes=[pltpu.VMEM((d, dff2), BF16), pltpu.VMEM((dff, d), BF16)],
    )
    return pl.pallas_call(
        _experts_kernel,
        grid_spec=grid_spec,
        out_shape=jax.ShapeDtypeStruct((n_rows, d), F32),
        compiler_params=_cparams(("arbitrary",)),
        name="experts",
    )(blk_e, n_used, xs, wu, bu.reshape(ne, 1, dff2), wd, bd.reshape(ne, 1, d))


def _combine_kernel(dest_ref, ys_ref, gate_ref, x1_ref, g2_ref, b2_ref, outp_ref, outs_ref, buf, sem, *, alpha, nt_p):
    i = pl.program_id(0)
    tm = x1_ref.shape[0]

    def start(t, carry):
        for k in range(TOP_K):
            pltpu.make_async_copy(ys_ref.at[pl.ds(dest_ref[0, k, t], 1), :], buf.at[k, pl.ds(t, 1), :], sem).start()
        return carry
    lax.fori_loop(0, tm, start, 0, unroll=ROW_COPY_UNROLL)

    for k in range(TOP_K):
        pltpu.make_async_copy(ys_ref.at[pl.ds(0, tm), :], buf.at[k], sem).wait()

    gate = gate_ref[...].T
    y = buf[0] * gate[:, 0:1]
    for k in range(1, TOP_K):
        y = y + buf[k] * gate[:, k:k + 1]
    out = _layer_norm(alpha * x1_ref[...] + y, g2_ref[...], b2_ref[...])

    @pl.when(i < nt_p)
    def _():
        outp_ref[...] = out

    @pl.when(i >= nt_p)
    def _():
        outs_ref[...] = out


def _combine(dest, ys, gate, x1, g2, b2, *, n_p, tm, alpha):
    n, d = x1.shape
    nt_p = n_p // tm
    return pl.pallas_call(
        functools.partial(_combine_kernel, alpha=alpha, nt_p=nt_p),
        grid=(n // tm,),
        in_specs=[pl.BlockSpec((1, TOP_K, tm), lambda i: (i, 0, 0), memory_space=pltpu.SMEM),
                  pl.BlockSpec(memory_space=pl.ANY),
                  pl.BlockSpec((SUBLANES, tm), lambda i: (0, i)),
                  pl.BlockSpec((tm, d), lambda i: (i, 0)),
                  pl.BlockSpec((1, d), lambda i: (0, 0)),
                  pl.BlockSpec((1, d), lambda i: (0, 0))],
        out_specs=[pl.BlockSpec((tm, d), lambda i: (jnp.minimum(i, nt_p - 1), 0)),
                   pl.BlockSpec((tm, d), lambda i: (jnp.maximum(i - nt_p, 0), 0))],
        out_shape=[jax.ShapeDtypeStruct((n_p, d), F32), jax.ShapeDtypeStruct((n - n_p, d), F32)],
        scratch_shapes=[pltpu.VMEM((TOP_K, tm, d), F32), pltpu.SemaphoreType.DMA(())],
        compiler_params=_cparams(("arbitrary",)),
        name="combine",
    )(dest, ys, gate, x1, g2, b2)


PROJ_TM = 256
HGRN_TL = 256
DSA_TQ = 256
MIX_TM = 256
ROUTE_TL = 1280
MOE_TM = 256
MOE_BM = 256


def kernel(x_prompt, x_sample, cache_k, cache_v, cache_kidx, state_hgrn, w_in, lb_logits, gn_a, w_oa, w_ob,
           w_out, ln1_g, ln1_b, w_router, b_router, w_up, b_up, w_down, b_down, ln2_g, ln2_b):
    depth = w_in.shape[0]
    assert depth == 1
    bsz, seq, d = x_prompt.shape
    dbsz, dseq, _ = x_sample.shape
    past = cache_k.shape[2]
    n_p, n_s = bsz * seq, dbsz * dseq
    n = n_p + n_s
    alpha = (2 * depth) ** 0.25

    sizes = (H_A * DK_A, H_A * DK_A, H_A * DV_A, H_A * DV_A, H_B * DH_B, DH_B, DH_B, H_I * D_I, H_I, D_I, d, d)
    offs = np.concatenate([[0], np.cumsum(sizes)])
    w = w_in[0].astype(BF16)
    grp = lambda g: w[:, offs[g]:offs[g + 1]]
    wqa, wfa, wia, wg, wqb, wkb, wvb, wqi, wwi, wki, wgta, wgtb = (grp(g) for g in range(12))
    wsm = jnp.concatenate([wkb, wvb, wqi, wki, wwi, jnp.zeros((d, SM_WIDTH - SM_USED), BF16)], axis=1)

    x_groups = (x_prompt.reshape(n_p, d), x_sample.reshape(n_s, d))
    qa, f, ia, sg, qb, sm = _proj(*x_groups, wqa, wfa, wia, wg, wqb, wsm, lb_logits, PROJ_TM)

    gn = gn_a[0].reshape(1, DV_A)
    oa_p, sfin_p = _hgrn(qa, f, ia, sg, gn, jnp.zeros((bsz, H_A, DK_A, DV_A), F32),
                         bsz=bsz, seq=seq, row0=0, tl=HGRN_TL)
    oa_s, sfin_s = _hgrn(qa, f, ia, sg, gn, state_hgrn[0], bsz=dbsz, seq=dseq, row0=n_p, tl=dseq)

    kb_p = sm[:n_p, SM_KB:SM_KB + DH_B].reshape(bsz, seq, DH_B)
    vb_p = sm[:n_p, SM_VB:SM_VB + DH_B].reshape(bsz, seq, DH_B)
    ki_p = sm[:n_p, SM_KI:SM_KI + D_I].reshape(bsz, seq, D_I)
    kb_s = sm[n_p:, SM_KB:SM_KB + DH_B].reshape(dbsz, dseq, DH_B)
    vb_s = sm[n_p:, SM_VB:SM_VB + DH_B].reshape(dbsz, dseq, DH_B)
    ki_s = sm[n_p:, SM_KI:SM_KI + D_I].reshape(dbsz, dseq, D_I)
    lim_p = (np.arange(seq) // CHUNK + 1) * CHUNK
    ob_p = _dsa(qb, sm, [(kb_p.astype(BF16), vb_p.astype(BF16), ki_p.astype(BF16))], lim_p,
                bsz=bsz, seq=seq, row0=0, tq=DSA_TQ, n_sel=min(TOPK_MAX, seq // 4))
    n_keys = past + dseq
    new_pad = -(-dseq // LANES) * LANES - dseq
    new_keys = tuple(jnp.pad(a.astype(BF16), ((0, 0), (0, new_pad), (0, 0))) for a in (kb_s, vb_s, ki_s))
    lim_s = np.full((dseq,), n_keys)
    ob_s = _dsa(qb, sm, [(cache_k[0], cache_v[0], cache_kidx[0]), new_keys], lim_s,
                bsz=dbsz, seq=dseq, row0=n_p, tq=dseq, n_sel=min(TOPK_MAX, n_keys // 4))

    wrh, wrl = _split_bf16(w_router[0].T)
    x1, eidx, gate, counts = _mix(
        (oa_p, oa_s), (ob_p, ob_s), x_groups, w_oa[0].astype(BF16), w_ob[0].astype(BF16), wgta, wgtb, w_out[0].astype(BF16),
        ln1_g[0].reshape(1, d), ln1_b[0].reshape(1, d), wrh, wrl, b_router[0].reshape(N_EXPERTS, 1),
        tm=MIX_TM, alpha=alpha)

    n_rows = -(-(n * TOP_K + N_EXPERTS * (MOE_BM - 1)) // MOE_BM) * MOE_BM
    dest, blk_e, meta = _route(eidx, counts, tl=ROUTE_TL, sub=MOE_TM, bm=MOE_BM, nb=n_rows // MOE_BM)
    xs = _dispatch(dest, meta, x1, n_rows, tm=MOE_TM, bm=MOE_BM, ne=N_EXPERTS)
    ys = _experts(blk_e[0, :n_rows // MOE_BM], meta[1, :1], xs, w_up[0], b_up[0], w_down[0], b_down[0], bm=MOE_BM)
    out_p, out_s = _combine(dest, ys, gate, x1, ln2_g[0].reshape(1, d), ln2_b[0].reshape(1, d),
                            n_p=n_p, tm=MOE_TM, alpha=alpha)

    return (out_p.reshape(bsz, seq, d), out_s.reshape(dbsz, dseq, d),
            kb_p[None], vb_p[None], ki_p[None], sfin_p[None],
            kb_s[None], vb_s[None], ki_s[None], sfin_s[None])
```

```python
import functools

import jax
import jax.numpy as jnp
import numpy as np
from jax import lax
from jax.experimental import pallas as pl
from jax.experimental.pallas import tpu as pltpu

F32 = jnp.float32
BF16 = jnp.bfloat16
I32 = jnp.int32

CHUNK = 64
H_A = 8
DK_A = 128
DV_A = 128
HGRN_BLOCK = 16
H_B = 8
DH_B = 128
H_I = 8
D_I = 64
TOPK_MAX = 256
N_EXPERTS = 32
TOP_K = 4
SWIGLU_ALPHA = 1.702
SWIGLU_LIMIT = 7.0
EPS = 1e-5

LANES = 128
SUBLANES = 8
VMEM_LIMIT_BYTES = 56 * 1024 * 1024

INT_MIN = -(2 ** 31)


def _cparams(sem):
    return pltpu.CompilerParams(dimension_semantics=sem, vmem_limit_bytes=VMEM_LIMIT_BYTES)


def _sigmoid(x):
    return 1.0 / (1.0 + jnp.exp(-x))


def _full_spec(shape):
    nd = len(shape)
    return pl.BlockSpec(shape, lambda *_: (0,) * nd)


def _layer_norm(x, g, b):
    mu = jnp.mean(x, axis=-1, keepdims=True)
    xc = x - mu
    var = jnp.mean(xc * xc, axis=-1, keepdims=True)
    return xc * lax.rsqrt(var + EPS) * g + b


def _two_group_specs(tm, width, nt_p):
    return (pl.BlockSpec((tm, width), lambda i: (jnp.minimum(i, nt_p - 1), 0)),
            pl.BlockSpec((tm, width), lambda i: (jnp.maximum(i - nt_p, 0), 0)))


def _group_tile(p_ref, s_ref, nt_p):
    return jnp.where(pl.program_id(0) < nt_p, p_ref[...], s_ref[...])


def _proj_kernel(xp_ref, xs_ref, wqa_ref, wfa_ref, wia_ref, wga_ref, wqb_ref, wsm_ref, lbl_ref,
                 qa_ref, f_ref, ia_ref, sg_ref, qb_ref, sm_ref, *, nt_p):
    xb = _group_tile(xp_ref, xs_ref, nt_p).astype(BF16)
    dot = lambda w_ref: jnp.dot(xb, w_ref[...], preferred_element_type=F32)
    qa_ref[...] = dot(wqa_ref)
    lbl = lbl_ref[...]
    lbe = jnp.exp(lbl - jnp.max(lbl, axis=0, keepdims=True))
    lb = lbe[0:1, :] / jnp.sum(lbe, axis=0, keepdims=True)
    f_ref[...] = lb + (1.0 - lb) * _sigmoid(dot(wfa_ref))
    ia_ref[...] = dot(wia_ref)
    g = dot(wga_ref)
    sg_ref[...] = g * _sigmoid(g)
    qb_ref[...] = dot(wqb_ref).astype(BF16)
    sm_ref[...] = dot(wsm_ref)


def _proj(x_p, x_s, wqa, wfa, wia, wga, wqb, wsm, lb_logits, tm):
    d = x_p.shape[1]
    n = x_p.shape[0] + x_s.shape[0]
    nt_p = x_p.shape[0] // tm
    nsm = wsm.shape[1]
    row = lambda w: pl.BlockSpec((tm, w), lambda i: (i, 0))
    wspec = lambda w: pl.BlockSpec(w.shape, lambda i: (0, 0))
    return pl.pallas_call(
        functools.partial(_proj_kernel, nt_p=nt_p),
        grid=(n // tm,),
        in_specs=[*_two_group_specs(tm, d, nt_p), wspec(wqa), wspec(wfa), wspec(wia), wspec(wga), wspec(wqb),
                  wspec(wsm), wspec(lb_logits)],
        out_specs=[row(d), row(d), row(d), row(d), row(d), row(nsm)],
        out_shape=[jax.ShapeDtypeStruct((n, d), F32)] * 4
        + [jax.ShapeDtypeStruct((n, d), BF16), jax.ShapeDtypeStruct((n, nsm), F32)],
        compiler_params=_cparams(("arbitrary",)),
        name="proj",
    )(x_p, x_s, wqa, wfa, wia, wga, wqb, wsm, lb_logits)


HGRN_UNROLL = 8


def _block_cumsum(a, row_in_block):
    s = 1
    while s < HGRN_BLOCK:
        a = a + jnp.where(row_in_block >= s, pltpu.roll(a, s, 0), 0.0)
        s *= 2
    return a


def _hgrn_kernel(qa_ref, f_ref, ia_ref, sg_ref, gn_ref, s0_ref, o_ref, sfin_ref,
                 s_scr, qd_scr, qm_scr, km_scr, kl_scr, v_scr, dl_scr, o_scr):
    j = pl.program_id(1)
    tl = qa_ref.shape[0]
    nblk = tl // HGRN_BLOCK
    half = HGRN_BLOCK // 2

    @pl.when(j == 0)
    def _():
        for h in range(H_A):
            s_scr[h] = s0_ref[0, h].T

    f = f_ref[...]
    q = qa_ref[...]
    d = f.shape[1]
    row_in_block = lax.broadcasted_iota(I32, (tl, d), 0) % HGRN_BLOCK
    cum = _block_cumsum(jnp.log(f), row_in_block)
    cum3 = cum.reshape(nblk, HGRN_BLOCK, d)
    mid = jnp.broadcast_to(cum3[:, half:half + 1, :], cum3.shape).reshape(tl, d)
    last3 = cum3[:, HGRN_BLOCK - 1:HGRN_BLOCK, :]
    last = jnp.broadcast_to(last3, cum3.shape).reshape(tl, d)
    k = 1.0 - f
    qd_scr[...] = (q * jnp.exp(cum)).astype(BF16)
    qm_scr[...] = (q * jnp.exp(cum - mid)).astype(BF16)
    km_scr[...] = (k * jnp.exp(mid - cum)).astype(BF16)
    kl_scr[...] = (k * jnp.exp(last - cum)).astype(BF16)
    dl_scr[...] = jnp.exp(last3.reshape(nblk, d))
    v_scr[...] = ia_ref[...].astype(BF16)

    nt = (((1,), (1,)), ((), ()))
    ri = lax.broadcasted_iota(I32, (tl, tl), 0)
    ci = lax.broadcasted_iota(I32, (tl, tl), 1)
    keep = (ri // HGRN_BLOCK == ci // HGRN_BLOCK) & (ri >= ci)
    for h in range(H_A):
        cols = slice(h * DK_A, (h + 1) * DK_A)
        vcols = slice(h * DV_A, (h + 1) * DV_A)
        att = lax.dot_general(qm_scr[:, cols], km_scr[:, cols], nt, preferred_element_type=F32)
        att = jnp.where(keep, att, 0.0).astype(BF16)
        o_scr[:, vcols] = jnp.dot(att, v_scr[:, vcols], preferred_element_type=F32)

    def block(b, carry):
        r0 = pl.multiple_of(b * HGRN_BLOCK, HGRN_BLOCK)
        rows = pl.ds(r0, HGRN_BLOCK)
        dl = dl_scr[pl.ds(b, 1), :]
        for h in range(H_A):
            cols = slice(h * DK_A, (h + 1) * DK_A)
            vcols = slice(h * DV_A, (h + 1) * DV_A)
            st_h = s_scr[h]
            o_scr[rows, vcols] += lax.dot_general(qd_scr[rows, cols], st_h.astype(BF16), nt,
                                                  preferred_element_type=F32)
            upd_t = lax.dot_general(v_scr[rows, vcols], kl_scr[rows, cols], (((0,), (0,)), ((), ())),
                                    preferred_element_type=F32)
            s_scr[h] = dl[:, cols] * st_h + upd_t
        return carry

    lax.fori_loop(0, nblk, block, 0, unroll=HGRN_UNROLL if nblk % HGRN_UNROLL == 0 else 1)

    gn = gn_ref[...]
    for h in range(H_A):
        vcols = slice(h * DV_A, (h + 1) * DV_A)
        o = o_scr[:, vcols]
        o = o * lax.rsqrt(jnp.mean(o * o, axis=-1, keepdims=True) + EPS) * gn
        o_ref[:, vcols] = (o * sg_ref[:, vcols]).astype(BF16)

    @pl.when(j == pl.num_programs(1) - 1)
    def _():
        for h in range(H_A):
            sfin_ref[0, h] = s_scr[h].T


def _hgrn(qa, f, ia, sg, gn, s0, *, bsz, seq, row0, tl):
    d = qa.shape[1]
    nt = seq // tl
    blk0 = row0 // tl
    row = pl.BlockSpec((tl, d), lambda b, j: (blk0 + b * nt + j, 0))
    orow = pl.BlockSpec((tl, d), lambda b, j: (b * nt + j, 0))
    sspec = pl.BlockSpec((1, H_A, DK_A, DV_A), lambda b, j: (b, 0, 0, 0))
    return pl.pallas_call(
        _hgrn_kernel,
        grid=(bsz, nt),
        in_specs=[row, row, row, row, pl.BlockSpec((1, DV_A), lambda b, j: (0, 0)), sspec],
        out_specs=[orow, sspec],
        out_shape=[jax.ShapeDtypeStruct((bsz * seq, d), BF16),
                   jax.ShapeDtypeStruct((bsz, H_A, DK_A, DV_A), F32)],
        scratch_shapes=[pltpu.VMEM((H_A, DK_A, DV_A), F32)]
        + [pltpu.VMEM((tl, d), BF16)] * 5
        + [pltpu.VMEM((tl // HGRN_BLOCK, d), F32), pltpu.VMEM((tl, d), F32)],
        compiler_params=_cparams(("arbitrary", "arbitrary")),
        name="hgrn",
    )(qa, f, ia, sg, gn, s0)


SM_KB, SM_VB, SM_QI, SM_KI, SM_WI = 0, DH_B, 2 * DH_B, 2 * DH_B + H_I * D_I, 2 * DH_B + H_I * D_I + D_I
SM_USED = SM_WI + H_I
SM_WIDTH = -(-SM_USED // LANES) * LANES
INDEX_SCALE = (H_I * D_I) ** -0.5
NEG_INF = float("-inf")
LOG2_E = 1.4426950408889634


F32_EXP_MASK = 0x7F800000
F32_MIN_NORMAL = 0x00800000
SEARCH_UNROLL = 4


def _key_to_float(u):
    key = u ^ jnp.int32(INT_MIN)
    bits = jnp.where(key < 0, key ^ jnp.int32(0x7FFFFFFF), key)
    below_neg_inf = (bits < 0) & ((bits & jnp.int32(0x7FFFFFFF)) > jnp.int32(F32_EXP_MASK))
    return jnp.where(below_neg_inf, NEG_INF, lax.bitcast_convert_type(bits, F32))


def _positive_bits_to_float(bits):
    return jnp.where(bits < jnp.int32(F32_MIN_NORMAL), 0.0, lax.bitcast_convert_type(bits, F32))


def _count(mask):
    return jnp.sum(jnp.where(mask, 1.0, 0.0), axis=1, keepdims=True)


def _select_bias(score_scr, bias_scr, adm, s, n_sel):
    tq = score_scr.shape[0]

    def resolve_bit(i, t_u, cnt_t):
        cand_u = t_u | lax.shift_left(jnp.int32(1), 31 - i)
        cnt = _count(score_scr[:, :s] >= _key_to_float(cand_u))
        ok = cnt >= n_sel
        return jnp.where(ok, cand_u, t_u), jnp.where(ok, cnt, cnt_t)

    def unresolved(c):
        i, _, cnt_t = c
        return jnp.logical_and(i < 32, jnp.any(cnt_t != n_sel))

    def resolve_bits(c):
        i, t_u, cnt_t = c
        for b in range(SEARCH_UNROLL):
            t_u, cnt_t = resolve_bit(i + b, t_u, cnt_t)
        return i + SEARCH_UNROLL, t_u, cnt_t

    _, t_u, cnt_t = lax.while_loop(unresolved, resolve_bits,
                                   (jnp.int32(0), jnp.zeros((tq, 1), I32), jnp.full((tq, 1), float(s), F32)))
    t_f = _key_to_float(t_u)
    score = score_scr[:, :s]
    bias_scr[:, :s] = jnp.where((score >= t_f) & adm, 0.0, NEG_INF)

    @pl.when(jnp.any(cnt_t > n_sel))
    def _():
        above = score >= _key_to_float(t_u + 1)
        bucket = (score >= t_f) & jnp.logical_not(above)
        need = n_sel - _count(above)
        off = jnp.where(bucket, score - jnp.where(t_f == NEG_INF, 0.0, t_f), -1.0)

        def resolve_offset_bit(i, r_bits):
            cand = r_bits | lax.shift_left(jnp.int32(1), 30 - i)
            ok = _count(off >= _positive_bits_to_float(cand)) >= need
            return jnp.where(ok, cand, r_bits)

        ranked = jnp.any(off > 0.0)
        r_bits = lax.fori_loop(0, jnp.where(ranked, 31, 0), resolve_offset_bit,
                               jnp.full((tq, 1), jnp.where(ranked, 0, F32_MIN_NORMAL - 1), I32))
        above2 = off >= _positive_bits_to_float(r_bits + 1)
        tie = (off >= _positive_bits_to_float(r_bits)) & jnp.logical_not(above2)
        need2 = need - _count(above2)
        upper = (lax.broadcasted_iota(I32, (LANES, LANES), 0)
                 <= lax.broadcasted_iota(I32, (LANES, LANES), 1)).astype(BF16)
        carry = jnp.zeros((tq, 1), F32)
        for c in range(s // LANES):
            cs = slice(c * LANES, (c + 1) * LANES)
            rank = jnp.dot(jnp.where(tie[:, cs], 1.0, 0.0).astype(BF16), upper, preferred_element_type=F32) + carry
            sel = (above[:, cs] | above2[:, cs] | (tie[:, cs] & (rank <= need2))) & adm[:, cs]
            bias_scr[:, cs] = jnp.where(sel, 0.0, NEG_INF)
            carry = rank[:, LANES - 1:LANES]


def _dsa_tile(qb_ref, sm_ref, *refs, segs, search, n_sel):
    key_refs, (lim_ref, o_ref, score_scr, bias_scr) = refs[:3 * len(segs)], refs[3 * len(segs):]
    tq = qb_ref.shape[0]
    s = sum(segs)
    nt = (((1,), (1,)), ((), ()))
    adm = lax.broadcasted_iota(I32, (tq, s), 1) < lim_ref[...]
    seg_keys = [[key_refs[3 * g + j][0, :sg, :].astype(BF16) for j in range(3)] for g, sg in enumerate(segs)]
    starts = [sum(segs[:g]) for g in range(len(segs))]
    against_keys = lambda lhs, j: jnp.concatenate(
        [lax.dot_general(lhs, kv[j], nt, preferred_element_type=F32) for kv in seg_keys], axis=1)

    if search:
        score = jnp.zeros((tq, s), F32)
        for h in range(H_I):
            qih = sm_ref[:, SM_QI + h * D_I:SM_QI + (h + 1) * D_I].astype(BF16)
            wih = sm_ref[:, SM_WI + h:SM_WI + h + 1] * INDEX_SCALE
            score = score + jnp.maximum(against_keys(qih, 2), 0.0) * wih
        score_scr[:, :s] = jnp.where(adm, score, NEG_INF)
        _select_bias(score_scr, bias_scr, adm, s, n_sel)
    else:
        bias_scr[:, :s] = jnp.where(adm, 0.0, NEG_INF)

    for h in range(H_B):
        hs = slice(h * DH_B, (h + 1) * DH_B)
        logits = against_keys(qb_ref[:, hs], 0) + bias_scr[:, :s]
        p = jnp.exp2((logits - jnp.max(logits, axis=1, keepdims=True)) * (DH_B ** -0.5 * LOG2_E))
        denom = jnp.sum(p, axis=1, keepdims=True)
        pb = p.astype(BF16)
        o = sum(jnp.dot(pb[:, st:st + sg], kv[1], preferred_element_type=F32)
                for st, sg, kv in zip(starts, segs, seg_keys))
        o_ref[:, hs] = (o / denom).astype(BF16)


DSA_KEY_GRAN = 256


def _dsa(qb, sm, key_segments, limits, *, bsz, seq, row0, tq, n_sel):
    d = qb.shape[1]
    seg_rows = [seg[0].shape[1] for seg in key_segments]
    n_keys = sum(seg_rows)
    nt = seq // tq
    blk0 = row0 // tq
    limits = np.asarray(limits, np.int32)
    tile_max = limits.reshape(nt, tq).max(axis=1)
    per_tile = [(int(min(n_keys, -(-m // DSA_KEY_GRAN) * DSA_KEY_GRAN)), bool(m > n_sel)) for m in tile_max]
    lim2d = jnp.asarray(limits.reshape(seq, 1))
    outs, lo = [], 0
    while lo < nt:
        hi = lo
        while hi + 1 < nt and per_tile[hi + 1] == per_tile[lo]:
            hi += 1
        s, search = per_tile[lo]
        assert s == n_keys or len(key_segments) == 1
        segs = tuple(seg_rows) if s == n_keys else (s,)
        ntv = hi - lo + 1
        row = lambda w, lo=lo: pl.BlockSpec((tq, w), lambda b, j: (blk0 + b * nt + lo + j, 0))
        keys = lambda rows, w: pl.BlockSpec((1, rows, w), lambda b, j: (b, 0, 0))
        out = pl.pallas_call(
            functools.partial(_dsa_tile, segs=segs, search=search, n_sel=n_sel),
            grid=(bsz, ntv),
            in_specs=[row(d), row(sm.shape[1])]
            + [keys(rows, a.shape[2]) for rows, seg in zip(segs, key_segments) for a in seg]
            + [pl.BlockSpec((tq, 1), lambda b, j, lo=lo: (lo + j, 0))],
            out_specs=pl.BlockSpec((tq, d), lambda b, j, ntv=ntv: (b * ntv + j, 0)),
            out_shape=jax.ShapeDtypeStruct((bsz * ntv * tq, d), BF16),
            scratch_shapes=[pltpu.VMEM((tq, s), F32), pltpu.VMEM((tq, s), F32)],
            compiler_params=_cparams(("arbitrary", "arbitrary")),
            name="dsa",
        )(qb, sm, *[a for seg in key_segments for a in seg], lim2d)
        outs.append(out.reshape(bsz, ntv * tq, d))
        lo = hi + 1
    return jnp.concatenate(outs, axis=1).reshape(bsz * seq, d)


def _split_bf16(a):
    hi = a.astype(BF16)
    return hi, (a - hi.astype(F32)).astype(BF16)


def _mix_kernel(oap_ref, oas_ref, obp_ref, obs_ref, xp_ref, xs_ref, woa_ref, wob_ref, wga_ref, wgb_ref, wout_ref,
                g1_ref, b1_ref, wrh_ref, wrl_ref, br_ref, x1_ref, eidx_ref, gate_ref, cnt_ref, *, alpha, nt_p):
    i = pl.program_id(0)
    x = _group_tile(xp_ref, xs_ref, nt_p)
    xb = x.astype(BF16)
    dot = lambda a, w_ref: jnp.dot(a, w_ref[...], preferred_element_type=F32)
    y_a = dot(_group_tile(oap_ref, oas_ref, nt_p), woa_ref)
    y_b = dot(_group_tile(obp_ref, obs_ref, nt_p), wob_ref)
    merged = _sigmoid(dot(xb, wga_ref)) * y_a + _sigmoid(dot(xb, wgb_ref)) * y_b
    mixed = dot(merged.astype(BF16), wout_ref)
    x1 = _layer_norm(alpha * x + mixed, g1_ref[...], b1_ref[...])
    x1_ref[...] = x1

    nt = (((1,), (1,)), ((), ()))
    xh, xl = _split_bf16(x1)
    wh, wl = wrh_ref[...], wrl_ref[...]
    logits = (lax.dot_general(wh, xh, nt, preferred_element_type=F32)
              + lax.dot_general(wh, xl, nt, preferred_element_type=F32)
              + lax.dot_general(wl, xh, nt, preferred_element_type=F32)) + br_ref[...]
    ne, tm = logits.shape
    erow = lax.broadcasted_iota(I32, (ne, tm), 0)
    vals, idxs = [], []
    for _ in range(TOP_K):
        m = jnp.max(logits, axis=0, keepdims=True)
        idx = jnp.min(jnp.where(logits == m, erow, ne), axis=0, keepdims=True)
        vals.append(m)
        idxs.append(idx)
        logits = jnp.where(erow == idx, NEG_INF, logits)
    ex = [jnp.exp(v - vals[0]) for v in vals]
    denom = ex[0] + ex[1] + ex[2] + ex[3]
    pad = SUBLANES - TOP_K
    eidx = jnp.concatenate(idxs + [jnp.zeros((pad, tm), I32)], axis=0)
    eidx_ref[...] = eidx
    gate_ref[...] = jnp.concatenate([e / denom for e in ex] + [jnp.zeros((pad, tm), F32)], axis=0)

    onehot = jnp.zeros((ne, tm), F32)
    for idx in idxs:
        onehot = onehot + jnp.where(erow == idx, 1.0, 0.0)
    tile_cnt = jnp.broadcast_to(jnp.sum(onehot, axis=1, keepdims=True), cnt_ref.shape)

    @pl.when(i == 0)
    def _():
        cnt_ref[...] = tile_cnt

    @pl.when(i > 0)
    def _():
        cnt_ref[...] = cnt_ref[...] + tile_cnt


def _mix(oa, ob, x, woa, wob, wga, wgb, wout, g1, b1, wrh, wrl, br, *, tm, alpha):
    d = x[0].shape[1]
    n = x[0].shape[0] + x[1].shape[0]
    nt_p = x[0].shape[0] // tm
    ne = wrh.shape[0]
    row = lambda w: pl.BlockSpec((tm, w), lambda i: (i, 0))
    col = pl.BlockSpec((SUBLANES, tm), lambda i: (0, i))
    full = lambda a: pl.BlockSpec(a.shape, lambda i: (0,) * a.ndim)
    return pl.pallas_call(
        functools.partial(_mix_kernel, alpha=alpha, nt_p=nt_p),
        grid=(n // tm,),
        in_specs=[*_two_group_specs(tm, d, nt_p)] * 3
        + [full(a) for a in (woa, wob, wga, wgb, wout, g1, b1, wrh, wrl, br)],
        out_specs=[row(d), col, col, pl.BlockSpec((ne, LANES), lambda i: (0, 0))],
        out_shape=[jax.ShapeDtypeStruct((n, d), F32), jax.ShapeDtypeStruct((SUBLANES, n), I32),
                   jax.ShapeDtypeStruct((SUBLANES, n), F32), jax.ShapeDtypeStruct((ne, LANES), F32)],
        compiler_params=_cparams(("arbitrary",)),
        name="mix",
    )(*oa, *ob, *x, woa, wob, wga, wgb, wout, g1, b1, wrh, wrl, br)


def _sublane_cumsum(a):
    n = a.shape[0]
    row = lax.broadcasted_iota(I32, a.shape, 0)
    s = 1
    while s < n:
        a = a + jnp.where(row >= s, pltpu.roll(a, s, 0), 0.0)
        s *= 2
    return a


def _route_kernel(eidx_ref, cnt_ref, dest_ref, blke_ref, meta_ref, carry_scr, *, bm, sub):
    i = pl.program_id(0)
    ne = cnt_ref.shape[0]
    tl = eidx_ref.shape[1]

    @pl.when(i == 0)
    def _():
        carry_scr[...] = jnp.zeros_like(carry_scr)

    counts = cnt_ref[...]
    padded = jnp.ceil(counts / bm) * bm
    pend = _sublane_cumsum(padded)
    pstart = (pend - padded)[:, 0:1]

    @pl.when(i == 0)
    def _():
        nb = blke_ref.shape[1]
        first_row = (lax.broadcasted_iota(I32, (ne, nb), 1) * bm).astype(F32)
        below = jnp.sum(jnp.where(pend[:, 0:1] <= first_row, 1.0, 0.0), axis=0, keepdims=True)
        blke_ref[...] = jnp.minimum(below, ne - 1.0).astype(I32)
        pend_lanes = jnp.concatenate([pend, jnp.zeros((LANES - ne, LANES), F32)], axis=0).T[0:1, :]
        n_used = jnp.max(pend, axis=0, keepdims=True) / bm
        meta_ref[...] = jnp.concatenate([pend_lanes, n_used, jnp.zeros((SUBLANES - 2, LANES), F32)],
                                        axis=0).astype(I32)

    eidx = eidx_ref[...]
    erow = lax.broadcasted_iota(I32, (ne, tl), 0)
    hot = [jnp.where(erow == eidx[k:k + 1, :], 1.0, 0.0) for k in range(TOP_K)]
    onehot = hot[0] + hot[1] + hot[2] + hot[3]
    before = (lax.broadcasted_iota(I32, (tl, tl), 0) < lax.broadcasted_iota(I32, (tl, tl), 1)).astype(BF16)
    base = jnp.dot(onehot.astype(BF16), before, preferred_element_type=F32) + carry_scr[:, 0:1] + pstart
    dest = jnp.concatenate([jnp.sum(hk * base, axis=0, keepdims=True) for hk in hot], axis=0).astype(I32)
    for c in range(tl // sub):
        dest_ref[c] = dest[:, c * sub:(c + 1) * sub]
    carry_scr[...] = carry_scr[...] + jnp.sum(onehot, axis=1, keepdims=True)


def _route(eidx, counts, *, tl, sub, bm, nb):
    n = eidx.shape[1]
    ne = counts.shape[0]
    nb_pad = -(-nb // LANES) * LANES
    return pl.pallas_call(
        functools.partial(_route_kernel, bm=bm, sub=sub),
        grid=(n // tl,),
        in_specs=[pl.BlockSpec((SUBLANES, tl), lambda i: (0, i)), pl.BlockSpec(counts.shape, lambda i: (0, 0))],
        out_specs=[pl.BlockSpec((tl // sub, TOP_K, sub), lambda i: (i, 0, 0)),
                   pl.BlockSpec((1, nb_pad), lambda i: (0, 0)),
                   pl.BlockSpec((SUBLANES, LANES), lambda i: (0, 0))],
        out_shape=[jax.ShapeDtypeStruct((n // sub, TOP_K, sub), I32), jax.ShapeDtypeStruct((1, nb_pad), I32),
                   jax.ShapeDtypeStruct((SUBLANES, LANES), I32)],
        scratch_shapes=[pltpu.VMEM((ne, LANES), F32)],
        compiler_params=_cparams(("arbitrary",)),
        name="route",
    )(eidx, counts)


ROW_COPY_UNROLL = 4


def _dispatch_kernel(dest_ref, meta_ref, x_ref, xs_ref, zbuf, sem, zsem, *, bm, ne, nb):
    i = pl.program_id(0)
    tm = x_ref.shape[0]

    @pl.when(i == 0)
    def _():
        zbuf[...] = jnp.zeros_like(zbuf)
        zero_block = lambda row0: pltpu.make_async_copy(zbuf, xs_ref.at[pl.ds(row0, bm), :], zsem)

        def each_padding_block(action):
            for e in range(ne):
                seg_end = meta_ref[0, e]
                seg_start = meta_ref[0, e - 1] if e > 0 else 0

                @pl.when(seg_end > seg_start)
                def _(seg_end=seg_end):
                    action(zero_block(pl.multiple_of(seg_end - bm, bm)))

            def tail_block(b, carry):
                action(zero_block(pl.multiple_of(b * bm, bm)))
                return carry
            lax.fori_loop(meta_ref[1, 0], nb, tail_block, 0)

        each_padding_block(lambda copy: copy.start())
        each_padding_block(lambda copy: copy.wait())

    def start(t, carry):
        for k in range(TOP_K):
            pltpu.make_async_copy(x_ref.at[pl.ds(t, 1), :], xs_ref.at[pl.ds(dest_ref[0, k, t], 1), :], sem).start()
        return carry
    lax.fori_loop(0, tm, start, 0, unroll=ROW_COPY_UNROLL)

    for _ in range(TOP_K):
        pltpu.make_async_copy(x_ref, xs_ref.at[pl.ds(0, tm), :], sem).wait()


def _dispatch(dest, meta, x1, n_rows, *, tm, bm, ne):
    n, d = x1.shape
    return pl.pallas_call(
        functools.partial(_dispatch_kernel, bm=bm, ne=ne, nb=n_rows // bm),
        grid=(n // tm,),
        in_specs=[pl.BlockSpec((1, TOP_K, tm), lambda i: (i, 0, 0), memory_space=pltpu.SMEM),
                  pl.BlockSpec(meta.shape, lambda i: (0, 0), memory_space=pltpu.SMEM),
                  pl.BlockSpec((tm, d), lambda i: (i, 0))],
        out_specs=pl.BlockSpec(memory_space=pl.ANY),
        out_shape=jax.ShapeDtypeStruct((n_rows, d), x1.dtype),
        scratch_shapes=[pltpu.VMEM((bm, d), x1.dtype), pltpu.SemaphoreType.DMA(()), pltpu.SemaphoreType.DMA(())],
        compiler_params=_cparams(("arbitrary",)),
        name="dispatch",
    )(dest, meta, x1)


def _experts_kernel(blke_ref, nused_ref, xs_ref, wu_ref, bu_ref, wd_ref, bd_ref, ys_ref, wu16, wd16):
    i = pl.program_id(0)
    dff = wd_ref.shape[1]

    @pl.when(i < nused_ref[0])
    def _():
        @pl.when(jnp.logical_or(i == 0, blke_ref[i] != blke_ref[jnp.maximum(i - 1, 0)]))
        def _():
            wu16[...] = wu_ref[0].astype(BF16)
            wd16[...] = wd_ref[0].astype(BF16)

        h = jnp.dot(xs_ref[...].astype(BF16), wu16[...], preferred_element_type=F32) + bu_ref[0]
        glu = jnp.minimum(h[:, :dff], SWIGLU_LIMIT)
        lin = jnp.clip(h[:, dff:], -SWIGLU_LIMIT, SWIGLU_LIMIT)
        act = glu * _sigmoid(SWIGLU_ALPHA * glu) * (lin + 1.0)
        ys_ref[...] = jnp.dot(act.astype(BF16), wd16[...], preferred_element_type=F32) + bd_ref[0]

    @pl.when(i >= nused_ref[0])
    def _():
        ys_ref[...] = jnp.zeros_like(ys_ref)


def _experts(blk_e, n_used, xs, wu, bu, wd, bd, *, bm):
    n_rows, d = xs.shape
    ne, _, dff2 = wu.shape
    dff = wd.shape[1]
    used = lambda i, nu: jnp.minimum(i, nu[0] - 1)
    grid_spec = pltpu.PrefetchScalarGridSpec(
        num_scalar_prefetch=2,
        grid=(n_rows // bm,),
        in_specs=[pl.BlockSpec((bm, d), lambda i, e, nu: (used(i, nu), 0)),
                  pl.BlockSpec((1, d, dff2), lambda i, e, nu: (e[i], 0, 0)),
                  pl.BlockSpec((1, 1, dff2), lambda i, e, nu: (e[i], 0, 0)),
                  pl.BlockSpec((1, dff, d), lambda i, e, nu: (e[i], 0, 0)),
                  pl.BlockSpec((1, 1, d), lambda i, e, nu: (e[i], 0, 0))],
        out_specs=pl.BlockSpec((bm, d), lambda i, e, nu: (i, 0)),
        scratch_shapes=[pltpu.VMEM((d, dff2), BF16), pltpu.VMEM((dff, d), BF16)],
    )
    return pl.pallas_call(
        _experts_kernel,
        grid_spec=grid_spec,
        out_shape=jax.ShapeDtypeStruct((n_rows, d), F32),
        compiler_params=_cparams(("arbitrary",)),
        name="experts",
    )(blk_e, n_used, xs, wu, bu.reshape(ne, 1, dff2), wd, bd.reshape(ne, 1, d))


def _combine_kernel(dest_ref, ys_ref, gate_ref, x1_ref, g2_ref, b2_ref, outp_ref, outs_ref, buf, sem, *, alpha, nt_p):
    i = pl.program_id(0)
    tm = x1_ref.shape[0]

    def start(t, carry):
        for k in range(TOP_K):
            pltpu.make_async_copy(ys_ref.at[pl.ds(dest_ref[0, k, t], 1), :], buf.at[k, pl.ds(t, 1), :], sem).start()
        return carry
    lax.fori_loop(0, tm, start, 0, unroll=ROW_COPY_UNROLL)

    for k in range(TOP_K):
        pltpu.make_async_copy(ys_ref.at[pl.ds(0, tm), :], buf.at[k], sem).wait()

    gate = gate_ref[...].T
    y = buf[0] * gate[:, 0:1]
    for k in range(1, TOP_K):
        y = y + buf[k] * gate[:, k:k + 1]
    out = _layer_norm(alpha * x1_ref[...] + y, g2_ref[...], b2_ref[...])

    @pl.when(i < nt_p)
    def _():
        outp_ref[...] = out

    @pl.when(i >= nt_p)
    def _():
        outs_ref[...] = out


def _combine(dest, ys, gate, x1, g2, b2, *, n_p, tm, alpha):
    n, d = x1.shape
    nt_p = n_p // tm
    return pl.pallas_call(
        functools.partial(_combine_kernel, alpha=alpha, nt_p=nt_p),
        grid=(n // tm,),
        in_specs=[pl.BlockSpec((1, TOP_K, tm), lambda i: (i, 0, 0), memory_space=pltpu.SMEM),
                  pl.BlockSpec(memory_space=pl.ANY),
                  pl.BlockSpec((SUBLANES, tm), lambda i: (0, i)),
                  pl.BlockSpec((tm, d), lambda i: (i, 0)),
                  pl.BlockSpec((1, d), lambda i: (0, 0)),
                  pl.BlockSpec((1, d), lambda i: (0, 0))],
        out_specs=[pl.BlockSpec((tm, d), lambda i: (jnp.minimum(i, nt_p - 1), 0)),
                   pl.BlockSpec((tm, d), lambda i: (jnp.maximum(i - nt_p, 0), 0))],
        out_shape=[jax.ShapeDtypeStruct((n_p, d), F32), jax.ShapeDtypeStruct((n - n_p, d), F32)],
        scratch_shapes=[pltpu.VMEM((TOP_K, tm, d), F32), pltpu.SemaphoreType.DMA(())],
        compiler_params=_cparams(("arbitrary",)),
        name="combine",
    )(dest, ys, gate, x1, g2, b2)


PROJ_TM = 256
HGRN_TL = 256
DSA_TQ = 256
MIX_TM = 256
ROUTE_TL = 1280
MOE_TM = 256
MOE_BM = 512


def kernel(x_prompt, x_sample, cache_k, cache_v, cache_kidx, state_hgrn, w_in, lb_logits, gn_a, w_oa, w_ob,
           w_out, ln1_g, ln1_b, w_router, b_router, w_up, b_up, w_down, b_down, ln2_g, ln2_b):
    depth = w_in.shape[0]
    assert depth == 1
    bsz, seq, d = x_prompt.shape
    dbsz, dseq, _ = x_sample.shape
    past = cache_k.shape[2]
    n_p, n_s = bsz * seq, dbsz * dseq
    n = n_p + n_s
    alpha = (2 * depth) ** 0.25

    sizes = (H_A * DK_A, H_A * DK_A, H_A * DV_A, H_A * DV_A, H_B * DH_B, DH_B, DH_B, H_I * D_I, H_I, D_I, d, d)
    offs = np.concatenate([[0], np.cumsum(sizes)])
    w = w_in[0].astype(BF16)
    grp = lambda g: w[:, offs[g]:offs[g + 1]]
    wqa, wfa, wia, wg, wqb, wkb, wvb, wqi, wwi, wki, wgta, wgtb = (grp(g) for g in range(12))
    wsm = jnp.concatenate([wkb, wvb, wqi, wki, wwi, jnp.zeros((d, SM_WIDTH - SM_USED), BF16)], axis=1)

    x_groups = (x_prompt.reshape(n_p, d), x_sample.reshape(n_s, d))
    qa, f, ia, sg, qb, sm = _proj(*x_groups, wqa, wfa, wia, wg, wqb, wsm, lb_logits, PROJ_TM)

    gn = gn_a[0].reshape(1, DV_A)
    oa_p, sfin_p = _hgrn(qa, f, ia, sg, gn, jnp.zeros((bsz, H_A, DK_A, DV_A), F32),
                         bsz=bsz, seq=seq, row0=0, tl=HGRN_TL)
    oa_s, sfin_s = _hgrn(qa, f, ia, sg, gn, state_hgrn[0], bsz=dbsz, seq=dseq, row0=n_p, tl=dseq)

    kb_p = sm[:n_p, SM_KB:SM_KB + DH_B].reshape(bsz, seq, DH_B)
    vb_p = sm[:n_p, SM_VB:SM_VB + DH_B].reshape(bsz, seq, DH_B)
    ki_p = sm[:n_p, SM_KI:SM_KI + D_I].reshape(bsz, seq, D_I)
    kb_s = sm[n_p:, SM_KB:SM_KB + DH_B].reshape(dbsz, dseq, DH_B)
    vb_s = sm[n_p:, SM_VB:SM_VB + DH_B].reshape(dbsz, dseq, DH_B)
    ki_s = sm[n_p:, SM_KI:SM_KI + D_I].reshape(dbsz, dseq, D_I)
    lim_p = (np.arange(seq) // CHUNK + 1) * CHUNK
    ob_p = _dsa(qb, sm, [(kb_p.astype(BF16), vb_p.astype(BF16), ki_p.astype(BF16))], lim_p,
                bsz=bsz, seq=seq, row0=0, tq=DSA_TQ, n_sel=min(TOPK_MAX, seq // 4))
    n_keys = past + dseq
    new_pad = -(-dseq // LANES) * LANES - dseq
    new_keys = tuple(jnp.pad(a.astype(BF16), ((0, 0), (0, new_pad), (0, 0))) for a in (kb_s, vb_s, ki_s))
    lim_s = np.full((dseq,), n_keys)
    ob_s = _dsa(qb, sm, [(cache_k[0], cache_v[0], cache_kidx[0]), new_keys], lim_s,
                bsz=dbsz, seq=dseq, row0=n_p, tq=dseq, n_sel=min(TOPK_MAX, n_keys // 4))

    wrh, wrl = _split_bf16(w_router[0].T)
    x1, eidx, gate, counts = _mix(
        (oa_p, oa_s), (ob_p, ob_s), x_groups, w_oa[0].astype(BF16), w_ob[0].astype(BF16), wgta, wgtb, w_out[0].astype(BF16),
        ln1_g[0].reshape(1, d), ln1_b[0].reshape(1, d), wrh, wrl, b_router[0].reshape(N_EXPERTS, 1),
        tm=MIX_TM, alpha=alpha)

    n_rows = -(-(n * TOP_K + N_EXPERTS * (MOE_BM - 1)) // MOE_BM) * MOE_BM
    dest, blk_e, meta = _route(eidx, counts, tl=ROUTE_TL, sub=MOE_TM, bm=MOE_BM, nb=n_rows // MOE_BM)
    xs = _dispatch(dest, meta, x1, n_rows, tm=MOE_TM, bm=MOE_BM, ne=N_EXPERTS)
    ys = _experts(blk_e[0, :n_rows // MOE_BM], meta[1, :1], xs, w_up[0], b_up[0], w_down[0], b_down[0], bm=MOE_BM)
    out_p, out_s = _combine(dest, ys, gate, x1, ln2_g[0].reshape(1, d), ln2_b[0].reshape(1, d),
                            n_p=n_p, tm=MOE_TM, alpha=alpha)

    return (out_p.reshape(bsz, seq, d), out_s.reshape(dbsz, dseq, d),
            kb_p[None], vb_p[None], ki_p[None], sfin_p[None],
            kb_s[None], vb_s[None], ki_s[None], sfin_s[None])
```

```python
import functools

import jax
import jax.numpy as jnp
import numpy as np
from jax import lax
from jax.experimental import pallas as pl
from jax.experimental.pallas import tpu as pltpu

F32 = jnp.float32
BF16 = jnp.bfloat16
I32 = jnp.int32

CHUNK = 64
H_A = 8
DK_A = 128
DV_A = 128
HGRN_BLOCK = 16
H_B = 8
DH_B = 128
H_I = 8
D_I = 64
TOPK_MAX = 256
N_EXPERTS = 32
TOP_K = 4
SWIGLU_ALPHA = 1.702
SWIGLU_LIMIT = 7.0
EPS = 1e-5

LANES = 128
SUBLANES = 8
VMEM_LIMIT_BYTES = 56 * 1024 * 1024

INT_MIN = -(2 ** 31)


def _cparams(sem):
    return pltpu.CompilerParams(dimension_semantics=sem, vmem_limit_bytes=VMEM_LIMIT_BYTES)


def _sigmoid(x):
    return 1.0 / (1.0 + jnp.exp(-x))


def _full_spec(shape):
    nd = len(shape)
    return pl.BlockSpec(shape, lambda *_: (0,) * nd)


def _layer_norm(x, g, b):
    mu = jnp.mean(x, axis=-1, keepdims=True)
    xc = x - mu
    var = jnp.mean(xc * xc, axis=-1, keepdims=True)
    return xc * lax.rsqrt(var + EPS) * g + b


def _two_group_specs(tm, width, nt_p):
    return (pl.BlockSpec((tm, width), lambda i: (jnp.minimum(i, nt_p - 1), 0)),
            pl.BlockSpec((tm, width), lambda i: (jnp.maximum(i - nt_p, 0), 0)))


def _group_tile(p_ref, s_ref, nt_p):
    return jnp.where(pl.program_id(0) < nt_p, p_ref[...], s_ref[...])


def _proj_kernel(xp_ref, xs_ref, wqa_ref, wfa_ref, wia_ref, wga_ref, wqb_ref, wsm_ref, lbl_ref,
                 qa_ref, f_ref, ia_ref, sg_ref, qb_ref, sm_ref, *, nt_p):
    xb = _group_tile(xp_ref, xs_ref, nt_p).astype(BF16)
    dot = lambda w_ref: jnp.dot(xb, w_ref[...], preferred_element_type=F32)
    qa_ref[...] = dot(wqa_ref)
    lbl = lbl_ref[...]
    lbe = jnp.exp(lbl - jnp.max(lbl, axis=0, keepdims=True))
    lb = lbe[0:1, :] / jnp.sum(lbe, axis=0, keepdims=True)
    f_ref[...] = lb + (1.0 - lb) * _sigmoid(dot(wfa_ref))
    ia_ref[...] = dot(wia_ref)
    g = dot(wga_ref)
    sg_ref[...] = g * _sigmoid(g)
    qb_ref[...] = dot(wqb_ref).astype(BF16)
    sm_ref[...] = dot(wsm_ref)


def _proj(x_p, x_s, wqa, wfa, wia, wga, wqb, wsm, lb_logits, tm):
    d = x_p.shape[1]
    n = x_p.shape[0] + x_s.shape[0]
    nt_p = x_p.shape[0] // tm
    nsm = wsm.shape[1]
    row = lambda w: pl.BlockSpec((tm, w), lambda i: (i, 0))
    wspec = lambda w: pl.BlockSpec(w.shape, lambda i: (0, 0))
    return pl.pallas_call(
        functools.partial(_proj_kernel, nt_p=nt_p),
        grid=(n // tm,),
        in_specs=[*_two_group_specs(tm, d, nt_p), wspec(wqa), wspec(wfa), wspec(wia), wspec(wga), wspec(wqb),
                  wspec(wsm), wspec(lb_logits)],
        out_specs=[row(d), row(d), row(d), row(d), row(d), row(nsm)],
        out_shape=[jax.ShapeDtypeStruct((n, d), F32)] * 4
        + [jax.ShapeDtypeStruct((n, d), BF16), jax.ShapeDtypeStruct((n, nsm), F32)],
        compiler_params=_cparams(("arbitrary",)),
        name="proj",
    )(x_p, x_s, wqa, wfa, wia, wga, wqb, wsm, lb_logits)


HGRN_UNROLL = 8


def _block_cumsum(a, row_in_block):
    s = 1
    while s < HGRN_BLOCK:
        a = a + jnp.where(row_in_block >= s, pltpu.roll(a, s, 0), 0.0)
        s *= 2
    return a


def _hgrn_kernel(qa_ref, f_ref, ia_ref, sg_ref, gn_ref, s0_ref, o_ref, sfin_ref,
                 s_scr, qd_scr, qm_scr, km_scr, kl_scr, v_scr, dl_scr, o_scr):
    j = pl.program_id(1)
    tl = qa_ref.shape[0]
    nblk = tl // HGRN_BLOCK
    half = HGRN_BLOCK // 2

    @pl.when(j == 0)
    def _():
        for h in range(H_A):
            s_scr[h] = s0_ref[0, h].T

    f = f_ref[...]
    q = qa_ref[...]
    d = f.shape[1]
    row_in_block = lax.broadcasted_iota(I32, (tl, d), 0) % HGRN_BLOCK
    cum = _block_cumsum(jnp.log(f), row_in_block)
    cum3 = cum.reshape(nblk, HGRN_BLOCK, d)
    mid = jnp.broadcast_to(cum3[:, half:half + 1, :], cum3.shape).reshape(tl, d)
    last3 = cum3[:, HGRN_BLOCK - 1:HGRN_BLOCK, :]
    last = jnp.broadcast_to(last3, cum3.shape).reshape(tl, d)
    k = 1.0 - f
    qd_scr[...] = (q * jnp.exp(cum)).astype(BF16)
    qm_scr[...] = (q * jnp.exp(cum - mid)).astype(BF16)
    km_scr[...] = (k * jnp.exp(mid - cum)).astype(BF16)
    kl_scr[...] = (k * jnp.exp(last - cum)).astype(BF16)
    dl_scr[...] = jnp.exp(last3.reshape(nblk, d))
    v_scr[...] = ia_ref[...].astype(BF16)

    nt = (((1,), (1,)), ((), ()))
    ri = lax.broadcasted_iota(I32, (tl, tl), 0)
    ci = lax.broadcasted_iota(I32, (tl, tl), 1)
    keep = (ri // HGRN_BLOCK == ci // HGRN_BLOCK) & (ri >= ci)
    for h in range(H_A):
        cols = slice(h * DK_A, (h + 1) * DK_A)
        vcols = slice(h * DV_A, (h + 1) * DV_A)
        att = lax.dot_general(qm_scr[:, cols], km_scr[:, cols], nt, preferred_element_type=F32)
        att = jnp.where(keep, att, 0.0).astype(BF16)
        o_scr[:, vcols] = jnp.dot(att, v_scr[:, vcols], preferred_element_type=F32)

    def block(b, carry):
        r0 = pl.multiple_of(b * HGRN_BLOCK, HGRN_BLOCK)
        rows = pl.ds(r0, HGRN_BLOCK)
        dl = dl_scr[pl.ds(b, 1), :]
        for h in range(H_A):
            cols = slice(h * DK_A, (h + 1) * DK_A)
            vcols = slice(h * DV_A, (h + 1) * DV_A)
            st_h = s_scr[h]
            o_scr[rows, vcols] += lax.dot_general(qd_scr[rows, cols], st_h.astype(BF16), nt,
                                                  preferred_element_type=F32)
            upd_t = lax.dot_general(v_scr[rows, vcols], kl_scr[rows, cols], (((0,), (0,)), ((), ())),
                                    preferred_element_type=F32)
            s_scr[h] = dl[:, cols] * st_h + upd_t
        return carry

    lax.fori_loop(0, nblk, block, 0, unroll=HGRN_UNROLL if nblk % HGRN_UNROLL == 0 else 1)

    gn = gn_ref[...]
    for h in range(H_A):
        vcols = slice(h * DV_A, (h + 1) * DV_A)
        o = o_scr[:, vcols]
        o = o * lax.rsqrt(jnp.mean(o * o, axis=-1, keepdims=True) + EPS) * gn
        o_ref[:, vcols] = (o * sg_ref[:, vcols]).astype(BF16)

    @pl.when(j == pl.num_programs(1) - 1)
    def _():
        for h in range(H_A):
            sfin_ref[0, h] = s_scr[h].T


def _hgrn(qa, f, ia, sg, gn, s0, *, bsz, seq, row0, tl):
    d = qa.shape[1]
    nt = seq // tl
    blk0 = row0 // tl
    row = pl.BlockSpec((tl, d), lambda b, j: (blk0 + b * nt + j, 0))
    orow = pl.BlockSpec((tl, d), lambda b, j: (b * nt + j, 0))
    sspec = pl.BlockSpec((1, H_A, DK_A, DV_A), lambda b, j: (b, 0, 0, 0))
    return pl.pallas_call(
        _hgrn_kernel,
        grid=(bsz, nt),
        in_specs=[row, row, row, row, pl.BlockSpec((1, DV_A), lambda b, j: (0, 0)), sspec],
        out_specs=[orow, sspec],
        out_shape=[jax.ShapeDtypeStruct((bsz * seq, d), BF16),
                   jax.ShapeDtypeStruct((bsz, H_A, DK_A, DV_A), F32)],
        scratch_shapes=[pltpu.VMEM((H_A, DK_A, DV_A), F32)]
        + [pltpu.VMEM((tl, d), BF16)] * 5
        + [pltpu.VMEM((tl // HGRN_BLOCK, d), F32), pltpu.VMEM((tl, d), F32)],
        compiler_params=_cparams(("arbitrary", "arbitrary")),
        name="hgrn",
    )(qa, f, ia, sg, gn, s0)


SM_KB, SM_VB, SM_QI, SM_KI, SM_WI = 0, DH_B, 2 * DH_B, 2 * DH_B + H_I * D_I, 2 * DH_B + H_I * D_I + D_I
SM_USED = SM_WI + H_I
SM_WIDTH = -(-SM_USED // LANES) * LANES
INDEX_SCALE = (H_I * D_I) ** -0.5
NEG_INF = float("-inf")
LOG2_E = 1.4426950408889634


F32_EXP_MASK = 0x7F800000
F32_MIN_NORMAL = 0x00800000
SEARCH_UNROLL = 4


def _key_to_float(u):
    key = u ^ jnp.int32(INT_MIN)
    bits = jnp.where(key < 0, key ^ jnp.int32(0x7FFFFFFF), key)
    below_neg_inf = (bits < 0) & ((bits & jnp.int32(0x7FFFFFFF)) > jnp.int32(F32_EXP_MASK))
    return jnp.where(below_neg_inf, NEG_INF, lax.bitcast_convert_type(bits, F32))


def _positive_bits_to_float(bits):
    return jnp.where(bits < jnp.int32(F32_MIN_NORMAL), 0.0, lax.bitcast_convert_type(bits, F32))


def _count(mask):
    return jnp.sum(jnp.where(mask, 1.0, 0.0), axis=1, keepdims=True)


def _select_bias(score_scr, bias_scr, adm, s, n_sel):
    tq = score_scr.shape[0]

    def resolve_bit(i, t_u, cnt_t):
        cand_u = t_u | lax.shift_left(jnp.int32(1), 31 - i)
        cnt = _count(score_scr[:, :s] >= _key_to_float(cand_u))
        ok = cnt >= n_sel
        return jnp.where(ok, cand_u, t_u), jnp.where(ok, cnt, cnt_t)

    def unresolved(c):
        i, _, cnt_t = c
        return jnp.logical_and(i < 32, jnp.any(cnt_t != n_sel))

    def resolve_bits(c):
        i, t_u, cnt_t = c
        for b in range(SEARCH_UNROLL):
            t_u, cnt_t = resolve_bit(i + b, t_u, cnt_t)
        return i + SEARCH_UNROLL, t_u, cnt_t

    _, t_u, cnt_t = lax.while_loop(unresolved, resolve_bits,
                                   (jnp.int32(0), jnp.zeros((tq, 1), I32), jnp.full((tq, 1), float(s), F32)))
    t_f = _key_to_float(t_u)
    score = score_scr[:, :s]
    bias_scr[:, :s] = jnp.where((score >= t_f) & adm, 0.0, NEG_INF)

    @pl.when(jnp.any(cnt_t > n_sel))
    def _():
        above = score >= _key_to_float(t_u + 1)
        bucket = (score >= t_f) & jnp.logical_not(above)
        need = n_sel - _count(above)
        off = jnp.where(bucket, score - jnp.where(t_f == NEG_INF, 0.0, t_f), -1.0)

        def resolve_offset_bit(i, r_bits):
            cand = r_bits | lax.shift_left(jnp.int32(1), 30 - i)
            ok = _count(off >= _positive_bits_to_float(cand)) >= need
            return jnp.where(ok, cand, r_bits)

        ranked = jnp.any(off > 0.0)
        r_bits = lax.fori_loop(0, jnp.where(ranked, 31, 0), resolve_offset_bit,
                               jnp.full((tq, 1), jnp.where(ranked, 0, F32_MIN_NORMAL - 1), I32))
        above2 = off >= _positive_bits_to_float(r_bits + 1)
        tie = (off >= _positive_bits_to_float(r_bits)) & jnp.logical_not(above2)
        need2 = need - _count(above2)
        upper = (lax.broadcasted_iota(I32, (LANES, LANES), 0)
                 <= lax.broadcasted_iota(I32, (LANES, LANES), 1)).astype(BF16)
        carry = jnp.zeros((tq, 1), F32)
        for c in range(s // LANES):
            cs = slice(c * LANES, (c + 1) * LANES)
            rank = jnp.dot(jnp.where(tie[:, cs], 1.0, 0.0).astype(BF16), upper, preferred_element_type=F32) + carry
            sel = (above[:, cs] | above2[:, cs] | (tie[:, cs] & (rank <= need2))) & adm[:, cs]
            bias_scr[:, cs] = jnp.where(sel, 0.0, NEG_INF)
            carry = rank[:, LANES - 1:LANES]


def _dsa_tile(qb_ref, sm_ref, *refs, segs, search, n_sel):
    key_refs, (lim_ref, o_ref, score_scr, bias_scr) = refs[:3 * len(segs)], refs[3 * len(segs):]
    tq = qb_ref.shape[0]
    s = sum(segs)
    nt = (((1,), (1,)), ((), ()))
    adm = lax.broadcasted_iota(I32, (tq, s), 1) < lim_ref[...]
    seg_keys = [[key_refs[3 * g + j][0, :sg, :].astype(BF16) for j in range(3)] for g, sg in enumerate(segs)]
    starts = [sum(segs[:g]) for g in range(len(segs))]
    against_keys = lambda lhs, j: jnp.concatenate(
        [lax.dot_general(lhs, kv[j], nt, preferred_element_type=F32) for kv in seg_keys], axis=1)

    if search:
        score = jnp.zeros((tq, s), F32)
        for h in range(H_I):
            qih = sm_ref[:, SM_QI + h * D_I:SM_QI + (h + 1) * D_I].astype(BF16)
            wih = sm_ref[:, SM_WI + h:SM_WI + h + 1] * INDEX_SCALE
            score = score + jnp.maximum(against_keys(qih, 2), 0.0) * wih
        score_scr[:, :s] = jnp.where(adm, score, NEG_INF)
        _select_bias(score_scr, bias_scr, adm, s, n_sel)
    else:
        bias_scr[:, :s] = jnp.where(adm, 0.0, NEG_INF)

    for h in range(H_B):
        hs = slice(h * DH_B, (h + 1) * DH_B)
        logits = against_keys(qb_ref[:, hs], 0) + bias_scr[:, :s]
        p = jnp.exp2((logits - jnp.max(logits, axis=1, keepdims=True)) * (DH_B ** -0.5 * LOG2_E))
        denom = jnp.sum(p, axis=1, keepdims=True)
        pb = p.astype(BF16)
        o = sum(jnp.dot(pb[:, st:st + sg], kv[1], preferred_element_type=F32)
                for st, sg, kv in zip(starts, segs, seg_keys))
        o_ref[:, hs] = (o / denom).astype(BF16)


DSA_KEY_GRAN = 256


def _dsa(qb, sm, key_segments, limits, *, bsz, seq, row0, tq, n_sel):
    d = qb.shape[1]
    seg_rows = [seg[0].shape[1] for seg in key_segments]
    n_keys = sum(seg_rows)
    nt = seq // tq
    blk0 = row0 // tq
    limits = np.asarray(limits, np.int32)
    tile_max = limits.reshape(nt, tq).max(axis=1)
    per_tile = [(int(min(n_keys, -(-m // DSA_KEY_GRAN) * DSA_KEY_GRAN)), bool(m > n_sel)) for m in tile_max]
    lim2d = jnp.asarray(limits.reshape(seq, 1))
    outs, lo = [], 0
    while lo < nt:
        hi = lo
        while hi + 1 < nt and per_tile[hi + 1] == per_tile[lo]:
            hi += 1
        s, search = per_tile[lo]
        assert s == n_keys or len(key_segments) == 1
        segs = tuple(seg_rows) if s == n_keys else (s,)
        ntv = hi - lo + 1
        row = lambda w, lo=lo: pl.BlockSpec((tq, w), lambda b, j: (blk0 + b * nt + lo + j, 0))
        keys = lambda rows, w: pl.BlockSpec((1, rows, w), lambda b, j: (b, 0, 0))
        out = pl.pallas_call(
            functools.partial(_dsa_tile, segs=segs, search=search, n_sel=n_sel),
            grid=(bsz, ntv),
            in_specs=[row(d), row(sm.shape[1])]
            + [keys(rows, a.shape[2]) for rows, seg in zip(segs, key_segments) for a in seg]
            + [pl.BlockSpec((tq, 1), lambda b, j, lo=lo: (lo + j, 0))],
            out_specs=pl.BlockSpec((tq, d), lambda b, j, ntv=ntv: (b * ntv + j, 0)),
            out_shape=jax.ShapeDtypeStruct((bsz * ntv * tq, d), BF16),
            scratch_shapes=[pltpu.VMEM((tq, s), F32), pltpu.VMEM((tq, s), F32)],
            compiler_params=_cparams(("arbitrary", "arbitrary")),
            name="dsa",
        )(qb, sm, *[a for seg in key_segments for a in seg], lim2d)
        outs.append(out.reshape(bsz, ntv * tq, d))
        lo = hi + 1
    return jnp.concatenate(outs, axis=1).reshape(bsz * seq, d)


def _split_bf16(a):
    hi = a.astype(BF16)
    return hi, (a - hi.astype(F32)).astype(BF16)


def _mix_kernel(oap_ref, oas_ref, obp_ref, obs_ref, xp_ref, xs_ref, woa_ref, wob_ref, wga_ref, wgb_ref, wout_ref,
                g1_ref, b1_ref, wrh_ref, wrl_ref, br_ref, x1_ref, eidx_ref, gate_ref, cnt_ref, *, alpha, nt_p):
    i = pl.program_id(0)
    x = _group_tile(xp_ref, xs_ref, nt_p)
    xb = x.astype(BF16)
    dot = lambda a, w_ref: jnp.dot(a, w_ref[...], preferred_element_type=F32)
    y_a = dot(_group_tile(oap_ref, oas_ref, nt_p), woa_ref)
    y_b = dot(_group_tile(obp_ref, obs_ref, nt_p), wob_ref)
    merged = _sigmoid(dot(xb, wga_ref)) * y_a + _sigmoid(dot(xb, wgb_ref)) * y_b
    mixed = dot(merged.astype(BF16), wout_ref)
    x1 = _layer_norm(alpha * x + mixed, g1_ref[...], b1_ref[...])
    x1_ref[...] = x1

    nt = (((1,), (1,)), ((), ()))
    xh, xl = _split_bf16(x1)
    wh, wl = wrh_ref[...], wrl_ref[...]
    logits = (lax.dot_general(wh, xh, nt, preferred_element_type=F32)
              + lax.dot_general(wh, xl, nt, preferred_element_type=F32)
              + lax.dot_general(wl, xh, nt, preferred_element_type=F32)) + br_ref[...]
    ne, tm = logits.shape
    erow = lax.broadcasted_iota(I32, (ne, tm), 0)
    vals, idxs = [], []
    for _ in range(TOP_K):
        m = jnp.max(logits, axis=0, keepdims=True)
        idx = jnp.min(jnp.where(logits == m, erow, ne), axis=0, keepdims=True)
        vals.append(m)
        idxs.append(idx)
        logits = jnp.where(erow == idx, NEG_INF, logits)
    ex = [jnp.exp(v - vals[0]) for v in vals]
    denom = ex[0] + ex[1] + ex[2] + ex[3]
    pad = SUBLANES - TOP_K
    eidx = jnp.concatenate(idxs + [jnp.zeros((pad, tm), I32)], axis=0)
    eidx_ref[...] = eidx
    gate_ref[...] = jnp.concatenate([e / denom for e in ex] + [jnp.zeros((pad, tm), F32)], axis=0)

    onehot = jnp.zeros((ne, tm), F32)
    for idx in idxs:
        onehot = onehot + jnp.where(erow == idx, 1.0, 0.0)
    tile_cnt = jnp.broadcast_to(jnp.sum(onehot, axis=1, keepdims=True), cnt_ref.shape)

    @pl.when(i == 0)
    def _():
        cnt_ref[...] = tile_cnt

    @pl.when(i > 0)
    def _():
        cnt_ref[...] = cnt_ref[...] + tile_cnt


def _mix(oa, ob, x, woa, wob, wga, wgb, wout, g1, b1, wrh, wrl, br, *, tm, alpha):
    d = x[0].shape[1]
    n = x[0].shape[0] + x[1].shape[0]
    nt_p = x[0].shape[0] // tm
    ne = wrh.shape[0]
    row = lambda w: pl.BlockSpec((tm, w), lambda i: (i, 0))
    col = pl.BlockSpec((SUBLANES, tm), lambda i: (0, i))
    full = lambda a: pl.BlockSpec(a.shape, lambda i: (0,) * a.ndim)
    return pl.pallas_call(
        functools.partial(_mix_kernel, alpha=alpha, nt_p=nt_p),
        grid=(n // tm,),
        in_specs=[*_two_group_specs(tm, d, nt_p)] * 3
        + [full(a) for a in (woa, wob, wga, wgb, wout, g1, b1, wrh, wrl, br)],
        out_specs=[row(d), col, col, pl.BlockSpec((ne, LANES), lambda i: (0, 0))],
        out_shape=[jax.ShapeDtypeStruct((n, d), F32), jax.ShapeDtypeStruct((SUBLANES, n), I32),
                   jax.ShapeDtypeStruct((SUBLANES, n), F32), jax.ShapeDtypeStruct((ne, LANES), F32)],
        compiler_params=_cparams(("arbitrary",)),
        name="mix",
    )(*oa, *ob, *x, woa, wob, wga, wgb, wout, g1, b1, wrh, wrl, br)


def _sublane_cumsum(a):
    n = a.shape[0]
    row = lax.broadcasted_iota(I32, a.shape, 0)
    s = 1
    while s < n:
        a = a + jnp.where(row >= s, pltpu.roll(a, s, 0), 0.0)
        s *= 2
    return a


def _route_kernel(eidx_ref, cnt_ref, dest_ref, blke_ref, meta_ref, carry_scr, *, bm, sub):
    i = pl.program_id(0)
    ne = cnt_ref.shape[0]
    tl = eidx_ref.shape[1]

    @pl.when(i == 0)
    def _():
        carry_scr[...] = jnp.zeros_like(carry_scr)

    counts = cnt_ref[...]
    padded = jnp.ceil(counts / bm) * bm
    pend = _sublane_cumsum(padded)
    pstart = (pend - padded)[:, 0:1]

    @pl.when(i == 0)
    def _():
        nb = blke_ref.shape[1]
        first_row = (lax.broadcasted_iota(I32, (ne, nb), 1) * bm).astype(F32)
        below = jnp.sum(jnp.where(pend[:, 0:1] <= first_row, 1.0, 0.0), axis=0, keepdims=True)
        blke_ref[...] = jnp.minimum(below, ne - 1.0).astype(I32)
        pend_lanes = jnp.concatenate([pend, jnp.zeros((LANES - ne, LANES), F32)], axis=0).T[0:1, :]
        n_used = jnp.max(pend, axis=0, keepdims=True) / bm
        meta_ref[...] = jnp.concatenate([pend_lanes, n_used, jnp.zeros((SUBLANES - 2, LANES), F32)],
                                        axis=0).astype(I32)

    eidx = eidx_ref[...]
    erow = lax.broadcasted_iota(I32, (ne, tl), 0)
    hot = [jnp.where(erow == eidx[k:k + 1, :], 1.0, 0.0) for k in range(TOP_K)]
    onehot = hot[0] + hot[1] + hot[2] + hot[3]
    before = (lax.broadcasted_iota(I32, (tl, tl), 0) < lax.broadcasted_iota(I32, (tl, tl), 1)).astype(BF16)
    base = jnp.dot(onehot.astype(BF16), before, preferred_element_type=F32) + carry_scr[:, 0:1] + pstart
    dest = jnp.concatenate([jnp.sum(hk * base, axis=0, keepdims=True) for hk in hot], axis=0).astype(I32)
    for c in range(tl // sub):
        dest_ref[c] = dest[:, c * sub:(c + 1) * sub]
    carry_scr[...] = carry_scr[...] + jnp.sum(onehot, axis=1, keepdims=True)


def _route(eidx, counts, *, tl, sub, bm, nb):
    n = eidx.shape[1]
    ne = counts.shape[0]
    nb_pad = -(-nb // LANES) * LANES
    return pl.pallas_call(
        functools.partial(_route_kernel, bm=bm, sub=sub),
        grid=(n // tl,),
        in_specs=[pl.BlockSpec((SUBLANES, tl), lambda i: (0, i)), pl.BlockSpec(counts.shape, lambda i: (0, 0))],
        out_specs=[pl.BlockSpec((tl // sub, TOP_K, sub), lambda i: (i, 0, 0)),
                   pl.BlockSpec((1, nb_pad), lambda i: (0, 0)),
                   pl.BlockSpec((SUBLANES, LANES), lambda i: (0, 0))],
        out_shape=[jax.ShapeDtypeStruct((n // sub, TOP_K, sub), I32), jax.ShapeDtypeStruct((1, nb_pad), I32),
                   jax.ShapeDtypeStruct((SUBLANES, LANES), I32)],
        scratch_shapes=[pltpu.VMEM((ne, LANES), F32)],
        compiler_params=_cparams(("arbitrary",)),
        name="route",
    )(eidx, counts)


ROW_COPY_UNROLL = 4


def _dispatch_kernel(dest_ref, meta_ref, x_ref, xs_ref, zbuf, sem, zsem, *, bm, ne, nb):
    i = pl.program_id(0)
    tm = x_ref.shape[0]

    @pl.when(i == 0)
    def _():
        zbuf[...] = jnp.zeros_like(zbuf)
        zero_block = lambda row0: pltpu.make_async_copy(zbuf, xs_ref.at[pl.ds(row0, bm), :], zsem)

        def each_padding_block(action):
            for e in range(ne):
                seg_end = meta_ref[0, e]
                seg_start = meta_ref[0, e - 1] if e > 0 else 0

                @pl.when(seg_end > seg_start)
                def _(seg_end=seg_end):
                    action(zero_block(pl.multiple_of(seg_end - bm, bm)))

            def tail_block(b, carry):
                action(zero_block(pl.multiple_of(b * bm, bm)))
                return carry
            lax.fori_loop(meta_ref[1, 0], nb, tail_block, 0)

        each_padding_block(lambda copy: copy.start())
        each_padding_block(lambda copy: copy.wait())

    def start(t, carry):
        for k in range(TOP_K):
            pltpu.make_async_copy(x_ref.at[pl.ds(t, 1), :], xs_ref.at[pl.ds(dest_ref[0, k, t], 1), :], sem).start()
        return carry
    lax.fori_loop(0, tm, start, 0, unroll=ROW_COPY_UNROLL)

    for _ in range(TOP_K):
        pltpu.make_async_copy(x_ref, xs_ref.at[pl.ds(0, tm), :], sem).wait()


def _dispatch(dest, meta, x1, n_rows, *, tm, bm, ne):
    n, d = x1.shape
    return pl.pallas_call(
        functools.partial(_dispatch_kernel, bm=bm, ne=ne, nb=n_rows // bm),
        grid=(n // tm,),
        in_specs=[pl.BlockSpec((1, TOP_K, tm), lambda i: (i, 0, 0), memory_space=pltpu.SMEM),
                  pl.BlockSpec(meta.shape, lambda i: (0, 0), memory_space=pltpu.SMEM),
                  pl.BlockSpec((tm, d), lambda i: (i, 0))],
        out_specs=pl.BlockSpec(memory_space=pl.ANY),
        out_shape=jax.ShapeDtypeStruct((n_rows, d), x1.dtype),
        scratch_shapes=[pltpu.VMEM((bm, d), x1.dtype), pltpu.SemaphoreType.DMA(()), pltpu.SemaphoreType.DMA(())],
        compiler_params=_cparams(("arbitrary",)),
        name="dispatch",
    )(dest, meta, x1)


def _experts_kernel(blke_ref, nused_ref, segend_ref, xs_ref, wu_ref, bu_ref, wd_ref, bd_ref, ys_ref,
                    wu32, wd32, wu16, wd16, sems):
    i = pl.program_id(0)
    bm = xs_ref.shape[0]
    dff = wd16.shape[0]
    n_used = nused_ref[0]

    def weight_copies(e):
        return (pltpu.make_async_copy(wu_ref.at[e], wu32, sems.at[0]),
                pltpu.make_async_copy(wd_ref.at[e], wd32, sems.at[1]))

    @pl.when(i < n_used)
    def _():
        e_cur = blke_ref[i]

        @pl.when(jnp.logical_or(i == 0, e_cur != blke_ref[jnp.maximum(i - 1, 0)]))
        def _():
            @pl.when(i == 0)
            def _():
                for copy in weight_copies(e_cur):
                    copy.start()

            for copy in weight_copies(e_cur):
                copy.wait()
            wu16[...] = wu32[...].astype(BF16)
            wd16[...] = wd32[...].astype(BF16)

            nxt = segend_ref[e_cur] // bm

            @pl.when(nxt < n_used)
            def _():
                for copy in weight_copies(blke_ref[jnp.minimum(nxt, blke_ref.shape[0] - 1)]):
                    copy.start()

        h = jnp.dot(xs_ref[...].astype(BF16), wu16[...], preferred_element_type=F32) + bu_ref[0]
        glu = jnp.minimum(h[:, :dff], SWIGLU_LIMIT)
        lin = jnp.clip(h[:, dff:], -SWIGLU_LIMIT, SWIGLU_LIMIT)
        act = glu * _sigmoid(SWIGLU_ALPHA * glu) * (lin + 1.0)
        ys_ref[...] = jnp.dot(act.astype(BF16), wd16[...], preferred_element_type=F32) + bd_ref[0]

    @pl.when(i >= nused_ref[0])
    def _():
        ys_ref[...] = jnp.zeros_like(ys_ref)


def _experts(blk_e, n_used, seg_end, xs, wu, bu, wd, bd, *, bm):
    n_rows, d = xs.shape
    ne, _, dff2 = wu.shape
    dff = wd.shape[1]
    used = lambda i, nu: jnp.minimum(i, nu[0] - 1)
    grid_spec = pltpu.PrefetchScalarGridSpec(
        num_scalar_prefetch=3,
        grid=(n_rows // bm,),
        in_specs=[pl.BlockSpec((bm, d), lambda i, e, nu, se: (used(i, nu), 0)),
                  pl.BlockSpec(memory_space=pl.ANY),
                  pl.BlockSpec((1, 1, dff2), lambda i, e, nu, se: (e[i], 0, 0)),
                  pl.BlockSpec(memory_space=pl.ANY),
                  pl.BlockSpec((1, 1, d), lambda i, e, nu, se: (e[i], 0, 0))],
        out_specs=pl.BlockSpec((bm, d), lambda i, e, nu, se: (i, 0)),
        scratch_shapes=[pltpu.VMEM((d, dff2), F32), pltpu.VMEM((dff, d), F32),
                        pltpu.VMEM((d, dff2), BF16), pltpu.VMEM((dff, d), BF16),
                        pltpu.SemaphoreType.DMA((2,))],
    )
    return pl.pallas_call(
        _experts_kernel,
        grid_spec=grid_spec,
        out_shape=jax.ShapeDtypeStruct((n_rows, d), F32),
        compiler_params=_cparams(("arbitrary",)),
        name="experts",
    )(blk_e, n_used, seg_end, xs, wu, bu.reshape(ne, 1, dff2), wd, bd.reshape(ne, 1, d))


def _combine_kernel(dest_ref, ys_ref, gate_ref, x1_ref, g2_ref, b2_ref, outp_ref, outs_ref, buf, sem, *, alpha, nt_p):
    i = pl.program_id(0)
    tm = x1_ref.shape[0]

    def start(t, carry):
        for k in range(TOP_K):
            pltpu.make_async_copy(ys_ref.at[pl.ds(dest_ref[0, k, t], 1), :], buf.at[k, pl.ds(t, 1), :], sem).start()
        return carry
    lax.fori_loop(0, tm, start, 0, unroll=ROW_COPY_UNROLL)

    for k in range(TOP_K):
        pltpu.make_async_copy(ys_ref.at[pl.ds(0, tm), :], buf.at[k], sem).wait()

    gate = gate_ref[...].T
    y = buf[0] * gate[:, 0:1]
    for k in range(1, TOP_K):
        y = y + buf[k] * gate[:, k:k + 1]
    out = _layer_norm(alpha * x1_ref[...] + y, g2_ref[...], b2_ref[...])

    @pl.when(i < nt_p)
    def _():
        outp_ref[...] = out

    @pl.when(i >= nt_p)
    def _():
        outs_ref[...] = out


def _combine(dest, ys, gate, x1, g2, b2, *, n_p, tm, alpha):
    n, d = x1.shape
    nt_p = n_p // tm
    return pl.pallas_call(
        functools.partial(_combine_kernel, alpha=alpha, nt_p=nt_p),
        grid=(n // tm,),
        in_specs=[pl.BlockSpec((1, TOP_K, tm), lambda i: (i, 0, 0), memory_space=pltpu.SMEM),
                  pl.BlockSpec(memory_space=pl.ANY),
                  pl.BlockSpec((SUBLANES, tm), lambda i: (0, i)),
                  pl.BlockSpec((tm, d), lambda i: (i, 0)),
                  pl.BlockSpec((1, d), lambda i: (0, 0)),
                  pl.BlockSpec((1, d), lambda i: (0, 0))],
        out_specs=[pl.BlockSpec((tm, d), lambda i: (jnp.minimum(i, nt_p - 1), 0)),
                   pl.BlockSpec((tm, d), lambda i: (jnp.maximum(i - nt_p, 0), 0))],
        out_shape=[jax.ShapeDtypeStruct((n_p, d), F32), jax.ShapeDtypeStruct((n - n_p, d), F32)],
        scratch_shapes=[pltpu.VMEM((TOP_K, tm, d), F32), pltpu.SemaphoreType.DMA(())],
        compiler_params=_cparams(("arbitrary",)),
        name="combine",
    )(dest, ys, gate, x1, g2, b2)


PROJ_TM = 256
HGRN_TL = 256
DSA_TQ = 256
MIX_TM = 256
ROUTE_TL = 1280
MOE_TM = 256
MOE_BM = 512


def kernel(x_prompt, x_sample, cache_k, cache_v, cache_kidx, state_hgrn, w_in, lb_logits, gn_a, w_oa, w_ob,
           w_out, ln1_g, ln1_b, w_router, b_router, w_up, b_up, w_down, b_down, ln2_g, ln2_b):
    depth = w_in.shape[0]
    assert depth == 1
    bsz, seq, d = x_prompt.shape
    dbsz, dseq, _ = x_sample.shape
    past = cache_k.shape[2]
    n_p, n_s = bsz * seq, dbsz * dseq
    n = n_p + n_s
    alpha = (2 * depth) ** 0.25

    sizes = (H_A * DK_A, H_A * DK_A, H_A * DV_A, H_A * DV_A, H_B * DH_B, DH_B, DH_B, H_I * D_I, H_I, D_I, d, d)
    offs = np.concatenate([[0], np.cumsum(sizes)])
    grp = lambda g: w_in[0, :, offs[g]:offs[g + 1]].astype(BF16)
    wqa, wfa, wia, wg, wqb, wkb, wvb, wqi, wwi, wki, wgta, wgtb = (grp(g) for g in range(12))
    wsm = jnp.concatenate([wkb, wvb, wqi, wki, wwi, jnp.zeros((d, SM_WIDTH - SM_USED), BF16)], axis=1)

    x_groups = (x_prompt.reshape(n_p, d), x_sample.reshape(n_s, d))
    qa, f, ia, sg, qb, sm = _proj(*x_groups, wqa, wfa, wia, wg, wqb, wsm, lb_logits, PROJ_TM)

    gn = gn_a[0].reshape(1, DV_A)
    oa_p, sfin_p = _hgrn(qa, f, ia, sg, gn, jnp.zeros((bsz, H_A, DK_A, DV_A), F32),
                         bsz=bsz, seq=seq, row0=0, tl=HGRN_TL)
    oa_s, sfin_s = _hgrn(qa, f, ia, sg, gn, state_hgrn[0], bsz=dbsz, seq=dseq, row0=n_p, tl=dseq)

    kb_p = sm[:n_p, SM_KB:SM_KB + DH_B].reshape(bsz, seq, DH_B)
    vb_p = sm[:n_p, SM_VB:SM_VB + DH_B].reshape(bsz, seq, DH_B)
    ki_p = sm[:n_p, SM_KI:SM_KI + D_I].reshape(bsz, seq, D_I)
    kb_s = sm[n_p:, SM_KB:SM_KB + DH_B].reshape(dbsz, dseq, DH_B)
    vb_s = sm[n_p:, SM_VB:SM_VB + DH_B].reshape(dbsz, dseq, DH_B)
    ki_s = sm[n_p:, SM_KI:SM_KI + D_I].reshape(dbsz, dseq, D_I)
    lim_p = (np.arange(seq) // CHUNK + 1) * CHUNK
    ob_p = _dsa(qb, sm, [(kb_p.astype(BF16), vb_p.astype(BF16), ki_p.astype(BF16))], lim_p,
                bsz=bsz, seq=seq, row0=0, tq=DSA_TQ, n_sel=min(TOPK_MAX, seq // 4))
    n_keys = past + dseq
    new_pad = -(-dseq // LANES) * LANES - dseq
    new_keys = tuple(jnp.pad(a.astype(BF16), ((0, 0), (0, new_pad), (0, 0))) for a in (kb_s, vb_s, ki_s))
    lim_s = np.full((dseq,), n_keys)
    ob_s = _dsa(qb, sm, [(cache_k[0], cache_v[0], cache_kidx[0]), new_keys], lim_s,
                bsz=dbsz, seq=dseq, row0=n_p, tq=dseq, n_sel=min(TOPK_MAX, n_keys // 4))

    wrh, wrl = _split_bf16(w_router[0].T)
    x1, eidx, gate, counts = _mix(
        (oa_p, oa_s), (ob_p, ob_s), x_groups, w_oa[0].astype(BF16), w_ob[0].astype(BF16), wgta, wgtb, w_out[0].astype(BF16),
        ln1_g[0].reshape(1, d), ln1_b[0].reshape(1, d), wrh, wrl, b_router[0].reshape(N_EXPERTS, 1),
        tm=MIX_TM, alpha=alpha)

    n_rows = -(-(n * TOP_K + N_EXPERTS * (MOE_BM - 1)) // MOE_BM) * MOE_BM
    dest, blk_e, meta = _route(eidx, counts, tl=ROUTE_TL, sub=MOE_TM, bm=MOE_BM, nb=n_rows // MOE_BM)
    xs = _dispatch(dest, meta, x1, n_rows, tm=MOE_TM, bm=MOE_BM, ne=N_EXPERTS)
    ys = _experts(blk_e[0, :n_rows // MOE_BM], meta[1, :1], meta[0, :N_EXPERTS], xs, w_up[0], b_up[0], w_down[0],
                  b_down[0], bm=MOE_BM)
    out_p, out_s = _combine(dest, ys, gate, x1, ln2_g[0].reshape(1, d), ln2_b[0].reshape(1, d),
                            n_p=n_p, tm=MOE_TM, alpha=alpha)

    return (out_p.reshape(bsz, seq, d), out_s.reshape(dbsz, dseq, d),
            kb_p[None], vb_p[None], ki_p[None], sfin_p[None],
            kb_s[None], vb_s[None], ki_s[None], sfin_s[None])
```

```python
import functools

import jax
import jax.numpy as jnp
import numpy as np
from jax import lax
from jax.experimental import pallas as pl
from jax.experimental.pallas import tpu as pltpu

F32 = jnp.float32
BF16 = jnp.bfloat16
I32 = jnp.int32

CHUNK = 64
H_A = 8
DK_A = 128
DV_A = 128
HGRN_BLOCK = 16
H_B = 8
DH_B = 128
H_I = 8
D_I = 64
TOPK_MAX = 256
N_EXPERTS = 32
TOP_K = 4
SWIGLU_ALPHA = 1.702
SWIGLU_LIMIT = 7.0
EPS = 1e-5

LANES = 128
SUBLANES = 8
MXU_ROWS = 256
VMEM_LIMIT_BYTES = 56 * 1024 * 1024

INT_MIN = -(2 ** 31)


def _cparams(sem):
    return pltpu.CompilerParams(dimension_semantics=sem, vmem_limit_bytes=VMEM_LIMIT_BYTES)


def _sigmoid(x):
    return 1.0 / (1.0 + jnp.exp(-x))


def _full_spec(shape):
    nd = len(shape)
    return pl.BlockSpec(shape, lambda *_: (0,) * nd)


def _layer_norm(x, g, b):
    mu = jnp.mean(x, axis=-1, keepdims=True)
    xc = x - mu
    var = jnp.mean(xc * xc, axis=-1, keepdims=True)
    return xc * lax.rsqrt(var + EPS) * g + b


def _two_group_specs(tm, width, nt_p):
    return (pl.BlockSpec((tm, width), lambda i: (jnp.minimum(i, nt_p - 1), 0)),
            pl.BlockSpec((tm, width), lambda i: (jnp.maximum(i - nt_p, 0), 0)))


def _group_tile(p_ref, s_ref, nt_p):
    return jnp.where(pl.program_id(0) < nt_p, p_ref[...], s_ref[...])


def _proj_kernel(xp_ref, xs_ref, wqa_ref, wfa_ref, wia_ref, wga_ref, wqb_ref, wsm_ref, lbl_ref,
                 qa_ref, f_ref, ia_ref, sg_ref, qb_ref, sm_ref, *, nt_p):
    xb = _group_tile(xp_ref, xs_ref, nt_p).astype(BF16)
    dot = lambda w_ref: jnp.dot(xb, w_ref[...], preferred_element_type=F32)
    qa_ref[...] = dot(wqa_ref)
    lbl = lbl_ref[...]
    lbe = jnp.exp(lbl - jnp.max(lbl, axis=0, keepdims=True))
    lb = lbe[0:1, :] / jnp.sum(lbe, axis=0, keepdims=True)
    f_ref[...] = lb + (1.0 - lb) * _sigmoid(dot(wfa_ref))
    ia_ref[...] = dot(wia_ref)
    g = dot(wga_ref)
    sg_ref[...] = g * _sigmoid(g)
    qb_ref[...] = dot(wqb_ref).astype(BF16)
    sm_ref[...] = dot(wsm_ref)


def _proj(x_p, x_s, wqa, wfa, wia, wga, wqb, wsm, lb_logits, tm):
    d = x_p.shape[1]
    n = x_p.shape[0] + x_s.shape[0]
    nt_p = x_p.shape[0] // tm
    nsm = wsm.shape[1]
    row = lambda w: pl.BlockSpec((tm, w), lambda i: (i, 0))
    wspec = lambda w: pl.BlockSpec(w.shape, lambda i: (0, 0))
    return pl.pallas_call(
        functools.partial(_proj_kernel, nt_p=nt_p),
        grid=(n // tm,),
        in_specs=[*_two_group_specs(tm, d, nt_p), wspec(wqa), wspec(wfa), wspec(wia), wspec(wga), wspec(wqb),
                  wspec(wsm), wspec(lb_logits)],
        out_specs=[row(d), row(d), row(d), row(d), row(d), row(nsm)],
        out_shape=[jax.ShapeDtypeStruct((n, d), F32)] * 4
        + [jax.ShapeDtypeStruct((n, d), BF16), jax.ShapeDtypeStruct((n, nsm), F32)],
        compiler_params=_cparams(("arbitrary",)),
        name="proj",
    )(x_p, x_s, wqa, wfa, wia, wga, wqb, wsm, lb_logits)


HGRN_UNROLL = 8


def _block_cumsum(a, row_in_block):
    s = 1
    while s < HGRN_BLOCK:
        a = a + jnp.where(row_in_block >= s, pltpu.roll(a, s, 0), 0.0)
        s *= 2
    return a


def _hgrn_kernel(qa_ref, f_ref, ia_ref, sg_ref, gn_ref, s0_ref, o_ref, sfin_ref,
                 s_scr, qd_scr, qm_scr, km_scr, kl_scr, v_scr, dl_scr, o_scr):
    j = pl.program_id(1)
    tl = qa_ref.shape[0]
    nblk = tl // HGRN_BLOCK
    half = HGRN_BLOCK // 2

    @pl.when(j == 0)
    def _():
        for h in range(H_A):
            s_scr[h] = s0_ref[0, h].T

    f = f_ref[...]
    q = qa_ref[...]
    d = f.shape[1]
    row_in_block = lax.broadcasted_iota(I32, (tl, d), 0) % HGRN_BLOCK
    cum = _block_cumsum(jnp.log(f), row_in_block)
    cum3 = cum.reshape(nblk, HGRN_BLOCK, d)
    mid = jnp.broadcast_to(cum3[:, half:half + 1, :], cum3.shape).reshape(tl, d)
    last3 = cum3[:, HGRN_BLOCK - 1:HGRN_BLOCK, :]
    last = jnp.broadcast_to(last3, cum3.shape).reshape(tl, d)
    k = 1.0 - f
    qd_scr[...] = (q * jnp.exp(cum)).astype(BF16)
    qm_scr[...] = (q * jnp.exp(cum - mid)).astype(BF16)
    km_scr[...] = (k * jnp.exp(mid - cum)).astype(BF16)
    kl_scr[...] = (k * jnp.exp(last - cum)).astype(BF16)
    dl_scr[...] = jnp.exp(last3.reshape(nblk, d))
    v_scr[...] = ia_ref[...].astype(BF16)

    nt = (((1,), (1,)), ((), ()))
    ri = lax.broadcasted_iota(I32, (tl, tl), 0)
    ci = lax.broadcasted_iota(I32, (tl, tl), 1)
    keep = (ri // HGRN_BLOCK == ci // HGRN_BLOCK) & (ri >= ci)
    for h in range(H_A):
        cols = slice(h * DK_A, (h + 1) * DK_A)
        vcols = slice(h * DV_A, (h + 1) * DV_A)
        att = lax.dot_general(qm_scr[:, cols], km_scr[:, cols], nt, preferred_element_type=F32)
        att = jnp.where(keep, att, 0.0).astype(BF16)
        o_scr[:, vcols] = jnp.dot(att, v_scr[:, vcols], preferred_element_type=F32)

    def block(b, carry):
        r0 = pl.multiple_of(b * HGRN_BLOCK, HGRN_BLOCK)
        rows = pl.ds(r0, HGRN_BLOCK)
        dl = dl_scr[pl.ds(b, 1), :]
        for h in range(H_A):
            cols = slice(h * DK_A, (h + 1) * DK_A)
            vcols = slice(h * DV_A, (h + 1) * DV_A)
            st_h = s_scr[h]
            o_scr[rows, vcols] += lax.dot_general(qd_scr[rows, cols], st_h.astype(BF16), nt,
                                                  preferred_element_type=F32)
            upd_t = lax.dot_general(v_scr[rows, vcols], kl_scr[rows, cols], (((0,), (0,)), ((), ())),
                                    preferred_element_type=F32)
            s_scr[h] = dl[:, cols] * st_h + upd_t
        return carry

    lax.fori_loop(0, nblk, block, 0, unroll=HGRN_UNROLL if nblk % HGRN_UNROLL == 0 else 1)

    gn = gn_ref[...]
    for h in range(H_A):
        vcols = slice(h * DV_A, (h + 1) * DV_A)
        o = o_scr[:, vcols]
        o = o * lax.rsqrt(jnp.mean(o * o, axis=-1, keepdims=True) + EPS) * gn
        o_ref[:, vcols] = (o * sg_ref[:, vcols]).astype(BF16)

    @pl.when(j == pl.num_programs(1) - 1)
    def _():
        for h in range(H_A):
            sfin_ref[0, h] = s_scr[h].T


def _hgrn(qa, f, ia, sg, gn, s0, *, bsz, seq, row0, tl):
    d = qa.shape[1]
    nt = seq // tl
    blk0 = row0 // tl
    row = pl.BlockSpec((tl, d), lambda b, j: (blk0 + b * nt + j, 0))
    orow = pl.BlockSpec((tl, d), lambda b, j: (b * nt + j, 0))
    sspec = pl.BlockSpec((1, H_A, DK_A, DV_A), lambda b, j: (b, 0, 0, 0))
    return pl.pallas_call(
        _hgrn_kernel,
        grid=(bsz, nt),
        in_specs=[row, row, row, row, pl.BlockSpec((1, DV_A), lambda b, j: (0, 0)), sspec],
        out_specs=[orow, sspec],
        out_shape=[jax.ShapeDtypeStruct((bsz * seq, d), BF16),
                   jax.ShapeDtypeStruct((bsz, H_A, DK_A, DV_A), F32)],
        scratch_shapes=[pltpu.VMEM((H_A, DK_A, DV_A), F32)]
        + [pltpu.VMEM((tl, d), BF16)] * 5
        + [pltpu.VMEM((tl // HGRN_BLOCK, d), F32), pltpu.VMEM((tl, d), F32)],
        compiler_params=_cparams(("arbitrary", "arbitrary")),
        name="hgrn",
    )(qa, f, ia, sg, gn, s0)


SM_KB, SM_VB, SM_QI, SM_KI, SM_WI = 0, DH_B, 2 * DH_B, 2 * DH_B + H_I * D_I, 2 * DH_B + H_I * D_I + D_I
SM_USED = SM_WI + H_I
SM_WIDTH = -(-SM_USED // LANES) * LANES
INDEX_SCALE = (H_I * D_I) ** -0.5
NEG_INF = float("-inf")
LOG2_E = 1.4426950408889634


F32_EXP_MASK = 0x7F800000
F32_MIN_NORMAL = 0x00800000
SEARCH_UNROLL = 4


def _key_to_float(u):
    key = u ^ jnp.int32(INT_MIN)
    bits = jnp.where(key < 0, key ^ jnp.int32(0x7FFFFFFF), key)
    below_neg_inf = (bits < 0) & ((bits & jnp.int32(0x7FFFFFFF)) > jnp.int32(F32_EXP_MASK))
    return jnp.where(below_neg_inf, NEG_INF, lax.bitcast_convert_type(bits, F32))


def _positive_bits_to_float(bits):
    return jnp.where(bits < jnp.int32(F32_MIN_NORMAL), 0.0, lax.bitcast_convert_type(bits, F32))


def _count(mask):
    return jnp.sum(jnp.where(mask, 1.0, 0.0), axis=1, keepdims=True)


def _select_bias(score_scr, bias_scr, adm, s, n_sel):
    tq = score_scr.shape[0]

    def resolve_bit(i, t_u, cnt_t):
        cand_u = t_u | lax.shift_left(jnp.int32(1), 31 - i)
        cnt = _count(score_scr[:, :s] >= _key_to_float(cand_u))
        ok = cnt >= n_sel
        return jnp.where(ok, cand_u, t_u), jnp.where(ok, cnt, cnt_t)

    def unresolved(c):
        i, _, cnt_t = c
        return jnp.logical_and(i < 32, jnp.any(cnt_t != n_sel))

    def resolve_bits(c):
        i, t_u, cnt_t = c
        for b in range(SEARCH_UNROLL):
            t_u, cnt_t = resolve_bit(i + b, t_u, cnt_t)
        return i + SEARCH_UNROLL, t_u, cnt_t

    _, t_u, cnt_t = lax.while_loop(unresolved, resolve_bits,
                                   (jnp.int32(0), jnp.zeros((tq, 1), I32), jnp.full((tq, 1), float(s), F32)))
    t_f = _key_to_float(t_u)
    score = score_scr[:, :s]
    bias_scr[:, :s] = jnp.where((score >= t_f) & adm, 0.0, NEG_INF)

    @pl.when(jnp.any(cnt_t > n_sel))
    def _():
        above = score >= _key_to_float(t_u + 1)
        bucket = (score >= t_f) & jnp.logical_not(above)
        need = n_sel - _count(above)
        off = jnp.where(bucket, score - jnp.where(t_f == NEG_INF, 0.0, t_f), -1.0)

        def resolve_offset_bit(i, r_bits):
            cand = r_bits | lax.shift_left(jnp.int32(1), 30 - i)
            ok = _count(off >= _positive_bits_to_float(cand)) >= need
            return jnp.where(ok, cand, r_bits)

        ranked = jnp.any(off > 0.0)
        r_bits = lax.fori_loop(0, jnp.where(ranked, 31, 0), resolve_offset_bit,
                               jnp.full((tq, 1), jnp.where(ranked, 0, F32_MIN_NORMAL - 1), I32))
        above2 = off >= _positive_bits_to_float(r_bits + 1)
        tie = (off >= _positive_bits_to_float(r_bits)) & jnp.logical_not(above2)
        need2 = need - _count(above2)
        upper = (lax.broadcasted_iota(I32, (LANES, LANES), 0)
                 <= lax.broadcasted_iota(I32, (LANES, LANES), 1)).astype(BF16)
        carry = jnp.zeros((tq, 1), F32)
        for c in range(s // LANES):
            cs = slice(c * LANES, (c + 1) * LANES)
            rank = jnp.dot(jnp.where(tie[:, cs], 1.0, 0.0).astype(BF16), upper, preferred_element_type=F32) + carry
            sel = (above[:, cs] | above2[:, cs] | (tie[:, cs] & (rank <= need2))) & adm[:, cs]
            bias_scr[:, cs] = jnp.where(sel, 0.0, NEG_INF)
            carry = rank[:, LANES - 1:LANES]


def _dsa_tile(qb_ref, sm_ref, *refs, segs, search, n_sel):
    key_refs, (lim_ref, o_ref, score_scr, bias_scr) = refs[:3 * len(segs)], refs[3 * len(segs):]
    tq = qb_ref.shape[0]
    s = sum(segs)
    nt = (((1,), (1,)), ((), ()))
    adm = lax.broadcasted_iota(I32, (tq, s), 1) < lim_ref[...]
    seg_keys = [[key_refs[3 * g + j][0, :sg, :].astype(BF16) for j in range(3)] for g, sg in enumerate(segs)]
    starts = [sum(segs[:g]) for g in range(len(segs))]
    against_keys = lambda lhs, j: jnp.concatenate(
        [lax.dot_general(lhs, kv[j], nt, preferred_element_type=F32) for kv in seg_keys], axis=1)
    times_values = lambda p: sum(jnp.dot(p[:, st:st + sg], kv[1], preferred_element_type=F32)
                                 for st, sg, kv in zip(starts, segs, seg_keys))
    stack = tq * max(H_I, H_B) <= MXU_ROWS
    head_rows = lambda a, h: a[h * tq:(h + 1) * tq]
    qi_head = lambda h: sm_ref[:, SM_QI + h * D_I:SM_QI + (h + 1) * D_I].astype(BF16)
    q_head = lambda h: qb_ref[:, h * DH_B:(h + 1) * DH_B]

    if search:
        if stack:
            stacked = against_keys(jnp.concatenate([qi_head(h) for h in range(H_I)], axis=0), 2)
        score = jnp.zeros((tq, s), F32)
        for h in range(H_I):
            sh = head_rows(stacked, h) if stack else against_keys(qi_head(h), 2)
            wih = sm_ref[:, SM_WI + h:SM_WI + h + 1] * INDEX_SCALE
            score = score + jnp.maximum(sh, 0.0) * wih
        score_scr[:, :s] = jnp.where(adm, score, NEG_INF)
        _select_bias(score_scr, bias_scr, adm, s, n_sel)
    else:
        bias_scr[:, :s] = jnp.where(adm, 0.0, NEG_INF)

    def softmax_terms(logits):
        p = jnp.exp2((logits - jnp.max(logits, axis=1, keepdims=True)) * (DH_B ** -0.5 * LOG2_E))
        return p.astype(BF16), jnp.sum(p, axis=1, keepdims=True)

    if stack:
        stacked = against_keys(jnp.concatenate([q_head(h) for h in range(H_B)], axis=0), 0)
        terms = [softmax_terms(head_rows(stacked, h) + bias_scr[:, :s]) for h in range(H_B)]
        o_all = times_values(jnp.concatenate([p for p, _ in terms], axis=0))
        for h in range(H_B):
            o_ref[:, h * DH_B:(h + 1) * DH_B] = (head_rows(o_all, h) / terms[h][1]).astype(BF16)
    else:
        for h in range(H_B):
            p, denom = softmax_terms(against_keys(q_head(h), 0) + bias_scr[:, :s])
            o_ref[:, h * DH_B:(h + 1) * DH_B] = (times_values(p) / denom).astype(BF16)


DSA_KEY_GRAN = 256


def _dsa(qb, sm, key_segments, limits, *, bsz, seq, row0, tq, n_sel):
    d = qb.shape[1]
    seg_rows = [seg[0].shape[1] for seg in key_segments]
    n_keys = sum(seg_rows)
    nt = seq // tq
    blk0 = row0 // tq
    limits = np.asarray(limits, np.int32)
    tile_max = limits.reshape(nt, tq).max(axis=1)
    per_tile = [(int(min(n_keys, -(-m // DSA_KEY_GRAN) * DSA_KEY_GRAN)), bool(m > n_sel)) for m in tile_max]
    lim2d = jnp.asarray(limits.reshape(seq, 1))
    outs, lo = [], 0
    while lo < nt:
        hi = lo
        while hi + 1 < nt and per_tile[hi + 1] == per_tile[lo]:
            hi += 1
        s, search = per_tile[lo]
        assert s == n_keys or len(key_segments) == 1
        segs = tuple(seg_rows) if s == n_keys else (s,)
        ntv = hi - lo + 1
        row = lambda w, lo=lo: pl.BlockSpec((tq, w), lambda b, j: (blk0 + b * nt + lo + j, 0))
        keys = lambda rows, w: pl.BlockSpec((1, rows, w), lambda b, j: (b, 0, 0))
        out = pl.pallas_call(
            functools.partial(_dsa_tile, segs=segs, search=search, n_sel=n_sel),
            grid=(bsz, ntv),
            in_specs=[row(d), row(sm.shape[1])]
            + [keys(rows, a.shape[2]) for rows, seg in zip(segs, key_segments) for a in seg]
            + [pl.BlockSpec((tq, 1), lambda b, j, lo=lo: (lo + j, 0))],
            out_specs=pl.BlockSpec((tq, d), lambda b, j, ntv=ntv: (b * ntv + j, 0)),
            out_shape=jax.ShapeDtypeStruct((bsz * ntv * tq, d), BF16),
            scratch_shapes=[pltpu.VMEM((tq, s), F32), pltpu.VMEM((tq, s), F32)],
            compiler_params=_cparams(("arbitrary", "arbitrary")),
            name="dsa",
        )(qb, sm, *[a for seg in key_segments for a in seg], lim2d)
        outs.append(out.reshape(bsz, ntv * tq, d))
        lo = hi + 1
    return jnp.concatenate(outs, axis=1).reshape(bsz * seq, d)


def _split_bf16(a):
    hi = a.astype(BF16)
    return hi, (a - hi.astype(F32)).astype(BF16)


def _mix_kernel(oap_ref, oas_ref, obp_ref, obs_ref, xp_ref, xs_ref, woa_ref, wob_ref, wga_ref, wgb_ref, wout_ref,
                g1_ref, b1_ref, wrh_ref, wrl_ref, br_ref, x1_ref, eidx_ref, gate_ref, cnt_ref, *, alpha, nt_p):
    i = pl.program_id(0)
    x = _group_tile(xp_ref, xs_ref, nt_p)
    xb = x.astype(BF16)
    dot = lambda a, w_ref: jnp.dot(a, w_ref[...], preferred_element_type=F32)
    y_a = dot(_group_tile(oap_ref, oas_ref, nt_p), woa_ref)
    y_b = dot(_group_tile(obp_ref, obs_ref, nt_p), wob_ref)
    merged = _sigmoid(dot(xb, wga_ref)) * y_a + _sigmoid(dot(xb, wgb_ref)) * y_b
    mixed = dot(merged.astype(BF16), wout_ref)
    x1 = _layer_norm(alpha * x + mixed, g1_ref[...], b1_ref[...])
    x1_ref[...] = x1

    nt = (((1,), (1,)), ((), ()))
    xh, xl = _split_bf16(x1)
    wh, wl = wrh_ref[...], wrl_ref[...]
    logits = (lax.dot_general(wh, xh, nt, preferred_element_type=F32)
              + lax.dot_general(wh, xl, nt, preferred_element_type=F32)
              + lax.dot_general(wl, xh, nt, preferred_element_type=F32)) + br_ref[...]
    ne, tm = logits.shape
    erow = lax.broadcasted_iota(I32, (ne, tm), 0)
    vals, idxs = [], []
    for _ in range(TOP_K):
        m = jnp.max(logits, axis=0, keepdims=True)
        idx = jnp.min(jnp.where(logits == m, erow, ne), axis=0, keepdims=True)
        vals.append(m)
        idxs.append(idx)
        logits = jnp.where(erow == idx, NEG_INF, logits)
    ex = [jnp.exp(v - vals[0]) for v in vals]
    denom = ex[0] + ex[1] + ex[2] + ex[3]
    pad = SUBLANES - TOP_K
    eidx = jnp.concatenate(idxs + [jnp.zeros((pad, tm), I32)], axis=0)
    eidx_ref[...] = eidx
    gate_ref[...] = jnp.concatenate([e / denom for e in ex] + [jnp.zeros((pad, tm), F32)], axis=0)

    onehot = jnp.zeros((ne, tm), F32)
    for idx in idxs:
        onehot = onehot + jnp.where(erow == idx, 1.0, 0.0)
    tile_cnt = jnp.broadcast_to(jnp.sum(onehot, axis=1, keepdims=True), cnt_ref.shape)

    @pl.when(i == 0)
    def _():
        cnt_ref[...] = tile_cnt

    @pl.when(i > 0)
    def _():
        cnt_ref[...] = cnt_ref[...] + tile_cnt


def _mix(oa, ob, x, woa, wob, wga, wgb, wout, g1, b1, wrh, wrl, br, *, tm, alpha):
    d = x[0].shape[1]
    n = x[0].shape[0] + x[1].shape[0]
    nt_p = x[0].shape[0] // tm
    ne = wrh.shape[0]
    row = lambda w: pl.BlockSpec((tm, w), lambda i: (i, 0))
    col = pl.BlockSpec((SUBLANES, tm), lambda i: (0, i))
    full = lambda a: pl.BlockSpec(a.shape, lambda i: (0,) * a.ndim)
    return pl.pallas_call(
        functools.partial(_mix_kernel, alpha=alpha, nt_p=nt_p),
        grid=(n // tm,),
        in_specs=[*_two_group_specs(tm, d, nt_p)] * 3
        + [full(a) for a in (woa, wob, wga, wgb, wout, g1, b1, wrh, wrl, br)],
        out_specs=[row(d), col, col, pl.BlockSpec((ne, LANES), lambda i: (0, 0))],
        out_shape=[jax.ShapeDtypeStruct((n, d), F32), jax.ShapeDtypeStruct((SUBLANES, n), I32),
                   jax.ShapeDtypeStruct((SUBLANES, n), F32), jax.ShapeDtypeStruct((ne, LANES), F32)],
        compiler_params=_cparams(("arbitrary",)),
        name="mix",
    )(*oa, *ob, *x, woa, wob, wga, wgb, wout, g1, b1, wrh, wrl, br)


def _sublane_cumsum(a):
    n = a.shape[0]
    row = lax.broadcasted_iota(I32, a.shape, 0)
    s = 1
    while s < n:
        a = a + jnp.where(row >= s, pltpu.roll(a, s, 0), 0.0)
        s *= 2
    return a


def _route_kernel(eidx_ref, cnt_ref, dest_ref, blke_ref, meta_ref, carry_scr, *, bm, sub):
    i = pl.program_id(0)
    ne = cnt_ref.shape[0]
    tl = eidx_ref.shape[1]

    @pl.when(i == 0)
    def _():
        carry_scr[...] = jnp.zeros_like(carry_scr)

    counts = cnt_ref[...]
    padded = jnp.ceil(counts / bm) * bm
    pend = _sublane_cumsum(padded)
    pstart = (pend - padded)[:, 0:1]

    @pl.when(i == 0)
    def _():
        nb = blke_ref.shape[1]
        first_row = (lax.broadcasted_iota(I32, (ne, nb), 1) * bm).astype(F32)
        below = jnp.sum(jnp.where(pend[:, 0:1] <= first_row, 1.0, 0.0), axis=0, keepdims=True)
        blke_ref[...] = jnp.minimum(below, ne - 1.0).astype(I32)
        pend_lanes = jnp.concatenate([pend, jnp.zeros((LANES - ne, LANES), F32)], axis=0).T[0:1, :]
        n_used = jnp.max(pend, axis=0, keepdims=True) / bm
        meta_ref[...] = jnp.concatenate([pend_lanes, n_used, jnp.zeros((SUBLANES - 2, LANES), F32)],
                                        axis=0).astype(I32)

    eidx = eidx_ref[...]
    erow = lax.broadcasted_iota(I32, (ne, tl), 0)
    hot = [jnp.where(erow == eidx[k:k + 1, :], 1.0, 0.0) for k in range(TOP_K)]
    onehot = hot[0] + hot[1] + hot[2] + hot[3]
    before = (lax.broadcasted_iota(I32, (tl, tl), 0) < lax.broadcasted_iota(I32, (tl, tl), 1)).astype(BF16)
    base = jnp.dot(onehot.astype(BF16), before, preferred_element_type=F32) + carry_scr[:, 0:1] + pstart
    dest = jnp.concatenate([jnp.sum(hk * base, axis=0, keepdims=True) for hk in hot], axis=0).astype(I32)
    for c in range(tl // sub):
        dest_ref[c] = dest[:, c * sub:(c + 1) * sub]
    carry_scr[...] = carry_scr[...] + jnp.sum(onehot, axis=1, keepdims=True)


def _route(eidx, counts, *, tl, sub, bm, nb):
    n = eidx.shape[1]
    ne = counts.shape[0]
    nb_pad = -(-nb // LANES) * LANES
    return pl.pallas_call(
        functools.partial(_route_kernel, bm=bm, sub=sub),
        grid=(n // tl,),
        in_specs=[pl.BlockSpec((SUBLANES, tl), lambda i: (0, i)), pl.BlockSpec(counts.shape, lambda i: (0, 0))],
        out_specs=[pl.BlockSpec((tl // sub, TOP_K, sub), lambda i: (i, 0, 0)),
                   pl.BlockSpec((1, nb_pad), lambda i: (0, 0)),
                   pl.BlockSpec((SUBLANES, LANES), lambda i: (0, 0))],
        out_shape=[jax.ShapeDtypeStruct((n // sub, TOP_K, sub), I32), jax.ShapeDtypeStruct((1, nb_pad), I32),
                   jax.ShapeDtypeStruct((SUBLANES, LANES), I32)],
        scratch_shapes=[pltpu.VMEM((ne, LANES), F32)],
        compiler_params=_cparams(("arbitrary",)),
        name="route",
    )(eidx, counts)


ROW_COPY_UNROLL = 4


def _dispatch_kernel(dest_ref, meta_ref, x_ref, xs_ref, zbuf, sem, zsem, *, bm, ne, nb):
    i = pl.program_id(0)
    tm = x_ref.shape[0]

    @pl.when(i == 0)
    def _():
        zbuf[...] = jnp.zeros_like(zbuf)
        zero_block = lambda row0: pltpu.make_async_copy(zbuf, xs_ref.at[pl.ds(row0, bm), :], zsem)

        def each_padding_block(action):
            for e in range(ne):
                seg_end = meta_ref[0, e]
                seg_start = meta_ref[0, e - 1] if e > 0 else 0

                @pl.when(seg_end > seg_start)
                def _(seg_end=seg_end):
                    action(zero_block(pl.multiple_of(seg_end - bm, bm)))

            def tail_block(b, carry):
                action(zero_block(pl.multiple_of(b * bm, bm)))
                return carry
            lax.fori_loop(meta_ref[1, 0], nb, tail_block, 0)

        each_padding_block(lambda copy: copy.start())
        each_padding_block(lambda copy: copy.wait())

    def start(t, carry):
        for k in range(TOP_K):
            pltpu.make_async_copy(x_ref.at[pl.ds(t, 1), :], xs_ref.at[pl.ds(dest_ref[0, k, t], 1), :], sem).start()
        return carry
    lax.fori_loop(0, tm, start, 0, unroll=ROW_COPY_UNROLL)

    for _ in range(TOP_K):
        pltpu.make_async_copy(x_ref, xs_ref.at[pl.ds(0, tm), :], sem).wait()


def _dispatch(dest, meta, x1, n_rows, *, tm, bm, ne):
    n, d = x1.shape
    return pl.pallas_call(
        functools.partial(_dispatch_kernel, bm=bm, ne=ne, nb=n_rows // bm),
        grid=(n // tm,),
        in_specs=[pl.BlockSpec((1, TOP_K, tm), lambda i: (i, 0, 0), memory_space=pltpu.SMEM),
                  pl.BlockSpec(meta.shape, lambda i: (0, 0), memory_space=pltpu.SMEM),
                  pl.BlockSpec((tm, d), lambda i: (i, 0))],
        out_specs=pl.BlockSpec(memory_space=pl.ANY),
        out_shape=jax.ShapeDtypeStruct((n_rows, d), x1.dtype),
        scratch_shapes=[pltpu.VMEM((bm, d), x1.dtype), pltpu.SemaphoreType.DMA(()), pltpu.SemaphoreType.DMA(())],
        compiler_params=_cparams(("arbitrary",)),
        name="dispatch",
    )(dest, meta, x1)


def _experts_kernel(blke_ref, nused_ref, segend_ref, xs_ref, wu_ref, bu_ref, wd_ref, bd_ref, ys_ref,
                    wu32, wd32, wu16, wd16, sems):
    i = pl.program_id(0)
    bm = xs_ref.shape[0]
    dff = wd16.shape[0]
    n_used = nused_ref[0]

    def weight_copies(e):
        return (pltpu.make_async_copy(wu_ref.at[e], wu32, sems.at[0]),
                pltpu.make_async_copy(wd_ref.at[e], wd32, sems.at[1]))

    @pl.when(i < n_used)
    def _():
        e_cur = blke_ref[i]

        @pl.when(jnp.logical_or(i == 0, e_cur != blke_ref[jnp.maximum(i - 1, 0)]))
        def _():
            @pl.when(i == 0)
            def _():
                for copy in weight_copies(e_cur):
                    copy.start()

            for copy in weight_copies(e_cur):
                copy.wait()
            wu16[...] = wu32[...].astype(BF16)
            wd16[...] = wd32[...].astype(BF16)

            nxt = segend_ref[e_cur] // bm

            @pl.when(nxt < n_used)
            def _():
                for copy in weight_copies(blke_ref[jnp.minimum(nxt, blke_ref.shape[0] - 1)]):
                    copy.start()

        h = jnp.dot(xs_ref[...].astype(BF16), wu16[...], preferred_element_type=F32) + bu_ref[0]
        glu = jnp.minimum(h[:, :dff], SWIGLU_LIMIT)
        lin = jnp.clip(h[:, dff:], -SWIGLU_LIMIT, SWIGLU_LIMIT)
        act = glu * _sigmoid(SWIGLU_ALPHA * glu) * (lin + 1.0)
        ys_ref[...] = jnp.dot(act.astype(BF16), wd16[...], preferred_element_type=F32) + bd_ref[0]

    @pl.when(i >= nused_ref[0])
    def _():
        ys_ref[...] = jnp.zeros_like(ys_ref)


def _experts(blk_e, n_used, seg_end, xs, wu, bu, wd, bd, *, bm):
    n_rows, d = xs.shape
    ne, _, dff2 = wu.shape
    dff = wd.shape[1]
    used = lambda i, nu: jnp.minimum(i, nu[0] - 1)
    grid_spec = pltpu.PrefetchScalarGridSpec(
        num_scalar_prefetch=3,
        grid=(n_rows // bm,),
        in_specs=[pl.BlockSpec((bm, d), lambda i, e, nu, se: (used(i, nu), 0)),
                  pl.BlockSpec(memory_space=pl.ANY),
                  pl.BlockSpec((1, 1, dff2), lambda i, e, nu, se: (e[i], 0, 0)),
                  pl.BlockSpec(memory_space=pl.ANY),
                  pl.BlockSpec((1, 1, d), lambda i, e, nu, se: (e[i], 0, 0))],
        out_specs=pl.BlockSpec((bm, d), lambda i, e, nu, se: (i, 0)),
        scratch_shapes=[pltpu.VMEM((d, dff2), F32), pltpu.VMEM((dff, d), F32),
                        pltpu.VMEM((d, dff2), BF16), pltpu.VMEM((dff, d), BF16),
                        pltpu.SemaphoreType.DMA((2,))],
    )
    return pl.pallas_call(
        _experts_kernel,
        grid_spec=grid_spec,
        out_shape=jax.ShapeDtypeStruct((n_rows, d), F32),
        compiler_params=_cparams(("arbitrary",)),
        name="experts",
    )(blk_e, n_used, seg_end, xs, wu, bu.reshape(ne, 1, dff2), wd, bd.reshape(ne, 1, d))


def _combine_kernel(dest_ref, ys_ref, gate_ref, x1_ref, g2_ref, b2_ref, outp_ref, outs_ref, buf, sem, *, alpha, nt_p):
    i = pl.program_id(0)
    tm = x1_ref.shape[0]

    def start(t, carry):
        for k in range(TOP_K):
            pltpu.make_async_copy(ys_ref.at[pl.ds(dest_ref[0, k, t], 1), :], buf.at[k, pl.ds(t, 1), :], sem).start()
        return carry
    lax.fori_loop(0, tm, start, 0, unroll=ROW_COPY_UNROLL)

    for k in range(TOP_K):
        pltpu.make_async_copy(ys_ref.at[pl.ds(0, tm), :], buf.at[k], sem).wait()

    gate = gate_ref[...].T
    y = buf[0] * gate[:, 0:1]
    for k in range(1, TOP_K):
        y = y + buf[k] * gate[:, k:k + 1]
    out = _layer_norm(alpha * x1_ref[...] + y, g2_ref[...], b2_ref[...])

    @pl.when(i < nt_p)
    def _():
        outp_ref[...] = out

    @pl.when(i >= nt_p)
    def _():
        outs_ref[...] = out


def _combine(dest, ys, gate, x1, g2, b2, *, n_p, tm, alpha):
    n, d = x1.shape
    nt_p = n_p // tm
    return pl.pallas_call(
        functools.partial(_combine_kernel, alpha=alpha, nt_p=nt_p),
        grid=(n // tm,),
        in_specs=[pl.BlockSpec((1, TOP_K, tm), lambda i: (i, 0, 0), memory_space=pltpu.SMEM),
                  pl.BlockSpec(memory_space=pl.ANY),
                  pl.BlockSpec((SUBLANES, tm), lambda i: (0, i)),
                  pl.BlockSpec((tm, d), lambda i: (i, 0)),
                  pl.BlockSpec((1, d), lambda i: (0, 0)),
                  pl.BlockSpec((1, d), lambda i: (0, 0))],
        out_specs=[pl.BlockSpec((tm, d), lambda i: (jnp.minimum(i, nt_p - 1), 0)),
                   pl.BlockSpec((tm, d), lambda i: (jnp.maximum(i - nt_p, 0), 0))],
        out_shape=[jax.ShapeDtypeStruct((n_p, d), F32), jax.ShapeDtypeStruct((n - n_p, d), F32)],
        scratch_shapes=[pltpu.VMEM((TOP_K, tm, d), F32), pltpu.SemaphoreType.DMA(())],
        compiler_params=_cparams(("arbitrary",)),
        name="combine",
    )(dest, ys, gate, x1, g2, b2)


PROJ_TM = 256
HGRN_TL = 256
DSA_TQ = 256
MIX_TM = 256
ROUTE_TL = 1280
MOE_TM = 256
MOE_BM = 512


def kernel(x_prompt, x_sample, cache_k, cache_v, cache_kidx, state_hgrn, w_in, lb_logits, gn_a, w_oa, w_ob,
           w_out, ln1_g, ln1_b, w_router, b_router, w_up, b_up, w_down, b_down, ln2_g, ln2_b):
    depth = w_in.shape[0]
    assert depth == 1
    bsz, seq, d = x_prompt.shape
    dbsz, dseq, _ = x_sample.shape
    past = cache_k.shape[2]
    n_p, n_s = bsz * seq, dbsz * dseq
    n = n_p + n_s
    alpha = (2 * depth) ** 0.25

    sizes = (H_A * DK_A, H_A * DK_A, H_A * DV_A, H_A * DV_A, H_B * DH_B, DH_B, DH_B, H_I * D_I, H_I, D_I, d, d)
    offs = np.concatenate([[0], np.cumsum(sizes)])
    grp = lambda g: w_in[0, :, offs[g]:offs[g + 1]].astype(BF16)
    wqa, wfa, wia, wg, wqb, wkb, wvb, wqi, wwi, wki, wgta, wgtb = (grp(g) for g in range(12))
    wsm = jnp.concatenate([wkb, wvb, wqi, wki, wwi, jnp.zeros((d, SM_WIDTH - SM_USED), BF16)], axis=1)

    x_groups = (x_prompt.reshape(n_p, d), x_sample.reshape(n_s, d))
    qa, f, ia, sg, qb, sm = _proj(*x_groups, wqa, wfa, wia, wg, wqb, wsm, lb_logits, PROJ_TM)

    gn = gn_a[0].reshape(1, DV_A)
    oa_p, sfin_p = _hgrn(qa, f, ia, sg, gn, jnp.zeros((bsz, H_A, DK_A, DV_A), F32),
                         bsz=bsz, seq=seq, row0=0, tl=HGRN_TL)
    oa_s, sfin_s = _hgrn(qa, f, ia, sg, gn, state_hgrn[0], bsz=dbsz, seq=dseq, row0=n_p, tl=dseq)

    kb_p = sm[:n_p, SM_KB:SM_KB + DH_B].reshape(bsz, seq, DH_B)
    vb_p = sm[:n_p, SM_VB:SM_VB + DH_B].reshape(bsz, seq, DH_B)
    ki_p = sm[:n_p, SM_KI:SM_KI + D_I].reshape(bsz, seq, D_I)
    kb_s = sm[n_p:, SM_KB:SM_KB + DH_B].reshape(dbsz, dseq, DH_B)
    vb_s = sm[n_p:, SM_VB:SM_VB + DH_B].reshape(dbsz, dseq, DH_B)
    ki_s = sm[n_p:, SM_KI:SM_KI + D_I].reshape(dbsz, dseq, D_I)
    lim_p = (np.arange(seq) // CHUNK + 1) * CHUNK
    ob_p = _dsa(qb, sm, [(kb_p.astype(BF16), vb_p.astype(BF16), ki_p.astype(BF16))], lim_p,
                bsz=bsz, seq=seq, row0=0, tq=DSA_TQ, n_sel=min(TOPK_MAX, seq // 4))
    n_keys = past + dseq
    new_pad = -(-dseq // LANES) * LANES - dseq
    new_keys = tuple(jnp.pad(a.astype(BF16), ((0, 0), (0, new_pad), (0, 0))) for a in (kb_s, vb_s, ki_s))
    lim_s = np.full((dseq,), n_keys)
    ob_s = _dsa(qb, sm, [(cache_k[0], cache_v[0], cache_kidx[0]), new_keys], lim_s,
                bsz=dbsz, seq=dseq, row0=n_p, tq=dseq, n_sel=min(TOPK_MAX, n_keys // 4))

    wrh, wrl = _split_bf16(w_router[0].T)
    x1, eidx, gate, counts = _mix(
        (oa_p, oa_s), (ob_p, ob_s), x_groups, w_oa[0].astype(BF16), w_ob[0].astype(BF16), wgta, wgtb, w_out[0].astype(BF16),
        ln1_g[0].reshape(1, d), ln1_b[0].reshape(1, d), wrh, wrl, b_router[0].reshape(N_EXPERTS, 1),
        tm=MIX_TM, alpha=alpha)

    n_rows = -(-(n * TOP_K + N_EXPERTS * (MOE_BM - 1)) // MOE_BM) * MOE_BM
    dest, blk_e, meta = _route(eidx, counts, tl=ROUTE_TL, sub=MOE_TM, bm=MOE_BM, nb=n_rows // MOE_BM)
    xs = _dispatch(dest, meta, x1, n_rows, tm=MOE_TM, bm=MOE_BM, ne=N_EXPERTS)
    ys = _experts(blk_e[0, :n_rows // MOE_BM], meta[1, :1], meta[0, :N_EXPERTS], xs, w_up[0], b_up[0], w_down[0],
                  b_down[0], bm=MOE_BM)
    out_p, out_s = _combine(dest, ys, gate, x1, ln2_g[0].reshape(1, d), ln2_b[0].reshape(1, d),
                            n_p=n_p, tm=MOE_TM, alpha=alpha)

    return (out_p.reshape(bsz, seq, d), out_s.reshape(dbsz, dseq, d),
            kb_p[None], vb_p[None], ki_p[None], sfin_p[None],
            kb_s[None], vb_s[None], ki_s[None], sfin_s[None])
```

```python
import functools

import jax
import jax.numpy as jnp
import numpy as np
from jax import lax
from jax.experimental import pallas as pl
from jax.experimental.pallas import tpu as pltpu

F32 = jnp.float32
BF16 = jnp.bfloat16
I32 = jnp.int32

CHUNK = 64
H_A = 8
DK_A = 128
DV_A = 128
HGRN_BLOCK = 16
H_B = 8
DH_B = 128
H_I = 8
D_I = 64
TOPK_MAX = 256
N_EXPERTS = 32
TOP_K = 4
SWIGLU_ALPHA = 1.702
SWIGLU_LIMIT = 7.0
EPS = 1e-5

LANES = 128
SUBLANES = 8
MXU_ROWS = 256
VMEM_LIMIT_BYTES = 56 * 1024 * 1024

INT_MIN = -(2 ** 31)


def _cparams(sem):
    return pltpu.CompilerParams(dimension_semantics=sem, vmem_limit_bytes=VMEM_LIMIT_BYTES)


def _sigmoid(x):
    return 1.0 / (1.0 + jnp.exp(-x))


def _layer_norm(x, g, b):
    mu = jnp.mean(x, axis=-1, keepdims=True)
    xc = x - mu
    var = jnp.mean(xc * xc, axis=-1, keepdims=True)
    return xc * lax.rsqrt(var + EPS) * g + b


def _two_group_specs(tm, width, nt_p):
    return (pl.BlockSpec((tm, width), lambda i: (jnp.minimum(i, nt_p - 1), 0)),
            pl.BlockSpec((tm, width), lambda i: (jnp.maximum(i - nt_p, 0), 0)))


def _group_tile(p_ref, s_ref, nt_p):
    return jnp.where(pl.program_id(0) < nt_p, p_ref[...], s_ref[...])


def _proj_kernel(xp_ref, xs_ref, wqa_ref, wfa_ref, wia_ref, wga_ref, wqb_ref, wsm_ref, lbl_ref,
                 qa_ref, f_ref, ia_ref, sg_ref, qb_ref, sm_ref, *, nt_p):
    xb = _group_tile(xp_ref, xs_ref, nt_p).astype(BF16)
    dot = lambda w_ref: jnp.dot(xb, w_ref[...], preferred_element_type=F32)
    qa_ref[...] = dot(wqa_ref)
    lbl = lbl_ref[...]
    lbe = jnp.exp(lbl - jnp.max(lbl, axis=0, keepdims=True))
    lb = lbe[0:1, :] / jnp.sum(lbe, axis=0, keepdims=True)
    f_ref[...] = lb + (1.0 - lb) * _sigmoid(dot(wfa_ref))
    ia_ref[...] = dot(wia_ref)
    g = dot(wga_ref)
    sg_ref[...] = g * _sigmoid(g)
    qb_ref[...] = dot(wqb_ref).astype(BF16)
    sm_ref[...] = dot(wsm_ref)


def _proj(x_p, x_s, wqa, wfa, wia, wga, wqb, wsm, lb_logits, tm):
    d = x_p.shape[1]
    n = x_p.shape[0] + x_s.shape[0]
    nt_p = x_p.shape[0] // tm
    nsm = wsm.shape[1]
    row = lambda w: pl.BlockSpec((tm, w), lambda i: (i, 0))
    wspec = lambda w: pl.BlockSpec(w.shape, lambda i: (0, 0))
    return pl.pallas_call(
        functools.partial(_proj_kernel, nt_p=nt_p),
        grid=(n // tm,),
        in_specs=[*_two_group_specs(tm, d, nt_p), wspec(wqa), wspec(wfa), wspec(wia), wspec(wga), wspec(wqb),
                  wspec(wsm), wspec(lb_logits)],
        out_specs=[row(d), row(d), row(d), row(d), row(d), row(nsm)],
        out_shape=[jax.ShapeDtypeStruct((n, d), F32)] * 4
        + [jax.ShapeDtypeStruct((n, d), BF16), jax.ShapeDtypeStruct((n, nsm), F32)],
        compiler_params=_cparams(("arbitrary",)),
        name="proj",
    )(x_p, x_s, wqa, wfa, wia, wga, wqb, wsm, lb_logits)


HGRN_UNROLL = 16


def _block_cumsum(a, row_in_block):
    s = 1
    while s < HGRN_BLOCK:
        a = a + jnp.where(row_in_block >= s, pltpu.roll(a, s, 0), 0.0)
        s *= 2
    return a


def _hgrn_kernel(qa_ref, f_ref, ia_ref, sg_ref, gn_ref, s0_ref, o_ref, sfin_ref,
                 s_scr, qd_scr, qm_scr, km_scr, kl_scr, v_scr, dl_scr, o_scr):
    j = pl.program_id(1)
    tl = qa_ref.shape[0]
    nblk = tl // HGRN_BLOCK
    half = HGRN_BLOCK // 2

    @pl.when(j == 0)
    def _():
        for h in range(H_A):
            s_scr[h] = s0_ref[0, h].T

    f = f_ref[...]
    q = qa_ref[...]
    d = f.shape[1]
    row_in_block = lax.broadcasted_iota(I32, (tl, d), 0) % HGRN_BLOCK
    cum = _block_cumsum(jnp.log(f), row_in_block)
    cum3 = cum.reshape(nblk, HGRN_BLOCK, d)
    mid = jnp.broadcast_to(cum3[:, half:half + 1, :], cum3.shape).reshape(tl, d)
    last3 = cum3[:, HGRN_BLOCK - 1:HGRN_BLOCK, :]
    last = jnp.broadcast_to(last3, cum3.shape).reshape(tl, d)
    k = 1.0 - f
    qd_scr[...] = (q * jnp.exp(cum)).astype(BF16)
    qm_scr[...] = (q * jnp.exp(cum - mid)).astype(BF16)
    km_scr[...] = (k * jnp.exp(mid - cum)).astype(BF16)
    kl_scr[...] = (k * jnp.exp(last - cum)).astype(BF16)
    dl_scr[...] = jnp.exp(last3.reshape(nblk, d))
    v_scr[...] = ia_ref[...].astype(BF16)

    nt = (((1,), (1,)), ((), ()))
    ri = lax.broadcasted_iota(I32, (tl, tl), 0)
    ci = lax.broadcasted_iota(I32, (tl, tl), 1)
    keep = (ri // HGRN_BLOCK == ci // HGRN_BLOCK) & (ri >= ci)
    for h in range(H_A):
        cols = slice(h * DK_A, (h + 1) * DK_A)
        vcols = slice(h * DV_A, (h + 1) * DV_A)
        att = lax.dot_general(qm_scr[:, cols], km_scr[:, cols], nt, preferred_element_type=F32)
        att = jnp.where(keep, att, 0.0).astype(BF16)
        o_scr[:, vcols] = jnp.dot(att, v_scr[:, vcols], preferred_element_type=F32)

    def block(b, carry):
        r0 = pl.multiple_of(b * HGRN_BLOCK, HGRN_BLOCK)
        rows = pl.ds(r0, HGRN_BLOCK)
        dl = dl_scr[pl.ds(b, 1), :]
        for h in range(H_A):
            cols = slice(h * DK_A, (h + 1) * DK_A)
            vcols = slice(h * DV_A, (h + 1) * DV_A)
            st_h = s_scr[h]
            o_scr[rows, vcols] += lax.dot_general(qd_scr[rows, cols], st_h.astype(BF16), nt,
                                                  preferred_element_type=F32)
            upd_t = lax.dot_general(v_scr[rows, vcols], kl_scr[rows, cols], (((0,), (0,)), ((), ())),
                                    preferred_element_type=F32)
            s_scr[h] = dl[:, cols] * st_h + upd_t
        return carry

    lax.fori_loop(0, nblk, block, 0, unroll=HGRN_UNROLL if nblk % HGRN_UNROLL == 0 else 1)

    gn = gn_ref[...]
    for h in range(H_A):
        vcols = slice(h * DV_A, (h + 1) * DV_A)
        o = o_scr[:, vcols]
        o = o * lax.rsqrt(jnp.mean(o * o, axis=-1, keepdims=True) + EPS) * gn
        o_ref[:, vcols] = (o * sg_ref[:, vcols]).astype(BF16)

    @pl.when(j == pl.num_programs(1) - 1)
    def _():
        for h in range(H_A):
            sfin_ref[0, h] = s_scr[h].T


def _hgrn(qa, f, ia, sg, gn, s0, *, bsz, seq, row0, tl):
    d = qa.shape[1]
    nt = seq // tl
    blk0 = row0 // tl
    row = pl.BlockSpec((tl, d), lambda b, j: (blk0 + b * nt + j, 0))
    orow = pl.BlockSpec((tl, d), lambda b, j: (b * nt + j, 0))
    sspec = pl.BlockSpec((1, H_A, DK_A, DV_A), lambda b, j: (b, 0, 0, 0))
    return pl.pallas_call(
        _hgrn_kernel,
        grid=(bsz, nt),
        in_specs=[row, row, row, row, pl.BlockSpec((1, DV_A), lambda b, j: (0, 0)), sspec],
        out_specs=[orow, sspec],
        out_shape=[jax.ShapeDtypeStruct((bsz * seq, d), BF16),
                   jax.ShapeDtypeStruct((bsz, H_A, DK_A, DV_A), F32)],
        scratch_shapes=[pltpu.VMEM((H_A, DK_A, DV_A), F32)]
        + [pltpu.VMEM((tl, d), BF16)] * 5
        + [pltpu.VMEM((tl // HGRN_BLOCK, d), F32), pltpu.VMEM((tl, d), F32)],
        compiler_params=_cparams(("arbitrary", "arbitrary")),
        name="hgrn",
    )(qa, f, ia, sg, gn, s0)


SM_KB, SM_VB, SM_QI, SM_KI, SM_WI = 0, DH_B, 2 * DH_B, 2 * DH_B + H_I * D_I, 2 * DH_B + H_I * D_I + D_I
SM_USED = SM_WI + H_I
SM_WIDTH = -(-SM_USED // LANES) * LANES
INDEX_SCALE = (H_I * D_I) ** -0.5
NEG_INF = float("-inf")
LOG2_E = 1.4426950408889634


F32_EXP_MASK = 0x7F800000
F32_MIN_NORMAL = 0x00800000
SEARCH_UNROLL = 4


def _key_to_float(u):
    key = u ^ jnp.int32(INT_MIN)
    bits = jnp.where(key < 0, key ^ jnp.int32(0x7FFFFFFF), key)
    below_neg_inf = (bits < 0) & ((bits & jnp.int32(0x7FFFFFFF)) > jnp.int32(F32_EXP_MASK))
    return jnp.where(below_neg_inf, NEG_INF, lax.bitcast_convert_type(bits, F32))


def _positive_bits_to_float(bits):
    return jnp.where(bits < jnp.int32(F32_MIN_NORMAL), 0.0, lax.bitcast_convert_type(bits, F32))


def _count(mask):
    return jnp.sum(jnp.where(mask, 1.0, 0.0), axis=1, keepdims=True)


def _select_bias(score_scr, bias_scr, adm, s, n_sel):
    tq = score_scr.shape[0]

    def resolve_bit(i, t_u, cnt_t):
        cand_u = t_u | lax.shift_left(jnp.int32(1), 31 - i)
        cnt = _count(score_scr[:, :s] >= _key_to_float(cand_u))
        ok = cnt >= n_sel
        return jnp.where(ok, cand_u, t_u), jnp.where(ok, cnt, cnt_t)

    def unresolved(c):
        i, _, cnt_t = c
        return jnp.logical_and(i < 32, jnp.any(cnt_t != n_sel))

    def resolve_bits(c):
        i, t_u, cnt_t = c
        for b in range(SEARCH_UNROLL):
            t_u, cnt_t = resolve_bit(i + b, t_u, cnt_t)
        return i + SEARCH_UNROLL, t_u, cnt_t

    _, t_u, cnt_t = lax.while_loop(unresolved, resolve_bits,
                                   (jnp.int32(0), jnp.zeros((tq, 1), I32), jnp.full((tq, 1), float(s), F32)))
    t_f = _key_to_float(t_u)
    score = score_scr[:, :s]
    bias_scr[:, :s] = jnp.where((score >= t_f) & adm, 0.0, NEG_INF)

    @pl.when(jnp.any(cnt_t > n_sel))
    def _():
        above = score >= _key_to_float(t_u + 1)
        bucket = (score >= t_f) & jnp.logical_not(above)
        need = n_sel - _count(above)
        off = jnp.where(bucket, score - jnp.where(t_f == NEG_INF, 0.0, t_f), -1.0)

        def resolve_offset_bit(i, r_bits):
            cand = r_bits | lax.shift_left(jnp.int32(1), 30 - i)
            ok = _count(off >= _positive_bits_to_float(cand)) >= need
            return jnp.where(ok, cand, r_bits)

        ranked = jnp.any(off > 0.0)
        r_bits = lax.fori_loop(0, jnp.where(ranked, 31, 0), resolve_offset_bit,
                               jnp.full((tq, 1), jnp.where(ranked, 0, F32_MIN_NORMAL - 1), I32))
        above2 = off >= _positive_bits_to_float(r_bits + 1)
        tie = (off >= _positive_bits_to_float(r_bits)) & jnp.logical_not(above2)
        need2 = need - _count(above2)
        upper = (lax.broadcasted_iota(I32, (LANES, LANES), 0)
                 <= lax.broadcasted_iota(I32, (LANES, LANES), 1)).astype(BF16)
        carry = jnp.zeros((tq, 1), F32)
        for c in range(s // LANES):
            cs = slice(c * LANES, (c + 1) * LANES)
            rank = jnp.dot(jnp.where(tie[:, cs], 1.0, 0.0).astype(BF16), upper, preferred_element_type=F32) + carry
            sel = (above[:, cs] | above2[:, cs] | (tie[:, cs] & (rank <= need2))) & adm[:, cs]
            bias_scr[:, cs] = jnp.where(sel, 0.0, NEG_INF)
            carry = rank[:, LANES - 1:LANES]


def _dsa_tile(qb_ref, sm_ref, *refs, segs, search, n_sel):
    key_refs, (lim_ref, o_ref, score_scr, bias_scr) = refs[:3 * len(segs)], refs[3 * len(segs):]
    tq = qb_ref.shape[0]
    s = sum(segs)
    nt = (((1,), (1,)), ((), ()))
    adm = lax.broadcasted_iota(I32, (tq, s), 1) < lim_ref[...]
    seg_keys = [[key_refs[3 * g + j][0, :sg, :].astype(BF16) for j in range(3)] for g, sg in enumerate(segs)]
    starts = [sum(segs[:g]) for g in range(len(segs))]
    against_keys = lambda lhs, j: jnp.concatenate(
        [lax.dot_general(lhs, kv[j], nt, preferred_element_type=F32) for kv in seg_keys], axis=1)
    times_values = lambda p: sum(jnp.dot(p[:, st:st + sg], kv[1], preferred_element_type=F32)
                                 for st, sg, kv in zip(starts, segs, seg_keys))
    stack = tq * max(H_I, H_B) <= MXU_ROWS
    head_rows = lambda a, h: a[h * tq:(h + 1) * tq]
    qi_head = lambda h: sm_ref[:, SM_QI + h * D_I:SM_QI + (h + 1) * D_I].astype(BF16)
    q_head = lambda h: qb_ref[:, h * DH_B:(h + 1) * DH_B]

    if search:
        if stack:
            stacked = against_keys(jnp.concatenate([qi_head(h) for h in range(H_I)], axis=0), 2)
        score = jnp.zeros((tq, s), F32)
        for h in range(H_I):
            sh = head_rows(stacked, h) if stack else against_keys(qi_head(h), 2)
            wih = sm_ref[:, SM_WI + h:SM_WI + h + 1] * INDEX_SCALE
            score = score + jnp.maximum(sh, 0.0) * wih
        score_scr[:, :s] = jnp.where(adm, score, NEG_INF)
        _select_bias(score_scr, bias_scr, adm, s, n_sel)
    else:
        bias_scr[:, :s] = jnp.where(adm, 0.0, NEG_INF)

    def softmax_terms(logits):
        p = jnp.exp2((logits - jnp.max(logits, axis=1, keepdims=True)) * (DH_B ** -0.5 * LOG2_E))
        return p.astype(BF16), jnp.sum(p, axis=1, keepdims=True)

    if stack:
        stacked = against_keys(jnp.concatenate([q_head(h) for h in range(H_B)], axis=0), 0)
        terms = [softmax_terms(head_rows(stacked, h) + bias_scr[:, :s]) for h in range(H_B)]
        o_all = times_values(jnp.concatenate([p for p, _ in terms], axis=0))
        for h in range(H_B):
            o_ref[:, h * DH_B:(h + 1) * DH_B] = (head_rows(o_all, h) / terms[h][1]).astype(BF16)
    else:
        for h in range(H_B):
            p, denom = softmax_terms(against_keys(q_head(h), 0) + bias_scr[:, :s])
            o_ref[:, h * DH_B:(h + 1) * DH_B] = (times_values(p) / denom).astype(BF16)


DSA_KEY_GRAN = 256


def _dsa(qb, sm, key_segments, limits, *, bsz, seq, row0, tq, n_sel):
    d = qb.shape[1]
    seg_rows = [seg[0].shape[1] for seg in key_segments]
    n_keys = sum(seg_rows)
    nt = seq // tq
    blk0 = row0 // tq
    limits = np.asarray(limits, np.int32)
    tile_max = limits.reshape(nt, tq).max(axis=1)
    per_tile = [(int(min(n_keys, -(-m // DSA_KEY_GRAN) * DSA_KEY_GRAN)), bool(m > n_sel)) for m in tile_max]
    lim2d = jnp.asarray(limits.reshape(seq, 1))
    outs, lo = [], 0
    while lo < nt:
        hi = lo
        while hi + 1 < nt and per_tile[hi + 1] == per_tile[lo]:
            hi += 1
        s, search = per_tile[lo]
        assert s == n_keys or len(key_segments) == 1
        segs = tuple(seg_rows) if s == n_keys else (s,)
        ntv = hi - lo + 1
        row = lambda w, lo=lo: pl.BlockSpec((tq, w), lambda b, j: (blk0 + b * nt + lo + j, 0))
        keys = lambda rows, w: pl.BlockSpec((1, rows, w), lambda b, j: (b, 0, 0))
        out = pl.pallas_call(
            functools.partial(_dsa_tile, segs=segs, search=search, n_sel=n_sel),
            grid=(bsz, ntv),
            in_specs=[row(d), row(sm.shape[1])]
            + [keys(rows, a.shape[2]) for rows, seg in zip(segs, key_segments) for a in seg]
            + [pl.BlockSpec((tq, 1), lambda b, j, lo=lo: (lo + j, 0))],
            out_specs=pl.BlockSpec((tq, d), lambda b, j, ntv=ntv: (b * ntv + j, 0)),
            out_shape=jax.ShapeDtypeStruct((bsz * ntv * tq, d), BF16),
            scratch_shapes=[pltpu.VMEM((tq, s), F32), pltpu.VMEM((tq, s), F32)],
            compiler_params=_cparams(("arbitrary", "arbitrary")),
            name="dsa",
        )(qb, sm, *[a for seg in key_segments for a in seg], lim2d)
        outs.append(out.reshape(bsz, ntv * tq, d))
        lo = hi + 1
    return jnp.concatenate(outs, axis=1).reshape(bsz * seq, d)


def _split_bf16(a):
    hi = a.astype(BF16)
    return hi, (a - hi.astype(F32)).astype(BF16)


def _mix_kernel(oap_ref, oas_ref, obp_ref, obs_ref, xp_ref, xs_ref, woa_ref, wob_ref, wga_ref, wgb_ref, wout_ref,
                g1_ref, b1_ref, wrh_ref, wrl_ref, br_ref, x1_ref, eidx_ref, gate_ref, cnt_ref, *, alpha, nt_p):
    i = pl.program_id(0)
    x = _group_tile(xp_ref, xs_ref, nt_p)
    xb = x.astype(BF16)
    dot = lambda a, w_ref: jnp.dot(a, w_ref[...], preferred_element_type=F32)
    y_a = dot(_group_tile(oap_ref, oas_ref, nt_p), woa_ref)
    y_b = dot(_group_tile(obp_ref, obs_ref, nt_p), wob_ref)
    merged = _sigmoid(dot(xb, wga_ref)) * y_a + _sigmoid(dot(xb, wgb_ref)) * y_b
    mixed = dot(merged.astype(BF16), wout_ref)
    x1 = _layer_norm(alpha * x + mixed, g1_ref[...], b1_ref[...])
    x1_ref[...] = x1

    nt = (((1,), (1,)), ((), ()))
    xh, xl = _split_bf16(x1)
    wh, wl = wrh_ref[...], wrl_ref[...]
    logits = (lax.dot_general(wh, xh, nt, preferred_element_type=F32)
              + lax.dot_general(wh, xl, nt, preferred_element_type=F32)
              + lax.dot_general(wl, xh, nt, preferred_element_type=F32)) + br_ref[...]
    ne, tm = logits.shape
    erow = lax.broadcasted_iota(I32, (ne, tm), 0)
    vals, idxs = [], []
    for _ in range(TOP_K):
        m = jnp.max(logits, axis=0, keepdims=True)
        idx = jnp.min(jnp.where(logits == m, erow, ne), axis=0, keepdims=True)
        vals.append(m)
        idxs.append(idx)
        logits = jnp.where(erow == idx, NEG_INF, logits)
    ex = [jnp.exp(v - vals[0]) for v in vals]
    denom = ex[0] + ex[1] + ex[2] + ex[3]
    pad = SUBLANES - TOP_K
    eidx = jnp.concatenate(idxs + [jnp.zeros((pad, tm), I32)], axis=0)
    eidx_ref[...] = eidx
    gate_ref[...] = jnp.concatenate([e / denom for e in ex] + [jnp.zeros((pad, tm), F32)], axis=0)

    onehot = jnp.zeros((ne, tm), F32)
    for idx in idxs:
        onehot = onehot + jnp.where(erow == idx, 1.0, 0.0)
    tile_cnt = jnp.broadcast_to(jnp.sum(onehot, axis=1, keepdims=True), cnt_ref.shape)

    @pl.when(i == 0)
    def _():
        cnt_ref[...] = tile_cnt

    @pl.when(i > 0)
    def _():
        cnt_ref[...] = cnt_ref[...] + tile_cnt


def _mix(oa, ob, x, woa, wob, wga, wgb, wout, g1, b1, wrh, wrl, br, *, tm, alpha):
    d = x[0].shape[1]
    n = x[0].shape[0] + x[1].shape[0]
    nt_p = x[0].shape[0] // tm
    ne = wrh.shape[0]
    row = lambda w: pl.BlockSpec((tm, w), lambda i: (i, 0))
    col = pl.BlockSpec((SUBLANES, tm), lambda i: (0, i))
    full = lambda a: pl.BlockSpec(a.shape, lambda i: (0,) * a.ndim)
    return pl.pallas_call(
        functools.partial(_mix_kernel, alpha=alpha, nt_p=nt_p),
        grid=(n // tm,),
        in_specs=[*_two_group_specs(tm, d, nt_p)] * 3
        + [full(a) for a in (woa, wob, wga, wgb, wout, g1, b1, wrh, wrl, br)],
        out_specs=[row(d), col, col, pl.BlockSpec((ne, LANES), lambda i: (0, 0))],
        out_shape=[jax.ShapeDtypeStruct((n, d), F32), jax.ShapeDtypeStruct((SUBLANES, n), I32),
                   jax.ShapeDtypeStruct((SUBLANES, n), F32), jax.ShapeDtypeStruct((ne, LANES), F32)],
        compiler_params=_cparams(("arbitrary",)),
        name="mix",
    )(*oa, *ob, *x, woa, wob, wga, wgb, wout, g1, b1, wrh, wrl, br)


def _sublane_cumsum(a):
    n = a.shape[0]
    row = lax.broadcasted_iota(I32, a.shape, 0)
    s = 1
    while s < n:
        a = a + jnp.where(row >= s, pltpu.roll(a, s, 0), 0.0)
        s *= 2
    return a


def _route_kernel(eidx_ref, cnt_ref, dest_ref, blke_ref, meta_ref, carry_scr, *, bm, sub):
    i = pl.program_id(0)
    ne = cnt_ref.shape[0]
    tl = eidx_ref.shape[1]

    @pl.when(i == 0)
    def _():
        carry_scr[...] = jnp.zeros_like(carry_scr)

    counts = cnt_ref[...]
    padded = jnp.ceil(counts / bm) * bm
    pend = _sublane_cumsum(padded)
    pstart = (pend - padded)[:, 0:1]

    @pl.when(i == 0)
    def _():
        nb = blke_ref.shape[1]
        first_row = (lax.broadcasted_iota(I32, (ne, nb), 1) * bm).astype(F32)
        below = jnp.sum(jnp.where(pend[:, 0:1] <= first_row, 1.0, 0.0), axis=0, keepdims=True)
        blke_ref[...] = jnp.minimum(below, ne - 1.0).astype(I32)
        pend_lanes = jnp.concatenate([pend, jnp.zeros((LANES - ne, LANES), F32)], axis=0).T[0:1, :]
        n_used = jnp.max(pend, axis=0, keepdims=True) / bm
        meta_ref[...] = jnp.concatenate([pend_lanes, n_used, jnp.zeros((SUBLANES - 2, LANES), F32)],
                                        axis=0).astype(I32)

    eidx = eidx_ref[...]
    erow = lax.broadcasted_iota(I32, (ne, tl), 0)
    hot = [jnp.where(erow == eidx[k:k + 1, :], 1.0, 0.0) for k in range(TOP_K)]
    onehot = hot[0] + hot[1] + hot[2] + hot[3]
    before = (lax.broadcasted_iota(I32, (tl, tl), 0) < lax.broadcasted_iota(I32, (tl, tl), 1)).astype(BF16)
    base = jnp.dot(onehot.astype(BF16), before, preferred_element_type=F32) + carry_scr[:, 0:1] + pstart
    dest = jnp.concatenate([jnp.sum(hk * base, axis=0, keepdims=True) for hk in hot], axis=0).astype(I32)
    for c in range(tl // sub):
        dest_ref[c] = dest[:, c * sub:(c + 1) * sub]
    carry_scr[...] = carry_scr[...] + jnp.sum(onehot, axis=1, keepdims=True)


def _route(eidx, counts, *, tl, sub, bm, nb):
    n = eidx.shape[1]
    ne = counts.shape[0]
    nb_pad = -(-nb // LANES) * LANES
    return pl.pallas_call(
        functools.partial(_route_kernel, bm=bm, sub=sub),
        grid=(n // tl,),
        in_specs=[pl.BlockSpec((SUBLANES, tl), lambda i: (0, i)), pl.BlockSpec(counts.shape, lambda i: (0, 0))],
        out_specs=[pl.BlockSpec((tl // sub, TOP_K, sub), lambda i: (i, 0, 0)),
                   pl.BlockSpec((1, nb_pad), lambda i: (0, 0)),
                   pl.BlockSpec((SUBLANES, LANES), lambda i: (0, 0))],
        out_shape=[jax.ShapeDtypeStruct((n // sub, TOP_K, sub), I32), jax.ShapeDtypeStruct((1, nb_pad), I32),
                   jax.ShapeDtypeStruct((SUBLANES, LANES), I32)],
        scratch_shapes=[pltpu.VMEM((ne, LANES), F32)],
        compiler_params=_cparams(("arbitrary",)),
        name="route",
    )(eidx, counts)


ROW_COPY_UNROLL = 8


def _dispatch_kernel(dest_ref, meta_ref, x_ref, xs_ref, zbuf, sem, zsem, *, bm, ne, nb):
    i = pl.program_id(0)
    tm = x_ref.shape[0]

    @pl.when(i == 0)
    def _():
        zbuf[...] = jnp.zeros_like(zbuf)
        zero_block = lambda row0: pltpu.make_async_copy(zbuf, xs_ref.at[pl.ds(row0, bm), :], zsem)

        def each_padding_block(action):
            for e in range(ne):
                seg_end = meta_ref[0, e]
                seg_start = meta_ref[0, e - 1] if e > 0 else 0

                @pl.when(seg_end > seg_start)
                def _(seg_end=seg_end):
                    action(zero_block(pl.multiple_of(seg_end - bm, bm)))

            def tail_block(b, carry):
                action(zero_block(pl.multiple_of(b * bm, bm)))
                return carry
            lax.fori_loop(meta_ref[1, 0], nb, tail_block, 0)

        each_padding_block(lambda copy: copy.start())
        each_padding_block(lambda copy: copy.wait())

    def start(t, carry):
        for k in range(TOP_K):
            pltpu.make_async_copy(x_ref.at[pl.ds(t, 1), :], xs_ref.at[pl.ds(dest_ref[0, k, t], 1), :], sem).start()
        return carry
    lax.fori_loop(0, tm, start, 0, unroll=ROW_COPY_UNROLL)

    for _ in range(TOP_K):
        pltpu.make_async_copy(x_ref, xs_ref.at[pl.ds(0, tm), :], sem).wait()


def _dispatch(dest, meta, x1, n_rows, *, tm, bm, ne):
    n, d = x1.shape
    return pl.pallas_call(
        functools.partial(_dispatch_kernel, bm=bm, ne=ne, nb=n_rows // bm),
        grid=(n // tm,),
        in_specs=[pl.BlockSpec((1, TOP_K, tm), lambda i: (i, 0, 0), memory_space=pltpu.SMEM),
                  pl.BlockSpec(meta.shape, lambda i: (0, 0), memory_space=pltpu.SMEM),
                  pl.BlockSpec((tm, d), lambda i: (i, 0))],
        out_specs=pl.BlockSpec(memory_space=pl.ANY),
        out_shape=jax.ShapeDtypeStruct((n_rows, d), x1.dtype),
        scratch_shapes=[pltpu.VMEM((bm, d), x1.dtype), pltpu.SemaphoreType.DMA(()), pltpu.SemaphoreType.DMA(())],
        compiler_params=_cparams(("arbitrary",)),
        name="dispatch",
    )(dest, meta, x1)


def _experts_kernel(blke_ref, nused_ref, segend_ref, xs_ref, wu_ref, bu_ref, wd_ref, bd_ref, ys_ref,
                    wu32, wd32, wu16, wd16, sems):
    i = pl.program_id(0)
    bm = xs_ref.shape[0]
    dff = wd16.shape[0]
    n_used = nused_ref[0]

    def weight_copies(e):
        return (pltpu.make_async_copy(wu_ref.at[e], wu32, sems.at[0]),
                pltpu.make_async_copy(wd_ref.at[e], wd32, sems.at[1]))

    @pl.when(i < n_used)
    def _():
        e_cur = blke_ref[i]

        @pl.when(jnp.logical_or(i == 0, e_cur != blke_ref[jnp.maximum(i - 1, 0)]))
        def _():
            @pl.when(i == 0)
            def _():
                for copy in weight_copies(e_cur):
                    copy.start()

            for copy in weight_copies(e_cur):
                copy.wait()
            wu16[...] = wu32[...].astype(BF16)
            wd16[...] = wd32[...].astype(BF16)

            nxt = segend_ref[e_cur] // bm

            @pl.when(nxt < n_used)
            def _():
                for copy in weight_copies(blke_ref[jnp.minimum(nxt, blke_ref.shape[0] - 1)]):
                    copy.start()

        h = jnp.dot(xs_ref[...].astype(BF16), wu16[...], preferred_element_type=F32) + bu_ref[0]
        glu = jnp.minimum(h[:, :dff], SWIGLU_LIMIT)
        lin = jnp.clip(h[:, dff:], -SWIGLU_LIMIT, SWIGLU_LIMIT)
        act = glu * _sigmoid(SWIGLU_ALPHA * glu) * (lin + 1.0)
        ys_ref[...] = jnp.dot(act.astype(BF16), wd16[...], preferred_element_type=F32) + bd_ref[0]

    @pl.when(i >= nused_ref[0])
    def _():
        ys_ref[...] = jnp.zeros_like(ys_ref)


def _experts(blk_e, n_used, seg_end, xs, wu, bu, wd, bd, *, bm):
    n_rows, d = xs.shape
    ne, _, dff2 = wu.shape
    dff = wd.shape[1]
    used = lambda i, nu: jnp.minimum(i, nu[0] - 1)
    grid_spec = pltpu.PrefetchScalarGridSpec(
        num_scalar_prefetch=3,
        grid=(n_rows // bm,),
        in_specs=[pl.BlockSpec((bm, d), lambda i, e, nu, se: (used(i, nu), 0)),
                  pl.BlockSpec(memory_space=pl.ANY),
                  pl.BlockSpec((1, 1, dff2), lambda i, e, nu, se: (e[i], 0, 0)),
                  pl.BlockSpec(memory_space=pl.ANY),
                  pl.BlockSpec((1, 1, d), lambda i, e, nu, se: (e[i], 0, 0))],
        out_specs=pl.BlockSpec((bm, d), lambda i, e, nu, se: (i, 0)),
        scratch_shapes=[pltpu.VMEM((d, dff2), F32), pltpu.VMEM((dff, d), F32),
                        pltpu.VMEM((d, dff2), BF16), pltpu.VMEM((dff, d), BF16),
                        pltpu.SemaphoreType.DMA((2,))],
    )
    return pl.pallas_call(
        _experts_kernel,
        grid_spec=grid_spec,
        out_shape=jax.ShapeDtypeStruct((n_rows, d), F32),
        compiler_params=_cparams(("arbitrary",)),
        name="experts",
    )(blk_e, n_used, seg_end, xs, wu, bu.reshape(ne, 1, dff2), wd, bd.reshape(ne, 1, d))


def _combine_kernel(dest_ref, ys_ref, gate_ref, x1_ref, g2_ref, b2_ref, outp_ref, outs_ref, buf, sem, *, alpha, nt_p):
    i = pl.program_id(0)
    tm = x1_ref.shape[0]

    def start(t, carry):
        for k in range(TOP_K):
            pltpu.make_async_copy(ys_ref.at[pl.ds(dest_ref[0, k, t], 1), :], buf.at[k, pl.ds(t, 1), :], sem).start()
        return carry
    lax.fori_loop(0, tm, start, 0, unroll=ROW_COPY_UNROLL)

    for k in range(TOP_K):
        pltpu.make_async_copy(ys_ref.at[pl.ds(0, tm), :], buf.at[k], sem).wait()

    gate = gate_ref[...].T
    y = buf[0] * gate[:, 0:1]
    for k in range(1, TOP_K):
        y = y + buf[k] * gate[:, k:k + 1]
    out = _layer_norm(alpha * x1_ref[...] + y, g2_ref[...], b2_ref[...])

    @pl.when(i < nt_p)
    def _():
        outp_ref[...] = out

    @pl.when(i >= nt_p)
    def _():
        outs_ref[...] = out


def _combine(dest, ys, gate, x1, g2, b2, *, n_p, tm, alpha):
    n, d = x1.shape
    nt_p = n_p // tm
    return pl.pallas_call(
        functools.partial(_combine_kernel, alpha=alpha, nt_p=nt_p),
        grid=(n // tm,),
        in_specs=[pl.BlockSpec((1, TOP_K, tm), lambda i: (i, 0, 0), memory_space=pltpu.SMEM),
                  pl.BlockSpec(memory_space=pl.ANY),
                  pl.BlockSpec((SUBLANES, tm), lambda i: (0, i)),
                  pl.BlockSpec((tm, d), lambda i: (i, 0)),
                  pl.BlockSpec((1, d), lambda i: (0, 0)),
                  pl.BlockSpec((1, d), lambda i: (0, 0))],
        out_specs=[pl.BlockSpec((tm, d), lambda i: (jnp.minimum(i, nt_p - 1), 0)),
                   pl.BlockSpec((tm, d), lambda i: (jnp.maximum(i - nt_p, 0), 0))],
        out_shape=[jax.ShapeDtypeStruct((n_p, d), F32), jax.ShapeDtypeStruct((n - n_p, d), F32)],
        scratch_shapes=[pltpu.VMEM((TOP_K, tm, d), F32), pltpu.SemaphoreType.DMA(())],
        compiler_params=_cparams(("arbitrary",)),
        name="combine",
    )(dest, ys, gate, x1, g2, b2)


PROJ_TM = 256
HGRN_TL = 256
DSA_TQ = 256
MIX_TM = 256
ROUTE_TL = 1280
MOE_TM = 256
MOE_BM = 512


def kernel(x_prompt, x_sample, cache_k, cache_v, cache_kidx, state_hgrn, w_in, lb_logits, gn_a, w_oa, w_ob,
           w_out, ln1_g, ln1_b, w_router, b_router, w_up, b_up, w_down, b_down, ln2_g, ln2_b):
    depth = w_in.shape[0]
    assert depth == 1
    bsz, seq, d = x_prompt.shape
    dbsz, dseq, _ = x_sample.shape
    past = cache_k.shape[2]
    n_p, n_s = bsz * seq, dbsz * dseq
    n = n_p + n_s
    alpha = (2 * depth) ** 0.25

    sizes = (H_A * DK_A, H_A * DK_A, H_A * DV_A, H_A * DV_A, H_B * DH_B, DH_B, DH_B, H_I * D_I, H_I, D_I, d, d)
    offs = np.concatenate([[0], np.cumsum(sizes)])
    grp = lambda g: w_in[0, :, offs[g]:offs[g + 1]].astype(BF16)
    wqa, wfa, wia, wg, wqb, wkb, wvb, wqi, wwi, wki, wgta, wgtb = (grp(g) for g in range(12))
    wsm = jnp.concatenate([wkb, wvb, wqi, wki, wwi, jnp.zeros((d, SM_WIDTH - SM_USED), BF16)], axis=1)

    x_groups = (x_prompt.reshape(n_p, d), x_sample.reshape(n_s, d))
    qa, f, ia, sg, qb, sm = _proj(*x_groups, wqa, wfa, wia, wg, wqb, wsm, lb_logits, PROJ_TM)

    gn = gn_a[0].reshape(1, DV_A)
    oa_p, sfin_p = _hgrn(qa, f, ia, sg, gn, jnp.zeros((bsz, H_A, DK_A, DV_A), F32),
                         bsz=bsz, seq=seq, row0=0, tl=HGRN_TL)
    oa_s, sfin_s = _hgrn(qa, f, ia, sg, gn, state_hgrn[0], bsz=dbsz, seq=dseq, row0=n_p, tl=dseq)

    kb_p = sm[:n_p, SM_KB:SM_KB + DH_B].reshape(bsz, seq, DH_B)
    vb_p = sm[:n_p, SM_VB:SM_VB + DH_B].reshape(bsz, seq, DH_B)
    ki_p = sm[:n_p, SM_KI:SM_KI + D_I].reshape(bsz, seq, D_I)
    kb_s = sm[n_p:, SM_KB:SM_KB + DH_B].reshape(dbsz, dseq, DH_B)
    vb_s = sm[n_p:, SM_VB:SM_VB + DH_B].reshape(dbsz, dseq, DH_B)
    ki_s = sm[n_p:, SM_KI:SM_KI + D_I].reshape(dbsz, dseq, D_I)
    lim_p = (np.arange(seq) // CHUNK + 1) * CHUNK
    ob_p = _dsa(qb, sm, [(kb_p.astype(BF16), vb_p.astype(BF16), ki_p.astype(BF16))], lim_p,
                bsz=bsz, seq=seq, row0=0, tq=DSA_TQ, n_sel=min(TOPK_MAX, seq // 4))
    n_keys = past + dseq
    new_pad = -(-dseq // LANES) * LANES - dseq
    new_keys = tuple(jnp.pad(a.astype(BF16), ((0, 0), (0, new_pad), (0, 0))) for a in (kb_s, vb_s, ki_s))
    lim_s = np.full((dseq,), n_keys)
    ob_s = _dsa(qb, sm, [(cache_k[0], cache_v[0], cache_kidx[0]), new_keys], lim_s,
                bsz=dbsz, seq=dseq, row0=n_p, tq=dseq, n_sel=min(TOPK_MAX, n_keys // 4))

    wrh, wrl = _split_bf16(w_router[0].T)
    x1, eidx, gate, counts = _mix(
        (oa_p, oa_s), (ob_p, ob_s), x_groups, w_oa[0].astype(BF16), w_ob[0].astype(BF16), wgta, wgtb, w_out[0].astype(BF16),
        ln1_g[0].reshape(1, d), ln1_b[0].reshape(1, d), wrh, wrl, b_router[0].reshape(N_EXPERTS, 1),
        tm=MIX_TM, alpha=alpha)

    n_rows = -(-(n * TOP_K + N_EXPERTS * (MOE_BM - 1)) // MOE_BM) * MOE_BM
    dest, blk_e, meta = _route(eidx, counts, tl=ROUTE_TL, sub=MOE_TM, bm=MOE_BM, nb=n_rows // MOE_BM)
    xs = _dispatch(dest, meta, x1, n_rows, tm=MOE_TM, bm=MOE_BM, ne=N_EXPERTS)
    ys = _experts(blk_e[0, :n_rows // MOE_BM], meta[1, :1], meta[0, :N_EXPERTS], xs, w_up[0], b_up[0], w_down[0],
                  b_down[0], bm=MOE_BM)
    out_p, out_s = _combine(dest, ys, gate, x1, ln2_g[0].reshape(1, d), ln2_b[0].reshape(1, d),
                            n_p=n_p, tm=MOE_TM, alpha=alpha)

    return (out_p.reshape(bsz, seq, d), out_s.reshape(dbsz, dseq, d),
            kb_p[None], vb_p[None], ki_p[None], sfin_p[None],
            kb_s[None], vb_s[None], ki_s[None], sfin_s[None])
```

```python
import functools

import jax
import jax.numpy as jnp
import numpy as np
from jax import lax
from jax.experimental import pallas as pl
from jax.experimental.pallas import tpu as pltpu

F32 = jnp.float32
BF16 = jnp.bfloat16
I32 = jnp.int32

CHUNK = 64
H_A = 8
DK_A = 128
DV_A = 128
HGRN_BLOCK = 16
H_B = 8
DH_B = 128
H_I = 8
D_I = 64
TOPK_MAX = 256
N_EXPERTS = 32
TOP_K = 4
SWIGLU_ALPHA = 1.702
SWIGLU_LIMIT = 7.0
EPS = 1e-5

LANES = 128
SUBLANES = 8
MXU_ROWS = 256
VMEM_LIMIT_BYTES = 56 * 1024 * 1024

INT_MIN = -(2 ** 31)


def _cparams(sem):
    return pltpu.CompilerParams(dimension_semantics=sem, vmem_limit_bytes=VMEM_LIMIT_BYTES)


def _sigmoid(x):
    return 1.0 / (1.0 + jnp.exp(-x))


def _layer_norm(x, g, b):
    mu = jnp.mean(x, axis=-1, keepdims=True)
    xc = x - mu
    var = jnp.mean(xc * xc, axis=-1, keepdims=True)
    return xc * lax.rsqrt(var + EPS) * g + b


def _two_group_specs(tm, width, nt_p):
    return (pl.BlockSpec((tm, width), lambda i: (jnp.minimum(i, nt_p - 1), 0)),
            pl.BlockSpec((tm, width), lambda i: (jnp.maximum(i - nt_p, 0), 0)))


def _group_tile(p_ref, s_ref, nt_p):
    return jnp.where(pl.program_id(0) < nt_p, p_ref[...], s_ref[...])


def _proj_kernel(xp_ref, xs_ref, wqa_ref, wfa_ref, wia_ref, wga_ref, wqb_ref, wsm_ref, lbl_ref,
                 qa_ref, f_ref, ia_ref, sg_ref, qb_ref, sm_ref, *, nt_p):
    xb = _group_tile(xp_ref, xs_ref, nt_p).astype(BF16)
    dot = lambda w_ref: jnp.dot(xb, w_ref[...], preferred_element_type=F32)
    qa_ref[...] = dot(wqa_ref)
    lbl = lbl_ref[...]
    lbe = jnp.exp(lbl - jnp.max(lbl, axis=0, keepdims=True))
    lb = lbe[0:1, :] / jnp.sum(lbe, axis=0, keepdims=True)
    f_ref[...] = lb + (1.0 - lb) * _sigmoid(dot(wfa_ref))
    ia_ref[...] = dot(wia_ref)
    g = dot(wga_ref)
    sg_ref[...] = g * _sigmoid(g)
    qb_ref[...] = dot(wqb_ref).astype(BF16)
    sm_ref[...] = dot(wsm_ref)


def _proj(x_p, x_s, wqa, wfa, wia, wga, wqb, wsm, lb_logits, tm):
    d = x_p.shape[1]
    n = x_p.shape[0] + x_s.shape[0]
    nt_p = x_p.shape[0] // tm
    nsm = wsm.shape[1]
    row = lambda w: pl.BlockSpec((tm, w), lambda i: (i, 0))
    wspec = lambda w: pl.BlockSpec(w.shape, lambda i: (0, 0))
    return pl.pallas_call(
        functools.partial(_proj_kernel, nt_p=nt_p),
        grid=(n // tm,),
        in_specs=[*_two_group_specs(tm, d, nt_p), wspec(wqa), wspec(wfa), wspec(wia), wspec(wga), wspec(wqb),
                  wspec(wsm), wspec(lb_logits)],
        out_specs=[row(d), row(d), row(d), row(d), row(d), row(nsm)],
        out_shape=[jax.ShapeDtypeStruct((n, d), F32)] * 4
        + [jax.ShapeDtypeStruct((n, d), BF16), jax.ShapeDtypeStruct((n, nsm), F32)],
        compiler_params=_cparams(("arbitrary",)),
        name="proj",
    )(x_p, x_s, wqa, wfa, wia, wga, wqb, wsm, lb_logits)


HGRN_UNROLL = 16


def _block_cumsum(a, row_in_block):
    s = 1
    while s < HGRN_BLOCK:
        a = a + jnp.where(row_in_block >= s, pltpu.roll(a, s, 0), 0.0)
        s *= 2
    return a


def _hgrn_kernel(qa_ref, f_ref, ia_ref, sg_ref, gn_ref, s0_ref, o_ref, sfin_ref,
                 s_scr, qd_scr, qm_scr, km_scr, kl_scr, v_scr, dl_scr, o_scr):
    j = pl.program_id(1)
    tl = qa_ref.shape[0]
    nblk = tl // HGRN_BLOCK
    half = HGRN_BLOCK // 2

    @pl.when(j == 0)
    def _():
        for h in range(H_A):
            s_scr[h] = s0_ref[0, h].T

    f = f_ref[...]
    q = qa_ref[...]
    d = f.shape[1]
    row_in_block = lax.broadcasted_iota(I32, (tl, d), 0) % HGRN_BLOCK
    cum = _block_cumsum(jnp.log(f), row_in_block)
    cum3 = cum.reshape(nblk, HGRN_BLOCK, d)
    mid = jnp.broadcast_to(cum3[:, half:half + 1, :], cum3.shape).reshape(tl, d)
    last3 = cum3[:, HGRN_BLOCK - 1:HGRN_BLOCK, :]
    last = jnp.broadcast_to(last3, cum3.shape).reshape(tl, d)
    k = 1.0 - f
    qd_scr[...] = (q * jnp.exp(cum)).astype(BF16)
    qm_scr[...] = (q * jnp.exp(cum - mid)).astype(BF16)
    km_scr[...] = (k * jnp.exp(mid - cum)).astype(BF16)
    kl_scr[...] = (k * jnp.exp(last - cum)).astype(BF16)
    dl_scr[...] = jnp.exp(last3.reshape(nblk, d))
    v_scr[...] = ia_ref[...].astype(BF16)

    nt = (((1,), (1,)), ((), ()))
    ri = lax.broadcasted_iota(I32, (tl, tl), 0)
    ci = lax.broadcasted_iota(I32, (tl, tl), 1)
    keep = (ri // HGRN_BLOCK == ci // HGRN_BLOCK) & (ri >= ci)
    for h in range(H_A):
        cols = slice(h * DK_A, (h + 1) * DK_A)
        vcols = slice(h * DV_A, (h + 1) * DV_A)
        att = lax.dot_general(qm_scr[:, cols], km_scr[:, cols], nt, preferred_element_type=F32)
        att = jnp.where(keep, att, 0.0).astype(BF16)
        o_scr[:, vcols] = jnp.dot(att, v_scr[:, vcols], preferred_element_type=F32)

    def block(b, carry):
        r0 = pl.multiple_of(b * HGRN_BLOCK, HGRN_BLOCK)
        rows = pl.ds(r0, HGRN_BLOCK)
        dl = dl_scr[pl.ds(b, 1), :]
        for h in range(H_A):
            cols = slice(h * DK_A, (h + 1) * DK_A)
            vcols = slice(h * DV_A, (h + 1) * DV_A)
            st_h = s_scr[h]
            o_scr[rows, vcols] += lax.dot_general(qd_scr[rows, cols], st_h.astype(BF16), nt,
                                                  preferred_element_type=F32)
            upd_t = lax.dot_general(v_scr[rows, vcols], kl_scr[rows, cols], (((0,), (0,)), ((), ())),
                                    preferred_element_type=F32)
            s_scr[h] = dl[:, cols] * st_h + upd_t
        return carry

    lax.fori_loop(0, nblk, block, 0, unroll=HGRN_UNROLL if nblk % HGRN_UNROLL == 0 else 1)

    gn = gn_ref[...]
    for h in range(H_A):
        vcols = slice(h * DV_A, (h + 1) * DV_A)
        o = o_scr[:, vcols]
        o = o * lax.rsqrt(jnp.mean(o * o, axis=-1, keepdims=True) + EPS) * gn
        o_ref[:, vcols] = (o * sg_ref[:, vcols]).astype(BF16)

    @pl.when(j == pl.num_programs(1) - 1)
    def _():
        for h in range(H_A):
            sfin_ref[0, h] = s_scr[h].T


def _hgrn(qa, f, ia, sg, gn, s0, *, bsz, seq, row0, tl):
    d = qa.shape[1]
    nt = seq // tl
    blk0 = row0 // tl
    row = pl.BlockSpec((tl, d), lambda b, j: (blk0 + b * nt + j, 0))
    orow = pl.BlockSpec((tl, d), lambda b, j: (b * nt + j, 0))
    sspec = pl.BlockSpec((1, H_A, DK_A, DV_A), lambda b, j: (b, 0, 0, 0))
    return pl.pallas_call(
        _hgrn_kernel,
        grid=(bsz, nt),
        in_specs=[row, row, row, row, pl.BlockSpec((1, DV_A), lambda b, j: (0, 0)), sspec],
        out_specs=[orow, sspec],
        out_shape=[jax.ShapeDtypeStruct((bsz * seq, d), BF16),
                   jax.ShapeDtypeStruct((bsz, H_A, DK_A, DV_A), F32)],
        scratch_shapes=[pltpu.VMEM((H_A, DK_A, DV_A), F32)]
        + [pltpu.VMEM((tl, d), BF16)] * 5
        + [pltpu.VMEM((tl // HGRN_BLOCK, d), F32), pltpu.VMEM((tl, d), F32)],
        compiler_params=_cparams(("arbitrary", "arbitrary")),
        name="hgrn",
    )(qa, f, ia, sg, gn, s0)


SM_KB, SM_VB, SM_QI, SM_KI, SM_WI = 0, DH_B, 2 * DH_B, 2 * DH_B + H_I * D_I, 2 * DH_B + H_I * D_I + D_I
SM_USED = SM_WI + H_I
SM_WIDTH = -(-SM_USED // LANES) * LANES
INDEX_SCALE = (H_I * D_I) ** -0.5
NEG_INF = float("-inf")
LOG2_E = 1.4426950408889634


F32_EXP_MASK = 0x7F800000
F32_MIN_NORMAL = 0x00800000
SEARCH_UNROLL = 4


def _key_to_float(u):
    key = u ^ jnp.int32(INT_MIN)
    bits = jnp.where(key < 0, key ^ jnp.int32(0x7FFFFFFF), key)
    below_neg_inf = (bits < 0) & ((bits & jnp.int32(0x7FFFFFFF)) > jnp.int32(F32_EXP_MASK))
    return jnp.where(below_neg_inf, NEG_INF, lax.bitcast_convert_type(bits, F32))


def _positive_bits_to_float(bits):
    return jnp.where(bits < jnp.int32(F32_MIN_NORMAL), 0.0, lax.bitcast_convert_type(bits, F32))


def _count(mask):
    return jnp.sum(jnp.where(mask, 1.0, 0.0), axis=1, keepdims=True)


def _select_bias(score_scr, bias_scr, adm, s, n_sel):
    tq = score_scr.shape[0]

    def resolve_bit(i, t_u, cnt_t):
        cand_u = t_u | lax.shift_left(jnp.int32(1), 31 - i)
        cnt = _count(score_scr[:, :s] >= _key_to_float(cand_u))
        ok = cnt >= n_sel
        return jnp.where(ok, cand_u, t_u), jnp.where(ok, cnt, cnt_t)

    def unresolved(c):
        i, _, cnt_t = c
        return jnp.logical_and(i < 32, jnp.any(cnt_t != n_sel))

    def resolve_bits(c):
        i, t_u, cnt_t = c
        for b in range(SEARCH_UNROLL):
            t_u, cnt_t = resolve_bit(i + b, t_u, cnt_t)
        return i + SEARCH_UNROLL, t_u, cnt_t

    _, t_u, cnt_t = lax.while_loop(unresolved, resolve_bits,
                                   (jnp.int32(0), jnp.zeros((tq, 1), I32), jnp.full((tq, 1), float(s), F32)))
    t_f = _key_to_float(t_u)
    score = score_scr[:, :s]
    bias_scr[:, :s] = jnp.where((score >= t_f) & adm, 0.0, NEG_INF)

    @pl.when(jnp.any(cnt_t > n_sel))
    def _():
        above = score >= _key_to_float(t_u + 1)
        bucket = (score >= t_f) & jnp.logical_not(above)
        need = n_sel - _count(above)
        off = jnp.where(bucket, score - jnp.where(t_f == NEG_INF, 0.0, t_f), -1.0)

        def resolve_offset_bit(i, r_bits):
            cand = r_bits | lax.shift_left(jnp.int32(1), 30 - i)
            ok = _count(off >= _positive_bits_to_float(cand)) >= need
            return jnp.where(ok, cand, r_bits)

        ranked = jnp.any(off > 0.0)
        r_bits = lax.fori_loop(0, jnp.where(ranked, 31, 0), resolve_offset_bit,
                               jnp.full((tq, 1), jnp.where(ranked, 0, F32_MIN_NORMAL - 1), I32))
        above2 = off >= _positive_bits_to_float(r_bits + 1)
        tie = (off >= _positive_bits_to_float(r_bits)) & jnp.logical_not(above2)
        need2 = need - _count(above2)
        upper = (lax.broadcasted_iota(I32, (LANES, LANES), 0)
                 <= lax.broadcasted_iota(I32, (LANES, LANES), 1)).astype(BF16)
        carry = jnp.zeros((tq, 1), F32)
        for c in range(s // LANES):
            cs = slice(c * LANES, (c + 1) * LANES)
            rank = jnp.dot(jnp.where(tie[:, cs], 1.0, 0.0).astype(BF16), upper, preferred_element_type=F32) + carry
            sel = (above[:, cs] | above2[:, cs] | (tie[:, cs] & (rank <= need2))) & adm[:, cs]
            bias_scr[:, cs] = jnp.where(sel, 0.0, NEG_INF)
            carry = rank[:, LANES - 1:LANES]


def _dsa_tile(qb_ref, sm_ref, *refs, segs, search, n_sel):
    key_refs, (lim_ref, o_ref, score_scr, bias_scr) = refs[:3 * len(segs)], refs[3 * len(segs):]
    tq = qb_ref.shape[0]
    s = sum(segs)
    nt = (((1,), (1,)), ((), ()))
    adm = lax.broadcasted_iota(I32, (tq, s), 1) < lim_ref[...]
    seg_keys = [[key_refs[3 * g + j][0, :sg, :].astype(BF16) for j in range(3)] for g, sg in enumerate(segs)]
    starts = [sum(segs[:g]) for g in range(len(segs))]
    against_keys = lambda lhs, j: jnp.concatenate(
        [lax.dot_general(lhs, kv[j], nt, preferred_element_type=F32) for kv in seg_keys], axis=1)
    times_values = lambda p: sum(jnp.dot(p[:, st:st + sg], kv[1], preferred_element_type=F32)
                                 for st, sg, kv in zip(starts, segs, seg_keys))
    stack = tq * max(H_I, H_B) <= MXU_ROWS
    head_rows = lambda a, h: a[h * tq:(h + 1) * tq]
    qi_head = lambda h: sm_ref[:, SM_QI + h * D_I:SM_QI + (h + 1) * D_I].astype(BF16)
    q_head = lambda h: qb_ref[:, h * DH_B:(h + 1) * DH_B]

    if search:
        if stack:
            stacked = against_keys(jnp.concatenate([qi_head(h) for h in range(H_I)], axis=0), 2)
        score = jnp.zeros((tq, s), F32)
        for h in range(H_I):
            sh = head_rows(stacked, h) if stack else against_keys(qi_head(h), 2)
            wih = sm_ref[:, SM_WI + h:SM_WI + h + 1] * INDEX_SCALE
            score = score + jnp.maximum(sh, 0.0) * wih
        score_scr[:, :s] = jnp.where(adm, score, NEG_INF)
        _select_bias(score_scr, bias_scr, adm, s, n_sel)
    else:
        bias_scr[:, :s] = jnp.where(adm, 0.0, NEG_INF)

    def softmax_terms(logits):
        p = jnp.exp2((logits - jnp.max(logits, axis=1, keepdims=True)) * (DH_B ** -0.5 * LOG2_E))
        return p.astype(BF16), jnp.sum(p, axis=1, keepdims=True)

    if stack:
        stacked = against_keys(jnp.concatenate([q_head(h) for h in range(H_B)], axis=0), 0)
        terms = [softmax_terms(head_rows(stacked, h) + bias_scr[:, :s]) for h in range(H_B)]
        o_all = times_values(jnp.concatenate([p for p, _ in terms], axis=0))
        for h in range(H_B):
            o_ref[:, h * DH_B:(h + 1) * DH_B] = (head_rows(o_all, h) / terms[h][1]).astype(BF16)
    else:
        for h in range(H_B):
            p, denom = softmax_terms(against_keys(q_head(h), 0) + bias_scr[:, :s])
            o_ref[:, h * DH_B:(h + 1) * DH_B] = (times_values(p) / denom).astype(BF16)


DSA_KEY_GRAN = 256


def _dsa(qb, sm, key_segments, limits, *, bsz, seq, row0, tq, n_sel):
    d = qb.shape[1]
    seg_rows = [seg[0].shape[1] for seg in key_segments]
    n_keys = sum(seg_rows)
    nt = seq // tq
    blk0 = row0 // tq
    limits = np.asarray(limits, np.int32)
    tile_max = limits.reshape(nt, tq).max(axis=1)
    per_tile = [(int(min(n_keys, -(-m // DSA_KEY_GRAN) * DSA_KEY_GRAN)), bool(m > n_sel)) for m in tile_max]
    lim2d = jnp.asarray(limits.reshape(seq, 1))
    outs, lo = [], 0
    while lo < nt:
        hi = lo
        while hi + 1 < nt and per_tile[hi + 1] == per_tile[lo]:
            hi += 1
        s, search = per_tile[lo]
        assert s == n_keys or len(key_segments) == 1
        segs = tuple(seg_rows) if s == n_keys else (s,)
        ntv = hi - lo + 1
        row = lambda w, lo=lo: pl.BlockSpec((tq, w), lambda b, j: (blk0 + b * nt + lo + j, 0))
        keys = lambda rows, w: pl.BlockSpec((1, rows, w), lambda b, j: (b, 0, 0))
        out = pl.pallas_call(
            functools.partial(_dsa_tile, segs=segs, search=search, n_sel=n_sel),
            grid=(bsz, ntv),
            in_specs=[row(d), row(sm.shape[1])]
            + [keys(rows, a.shape[2]) for rows, seg in zip(segs, key_segments) for a in seg]
            + [pl.BlockSpec((tq, 1), lambda b, j, lo=lo: (lo + j, 0))],
            out_specs=pl.BlockSpec((tq, d), lambda b, j, ntv=ntv: (b * ntv + j, 0)),
            out_shape=jax.ShapeDtypeStruct((bsz * ntv * tq, d), BF16),
            scratch_shapes=[pltpu.VMEM((tq, s), F32), pltpu.VMEM((tq, s), F32)],
            compiler_params=_cparams(("arbitrary", "arbitrary")),
            name="dsa",
        )(qb, sm, *[a for seg in key_segments for a in seg], lim2d)
        outs.append(out.reshape(bsz, ntv * tq, d))
        lo = hi + 1
    return jnp.concatenate(outs, axis=1).reshape(bsz * seq, d)


def _split_bf16(a):
    hi = a.astype(BF16)
    return hi, (a - hi.astype(F32)).astype(BF16)


def _mix_kernel(oap_ref, oas_ref, obp_ref, obs_ref, xp_ref, xs_ref, woa_ref, wob_ref, wga_ref, wgb_ref, wout_ref,
                g1_ref, b1_ref, wrh_ref, wrl_ref, br_ref, x1_ref, eidx_ref, gate_ref, cnt_ref, *, alpha, nt_p):
    i = pl.program_id(0)
    x = _group_tile(xp_ref, xs_ref, nt_p)
    xb = x.astype(BF16)
    dot = lambda a, w_ref: jnp.dot(a, w_ref[...], preferred_element_type=F32)
    y_a = dot(_group_tile(oap_ref, oas_ref, nt_p), woa_ref)
    y_b = dot(_group_tile(obp_ref, obs_ref, nt_p), wob_ref)
    merged = _sigmoid(dot(xb, wga_ref)) * y_a + _sigmoid(dot(xb, wgb_ref)) * y_b
    mixed = dot(merged.astype(BF16), wout_ref)
    x1 = _layer_norm(alpha * x + mixed, g1_ref[...], b1_ref[...])
    x1_ref[...] = x1

    nt = (((1,), (1,)), ((), ()))
    xh, xl = _split_bf16(x1)
    wh, wl = wrh_ref[...], wrl_ref[...]
    logits = (lax.dot_general(wh, xh, nt, preferred_element_type=F32)
              + lax.dot_general(wh, xl, nt, preferred_element_type=F32)
              + lax.dot_general(wl, xh, nt, preferred_element_type=F32)) + br_ref[...]
    ne, tm = logits.shape
    erow = lax.broadcasted_iota(I32, (ne, tm), 0)
    vals, idxs = [], []
    for _ in range(TOP_K):
        m = jnp.max(logits, axis=0, keepdims=True)
        idx = jnp.min(jnp.where(logits == m, erow, ne), axis=0, keepdims=True)
        vals.append(m)
        idxs.append(idx)
        logits = jnp.where(erow == idx, NEG_INF, logits)
    ex = [jnp.exp(v - vals[0]) for v in vals]
    denom = ex[0] + ex[1] + ex[2] + ex[3]
    pad = SUBLANES - TOP_K
    eidx = jnp.concatenate(idxs + [jnp.zeros((pad, tm), I32)], axis=0)
    eidx_ref[...] = eidx
    gate_ref[...] = jnp.concatenate([e / denom for e in ex] + [jnp.zeros((pad, tm), F32)], axis=0)

    onehot = jnp.zeros((ne, tm), F32)
    for idx in idxs:
        onehot = onehot + jnp.where(erow == idx, 1.0, 0.0)
    tile_cnt = jnp.broadcast_to(jnp.sum(onehot, axis=1, keepdims=True), cnt_ref.shape)

    @pl.when(i == 0)
    def _():
        cnt_ref[...] = tile_cnt

    @pl.when(i > 0)
    def _():
        cnt_ref[...] = cnt_ref[...] + tile_cnt


def _mix(oa, ob, x, woa, wob, wga, wgb, wout, g1, b1, wrh, wrl, br, *, tm, alpha):
    d = x[0].shape[1]
    n = x[0].shape[0] + x[1].shape[0]
    nt_p = x[0].shape[0] // tm
    ne = wrh.shape[0]
    row = lambda w: pl.BlockSpec((tm, w), lambda i: (i, 0))
    col = pl.BlockSpec((SUBLANES, tm), lambda i: (0, i))
    full = lambda a: pl.BlockSpec(a.shape, lambda i: (0,) * a.ndim)
    return pl.pallas_call(
        functools.partial(_mix_kernel, alpha=alpha, nt_p=nt_p),
        grid=(n // tm,),
        in_specs=[*_two_group_specs(tm, d, nt_p)] * 3
        + [full(a) for a in (woa, wob, wga, wgb, wout, g1, b1, wrh, wrl, br)],
        out_specs=[row(d), col, col, pl.BlockSpec((ne, LANES), lambda i: (0, 0))],
        out_shape=[jax.ShapeDtypeStruct((n, d), F32), jax.ShapeDtypeStruct((SUBLANES, n), I32),
                   jax.ShapeDtypeStruct((SUBLANES, n), F32), jax.ShapeDtypeStruct((ne, LANES), F32)],
        compiler_params=_cparams(("arbitrary",)),
        name="mix",
    )(*oa, *ob, *x, woa, wob, wga, wgb, wout, g1, b1, wrh, wrl, br)


def _sublane_cumsum(a):
    n = a.shape[0]
    row = lax.broadcasted_iota(I32, a.shape, 0)
    s = 1
    while s < n:
        a = a + jnp.where(row >= s, pltpu.roll(a, s, 0), 0.0)
        s *= 2
    return a


def _route_kernel(eidx_ref, cnt_ref, dest_ref, blke_ref, meta_ref, carry_scr, *, bm, sub):
    i = pl.program_id(0)
    ne = cnt_ref.shape[0]
    tl = eidx_ref.shape[1]

    @pl.when(i == 0)
    def _():
        carry_scr[...] = jnp.zeros_like(carry_scr)

    counts = cnt_ref[...]
    padded = jnp.ceil(counts / bm) * bm
    pend = _sublane_cumsum(padded)
    pstart = (pend - padded)[:, 0:1]

    @pl.when(i == 0)
    def _():
        nb = blke_ref.shape[1]
        first_row = (lax.broadcasted_iota(I32, (ne, nb), 1) * bm).astype(F32)
        below = jnp.sum(jnp.where(pend[:, 0:1] <= first_row, 1.0, 0.0), axis=0, keepdims=True)
        blke_ref[...] = jnp.minimum(below, ne - 1.0).astype(I32)
        pend_lanes = jnp.concatenate([pend, jnp.zeros((LANES - ne, LANES), F32)], axis=0).T[0:1, :]
        n_used = jnp.max(pend, axis=0, keepdims=True) / bm
        meta_ref[...] = jnp.concatenate([pend_lanes, n_used, jnp.zeros((SUBLANES - 2, LANES), F32)],
                                        axis=0).astype(I32)

    eidx = eidx_ref[...]
    erow = lax.broadcasted_iota(I32, (ne, tl), 0)
    hot = [jnp.where(erow == eidx[k:k + 1, :], 1.0, 0.0) for k in range(TOP_K)]
    onehot = hot[0] + hot[1] + hot[2] + hot[3]
    before = (lax.broadcasted_iota(I32, (tl, tl), 0) < lax.broadcasted_iota(I32, (tl, tl), 1)).astype(BF16)
    base = jnp.dot(onehot.astype(BF16), before, preferred_element_type=F32) + carry_scr[:, 0:1] + pstart
    dest = jnp.concatenate([jnp.sum(hk * base, axis=0, keepdims=True) for hk in hot], axis=0).astype(I32)
    for c in range(tl // sub):
        dest_ref[c] = dest[:, c * sub:(c + 1) * sub]
    carry_scr[...] = carry_scr[...] + jnp.sum(onehot, axis=1, keepdims=True)


def _route(eidx, counts, *, tl, sub, bm, nb):
    n = eidx.shape[1]
    ne = counts.shape[0]
    nb_pad = -(-nb // LANES) * LANES
    return pl.pallas_call(
        functools.partial(_route_kernel, bm=bm, sub=sub),
        grid=(n // tl,),
        in_specs=[pl.BlockSpec((SUBLANES, tl), lambda i: (0, i)), pl.BlockSpec(counts.shape, lambda i: (0, 0))],
        out_specs=[pl.BlockSpec((tl // sub, TOP_K, sub), lambda i: (i, 0, 0)),
                   pl.BlockSpec((1, nb_pad), lambda i: (0, 0)),
                   pl.BlockSpec((SUBLANES, LANES), lambda i: (0, 0))],
        out_shape=[jax.ShapeDtypeStruct((n // sub, TOP_K, sub), I32), jax.ShapeDtypeStruct((1, nb_pad), I32),
                   jax.ShapeDtypeStruct((SUBLANES, LANES), I32)],
        scratch_shapes=[pltpu.VMEM((ne, LANES), F32)],
        compiler_params=_cparams(("arbitrary",)),
        name="route",
    )(eidx, counts)


ROW_COPY_UNROLL = 8
DMA_PRIORITIES = 2


def _dispatch_kernel(dest_ref, meta_ref, x_ref, xs_ref, zbuf, sem, zsem, *, bm, ne, nb):
    i = pl.program_id(0)
    tm = x_ref.shape[0]

    @pl.when(i == 0)
    def _():
        zbuf[...] = jnp.zeros_like(zbuf)
        zero_block = lambda row0: pltpu.make_async_copy(zbuf, xs_ref.at[pl.ds(row0, bm), :], zsem)

        def each_padding_block(action):
            for e in range(ne):
                seg_end = meta_ref[0, e]
                seg_start = meta_ref[0, e - 1] if e > 0 else 0

                @pl.when(seg_end > seg_start)
                def _(seg_end=seg_end):
                    action(zero_block(pl.multiple_of(seg_end - bm, bm)))

            def tail_block(b, carry):
                action(zero_block(pl.multiple_of(b * bm, bm)))
                return carry
            lax.fori_loop(meta_ref[1, 0], nb, tail_block, 0)

        each_padding_block(lambda copy: copy.start())
        each_padding_block(lambda copy: copy.wait())

    def start(t, carry):
        for k in range(TOP_K):
            pltpu.make_async_copy(x_ref.at[pl.ds(t, 1), :], xs_ref.at[pl.ds(dest_ref[0, k, t], 1), :],
                                  sem).start(priority=k % DMA_PRIORITIES)
        return carry
    lax.fori_loop(0, tm, start, 0, unroll=ROW_COPY_UNROLL)

    for _ in range(TOP_K):
        pltpu.make_async_copy(x_ref, xs_ref.at[pl.ds(0, tm), :], sem).wait()


def _dispatch(dest, meta, x1, n_rows, *, tm, bm, ne):
    n, d = x1.shape
    return pl.pallas_call(
        functools.partial(_dispatch_kernel, bm=bm, ne=ne, nb=n_rows // bm),
        grid=(n // tm,),
        in_specs=[pl.BlockSpec((1, TOP_K, tm), lambda i: (i, 0, 0), memory_space=pltpu.SMEM),
                  pl.BlockSpec(meta.shape, lambda i: (0, 0), memory_space=pltpu.SMEM),
                  pl.BlockSpec((tm, d), lambda i: (i, 0))],
        out_specs=pl.BlockSpec(memory_space=pl.ANY),
        out_shape=jax.ShapeDtypeStruct((n_rows, d), x1.dtype),
        scratch_shapes=[pltpu.VMEM((bm, d), x1.dtype), pltpu.SemaphoreType.DMA(()), pltpu.SemaphoreType.DMA(())],
        compiler_params=_cparams(("arbitrary",)),
        name="dispatch",
    )(dest, meta, x1)


def _experts_kernel(blke_ref, nused_ref, segend_ref, xs_ref, wu_ref, bu_ref, wd_ref, bd_ref, ys_ref,
                    wu32, wd32, wu16, wd16, sems):
    i = pl.program_id(0)
    bm = xs_ref.shape[0]
    dff = wd16.shape[0]
    n_used = nused_ref[0]

    def weight_copies(e):
        return (pltpu.make_async_copy(wu_ref.at[e], wu32, sems.at[0]),
                pltpu.make_async_copy(wd_ref.at[e], wd32, sems.at[1]))

    @pl.when(i < n_used)
    def _():
        e_cur = blke_ref[i]

        @pl.when(jnp.logical_or(i == 0, e_cur != blke_ref[jnp.maximum(i - 1, 0)]))
        def _():
            @pl.when(i == 0)
            def _():
                for copy in weight_copies(e_cur):
                    copy.start()

            for copy in weight_copies(e_cur):
                copy.wait()
            wu16[...] = wu32[...].astype(BF16)
            wd16[...] = wd32[...].astype(BF16)

            nxt = segend_ref[e_cur] // bm

            @pl.when(nxt < n_used)
            def _():
                for copy in weight_copies(blke_ref[jnp.minimum(nxt, blke_ref.shape[0] - 1)]):
                    copy.start()

        h = jnp.dot(xs_ref[...].astype(BF16), wu16[...], preferred_element_type=F32) + bu_ref[0]
        glu = jnp.minimum(h[:, :dff], SWIGLU_LIMIT)
        lin = jnp.clip(h[:, dff:], -SWIGLU_LIMIT, SWIGLU_LIMIT)
        act = glu * _sigmoid(SWIGLU_ALPHA * glu) * (lin + 1.0)
        ys_ref[...] = jnp.dot(act.astype(BF16), wd16[...], preferred_element_type=F32) + bd_ref[0]

    @pl.when(i >= nused_ref[0])
    def _():
        ys_ref[...] = jnp.zeros_like(ys_ref)


def _experts(blk_e, n_used, seg_end, xs, wu, bu, wd, bd, *, bm):
    n_rows, d = xs.shape
    ne, _, dff2 = wu.shape
    dff = wd.shape[1]
    used = lambda i, nu: jnp.minimum(i, nu[0] - 1)
    grid_spec = pltpu.PrefetchScalarGridSpec(
        num_scalar_prefetch=3,
        grid=(n_rows // bm,),
        in_specs=[pl.BlockSpec((bm, d), lambda i, e, nu, se: (used(i, nu), 0)),
                  pl.BlockSpec(memory_space=pl.ANY),
                  pl.BlockSpec((1, 1, dff2), lambda i, e, nu, se: (e[i], 0, 0)),
                  pl.BlockSpec(memory_space=pl.ANY),
                  pl.BlockSpec((1, 1, d), lambda i, e, nu, se: (e[i], 0, 0))],
        out_specs=pl.BlockSpec((bm, d), lambda i, e, nu, se: (i, 0)),
        scratch_shapes=[pltpu.VMEM((d, dff2), F32), pltpu.VMEM((dff, d), F32),
                        pltpu.VMEM((d, dff2), BF16), pltpu.VMEM((dff, d), BF16),
                        pltpu.SemaphoreType.DMA((2,))],
    )
    return pl.pallas_call(
        _experts_kernel,
        grid_spec=grid_spec,
        out_shape=jax.ShapeDtypeStruct((n_rows, d), F32),
        compiler_params=_cparams(("arbitrary",)),
        name="experts",
    )(blk_e, n_used, seg_end, xs, wu, bu.reshape(ne, 1, dff2), wd, bd.reshape(ne, 1, d))


def _combine_kernel(dest_ref, ys_ref, gate_ref, x1_ref, g2_ref, b2_ref, outp_ref, outs_ref, buf, sem, *, alpha, nt_p):
    i = pl.program_id(0)
    tm = x1_ref.shape[0]

    def start(t, carry):
        for k in range(TOP_K):
            pltpu.make_async_copy(ys_ref.at[pl.ds(dest_ref[0, k, t], 1), :], buf.at[k, pl.ds(t, 1), :],
                                  sem).start(priority=k % DMA_PRIORITIES)
        return carry
    lax.fori_loop(0, tm, start, 0, unroll=ROW_COPY_UNROLL)

    for k in range(TOP_K):
        pltpu.make_async_copy(ys_ref.at[pl.ds(0, tm), :], buf.at[k], sem).wait()

    gate = gate_ref[...].T
    y = buf[0] * gate[:, 0:1]
    for k in range(1, TOP_K):
        y = y + buf[k] * gate[:, k:k + 1]
    out = _layer_norm(alpha * x1_ref[...] + y, g2_ref[...], b2_ref[...])

    @pl.when(i < nt_p)
    def _():
        outp_ref[...] = out

    @pl.when(i >= nt_p)
    def _():
        outs_ref[...] = out


def _combine(dest, ys, gate, x1, g2, b2, *, n_p, tm, alpha):
    n, d = x1.shape
    nt_p = n_p // tm
    return pl.pallas_call(
        functools.partial(_combine_kernel, alpha=alpha, nt_p=nt_p),
        grid=(n // tm,),
        in_specs=[pl.BlockSpec((1, TOP_K, tm), lambda i: (i, 0, 0), memory_space=pltpu.SMEM),
                  pl.BlockSpec(memory_space=pl.ANY),
                  pl.BlockSpec((SUBLANES, tm), lambda i: (0, i)),
                  pl.BlockSpec((tm, d), lambda i: (i, 0)),
                  pl.BlockSpec((1, d), lambda i: (0, 0)),
                  pl.BlockSpec((1, d), lambda i: (0, 0))],
        out_specs=[pl.BlockSpec((tm, d), lambda i: (jnp.minimum(i, nt_p - 1), 0)),
                   pl.BlockSpec((tm, d), lambda i: (jnp.maximum(i - nt_p, 0), 0))],
        out_shape=[jax.ShapeDtypeStruct((n_p, d), F32), jax.ShapeDtypeStruct((n - n_p, d), F32)],
        scratch_shapes=[pltpu.VMEM((TOP_K, tm, d), F32), pltpu.SemaphoreType.DMA(())],
        compiler_params=_cparams(("arbitrary",)),
        name="combine",
    )(dest, ys, gate, x1, g2, b2)


PROJ_TM = 256
HGRN_TL = 256
DSA_TQ = 256
MIX_TM = 256
ROUTE_TL = 1280
MOE_TM = 256
MOE_BM = 512


def kernel(x_prompt, x_sample, cache_k, cache_v, cache_kidx, state_hgrn, w_in, lb_logits, gn_a, w_oa, w_ob,
           w_out, ln1_g, ln1_b, w_router, b_router, w_up, b_up, w_down, b_down, ln2_g, ln2_b):
    depth = w_in.shape[0]
    assert depth == 1
    bsz, seq, d = x_prompt.shape
    dbsz, dseq, _ = x_sample.shape
    past = cache_k.shape[2]
    n_p, n_s = bsz * seq, dbsz * dseq
    n = n_p + n_s
    alpha = (2 * depth) ** 0.25

    sizes = (H_A * DK_A, H_A * DK_A, H_A * DV_A, H_A * DV_A, H_B * DH_B, DH_B, DH_B, H_I * D_I, H_I, D_I, d, d)
    offs = np.concatenate([[0], np.cumsum(sizes)])
    grp = lambda g: w_in[0, :, offs[g]:offs[g + 1]].astype(BF16)
    wqa, wfa, wia, wg, wqb, wkb, wvb, wqi, wwi, wki, wgta, wgtb = (grp(g) for g in range(12))
    wsm = jnp.concatenate([wkb, wvb, wqi, wki, wwi, jnp.zeros((d, SM_WIDTH - SM_USED), BF16)], axis=1)

    x_groups = (x_prompt.reshape(n_p, d), x_sample.reshape(n_s, d))
    qa, f, ia, sg, qb, sm = _proj(*x_groups, wqa, wfa, wia, wg, wqb, wsm, lb_logits, PROJ_TM)

    gn = gn_a[0].reshape(1, DV_A)
    oa_p, sfin_p = _hgrn(qa, f, ia, sg, gn, jnp.zeros((bsz, H_A, DK_A, DV_A), F32),
                         bsz=bsz, seq=seq, row0=0, tl=HGRN_TL)
    oa_s, sfin_s = _hgrn(qa, f, ia, sg, gn, state_hgrn[0], bsz=dbsz, seq=dseq, row0=n_p, tl=dseq)

    kb_p = sm[:n_p, SM_KB:SM_KB + DH_B].reshape(bsz, seq, DH_B)
    vb_p = sm[:n_p, SM_VB:SM_VB + DH_B].reshape(bsz, seq, DH_B)
    ki_p = sm[:n_p, SM_KI:SM_KI + D_I].reshape(bsz, seq, D_I)
    kb_s = sm[n_p:, SM_KB:SM_KB + DH_B].reshape(dbsz, dseq, DH_B)
    vb_s = sm[n_p:, SM_VB:SM_VB + DH_B].reshape(dbsz, dseq, DH_B)
    ki_s = sm[n_p:, SM_KI:SM_KI + D_I].reshape(dbsz, dseq, D_I)
    lim_p = (np.arange(seq) // CHUNK + 1) * CHUNK
    ob_p = _dsa(qb, sm, [(kb_p.astype(BF16), vb_p.astype(BF16), ki_p.astype(BF16))], lim_p,
                bsz=bsz, seq=seq, row0=0, tq=DSA_TQ, n_sel=min(TOPK_MAX, seq // 4))
    n_keys = past + dseq
    new_pad = -(-dseq // LANES) * LANES - dseq
    new_keys = tuple(jnp.pad(a.astype(BF16), ((0, 0), (0, new_pad), (0, 0))) for a in (kb_s, vb_s, ki_s))
    lim_s = np.full((dseq,), n_keys)
    ob_s = _dsa(qb, sm, [(cache_k[0], cache_v[0], cache_kidx[0]), new_keys], lim_s,
                bsz=dbsz, seq=dseq, row0=n_p, tq=dseq, n_sel=min(TOPK_MAX, n_keys // 4))

    wrh, wrl = _split_bf16(w_router[0].T)
    x1, eidx, gate, counts = _mix(
        (oa_p, oa_s), (ob_p, ob_s), x_groups, w_oa[0].astype(BF16), w_ob[0].astype(BF16), wgta, wgtb, w_out[0].astype(BF16),
        ln1_g[0].reshape(1, d), ln1_b[0].reshape(1, d), wrh, wrl, b_router[0].reshape(N_EXPERTS, 1),
        tm=MIX_TM, alpha=alpha)

    n_rows = -(-(n * TOP_K + N_EXPERTS * (MOE_BM - 1)) // MOE_BM) * MOE_BM
    dest, blk_e, meta = _route(eidx, counts, tl=ROUTE_TL, sub=MOE_TM, bm=MOE_BM, nb=n_rows // MOE_BM)
    xs = _dispatch(dest, meta, x1, n_rows, tm=MOE_TM, bm=MOE_BM, ne=N_EXPERTS)
    ys = _experts(blk_e[0, :n_rows // MOE_BM], meta[1, :1], meta[0, :N_EXPERTS], xs, w_up[0], b_up[0], w_down[0],
                  b_down[0], bm=MOE_BM)
    out_p, out_s = _combine(dest, ys, gate, x1, ln2_g[0].reshape(1, d), ln2_b[0].reshape(1, d),
                            n_p=n_p, tm=MOE_TM, alpha=alpha)

    return (out_p.reshape(bsz, seq, d), out_s.reshape(dbsz, dseq, d),
            kb_p[None], vb_p[None], ki_p[None], sfin_p[None],
            kb_s[None], vb_s[None], ki_s[None], sfin_s[None])
```

```python
import functools

import jax
import jax.numpy as jnp
import numpy as np
from jax import lax
from jax.experimental import pallas as pl
from jax.experimental.pallas import tpu as pltpu

F32 = jnp.float32
BF16 = jnp.bfloat16
I32 = jnp.int32

CHUNK = 64
H_A = 8
DK_A = 128
DV_A = 128
HGRN_BLOCK = 16
H_B = 8
DH_B = 128
H_I = 8
D_I = 64
TOPK_MAX = 256
N_EXPERTS = 32
TOP_K = 4
SWIGLU_ALPHA = 1.702
SWIGLU_LIMIT = 7.0
EPS = 1e-5

LANES = 128
SUBLANES = 8
MXU_ROWS = 256
VMEM_LIMIT_BYTES = 56 * 1024 * 1024

INT_MIN = -(2 ** 31)


def _cparams(sem):
    return pltpu.CompilerParams(dimension_semantics=sem, vmem_limit_bytes=VMEM_LIMIT_BYTES)


def _sigmoid(x):
    return 1.0 / (1.0 + jnp.exp(-x))


def _layer_norm(x, g, b):
    mu = jnp.mean(x, axis=-1, keepdims=True)
    xc = x - mu
    var = jnp.mean(xc * xc, axis=-1, keepdims=True)
    return xc * lax.rsqrt(var + EPS) * g + b


def _two_group_specs(tm, width, nt_p):
    return (pl.BlockSpec((tm, width), lambda i: (jnp.minimum(i, nt_p - 1), 0)),
            pl.BlockSpec((tm, width), lambda i: (jnp.maximum(i - nt_p, 0), 0)))


def _group_tile(p_ref, s_ref, nt_p):
    return jnp.where(pl.program_id(0) < nt_p, p_ref[...], s_ref[...])


def _proj_kernel(xp_ref, xs_ref, wqa_ref, wfa_ref, wia_ref, wga_ref, wqb_ref, wsm_ref, lbl_ref,
                 qd_ref, qm_ref, km_ref, kl_ref, cum_ref, ia_ref, sg_ref, qb_ref, sm_ref, *, nt_p):
    xb = _group_tile(xp_ref, xs_ref, nt_p).astype(BF16)
    dot = lambda w_ref: jnp.dot(xb, w_ref[...], preferred_element_type=F32)
    q = dot(wqa_ref)
    lbl = lbl_ref[...]
    lbe = jnp.exp(lbl - jnp.max(lbl, axis=0, keepdims=True))
    lb = lbe[0:1, :] / jnp.sum(lbe, axis=0, keepdims=True)
    f = lb + (1.0 - lb) * _sigmoid(dot(wfa_ref))
    tm, d = q.shape
    nblk, half = tm // HGRN_BLOCK, HGRN_BLOCK // 2
    cum = _block_cumsum(jnp.log(f), lax.broadcasted_iota(I32, (tm, d), 0) % HGRN_BLOCK)
    cum3 = cum.reshape(nblk, HGRN_BLOCK, d)
    mid = jnp.broadcast_to(cum3[:, half:half + 1, :], cum3.shape).reshape(tm, d)
    last = jnp.broadcast_to(cum3[:, HGRN_BLOCK - 1:HGRN_BLOCK, :], cum3.shape).reshape(tm, d)
    k = 1.0 - f
    qd_ref[...] = (q * jnp.exp(cum)).astype(BF16)
    qm_ref[...] = (q * jnp.exp(cum - mid)).astype(BF16)
    km_ref[...] = (k * jnp.exp(mid - cum)).astype(BF16)
    kl_ref[...] = (k * jnp.exp(last - cum)).astype(BF16)
    cum_ref[...] = cum
    ia_ref[...] = dot(wia_ref).astype(BF16)
    g = dot(wga_ref)
    sg_ref[...] = g * _sigmoid(g)
    qb_ref[...] = dot(wqb_ref).astype(BF16)
    sm_ref[...] = dot(wsm_ref)


def _proj(x_p, x_s, wqa, wfa, wia, wga, wqb, wsm, lb_logits, tm):
    d = x_p.shape[1]
    n = x_p.shape[0] + x_s.shape[0]
    nt_p = x_p.shape[0] // tm
    nsm = wsm.shape[1]
    row = lambda w: pl.BlockSpec((tm, w), lambda i: (i, 0))
    wspec = lambda w: pl.BlockSpec(w.shape, lambda i: (0, 0))
    return pl.pallas_call(
        functools.partial(_proj_kernel, nt_p=nt_p),
        grid=(n // tm,),
        in_specs=[*_two_group_specs(tm, d, nt_p), wspec(wqa), wspec(wfa), wspec(wia), wspec(wga), wspec(wqb),
                  wspec(wsm), wspec(lb_logits)],
        out_specs=[row(d)] * 8 + [row(nsm)],
        out_shape=[jax.ShapeDtypeStruct((n, d), dt) for dt in (BF16, BF16, BF16, BF16, F32, BF16, F32, BF16)]
        + [jax.ShapeDtypeStruct((n, nsm), F32)],
        compiler_params=_cparams(("arbitrary",)),
        name="proj",
    )(x_p, x_s, wqa, wfa, wia, wga, wqb, wsm, lb_logits)


HGRN_UNROLL = 16


def _block_cumsum(a, row_in_block):
    s = 1
    while s < HGRN_BLOCK:
        a = a + jnp.where(row_in_block >= s, pltpu.roll(a, s, 0), 0.0)
        s *= 2
    return a


def _hgrn_kernel(qd_scr, qm_scr, km_scr, kl_scr, cum_ref, v_scr, sg_ref, gn_ref, s0_ref, o_ref, sfin_ref,
                 s_scr, dl_scr, o_scr):
    j = pl.program_id(1)
    tl, d = cum_ref.shape
    nblk = tl // HGRN_BLOCK

    @pl.when(j == 0)
    def _():
        for h in range(H_A):
            s_scr[h] = s0_ref[0, h].T

    last3 = cum_ref[...].reshape(nblk, HGRN_BLOCK, d)[:, HGRN_BLOCK - 1:HGRN_BLOCK, :]
    dl_scr[...] = jnp.exp(last3.reshape(nblk, d))

    nt = (((1,), (1,)), ((), ()))
    ri = lax.broadcasted_iota(I32, (tl, tl), 0)
    ci = lax.broadcasted_iota(I32, (tl, tl), 1)
    keep = (ri // HGRN_BLOCK == ci // HGRN_BLOCK) & (ri >= ci)
    for h in range(H_A):
        cols = slice(h * DK_A, (h + 1) * DK_A)
        vcols = slice(h * DV_A, (h + 1) * DV_A)
        att = lax.dot_general(qm_scr[:, cols], km_scr[:, cols], nt, preferred_element_type=F32)
        att = jnp.where(keep, att, 0.0).astype(BF16)
        o_scr[:, vcols] = jnp.dot(att, v_scr[:, vcols], preferred_element_type=F32)

    def block(b, carry):
        r0 = pl.multiple_of(b * HGRN_BLOCK, HGRN_BLOCK)
        rows = pl.ds(r0, HGRN_BLOCK)
        dl = dl_scr[pl.ds(b, 1), :]
        for h in range(H_A):
            cols = slice(h * DK_A, (h + 1) * DK_A)
            vcols = slice(h * DV_A, (h + 1) * DV_A)
            st_h = s_scr[h]
            o_scr[rows, vcols] += lax.dot_general(qd_scr[rows, cols], st_h.astype(BF16), nt,
                                                  preferred_element_type=F32)
            upd_t = lax.dot_general(v_scr[rows, vcols], kl_scr[rows, cols], (((0,), (0,)), ((), ())),
                                    preferred_element_type=F32)
            s_scr[h] = dl[:, cols] * st_h + upd_t
        return carry

    lax.fori_loop(0, nblk, block, 0, unroll=HGRN_UNROLL if nblk % HGRN_UNROLL == 0 else 1)

    gn = gn_ref[...]
    for h in range(H_A):
        vcols = slice(h * DV_A, (h + 1) * DV_A)
        o = o_scr[:, vcols]
        o = o * lax.rsqrt(jnp.mean(o * o, axis=-1, keepdims=True) + EPS) * gn
        o_ref[:, vcols] = (o * sg_ref[:, vcols]).astype(BF16)

    @pl.when(j == pl.num_programs(1) - 1)
    def _():
        for h in range(H_A):
            sfin_ref[0, h] = s_scr[h].T


def _hgrn(qd, qm, km, kl, cum, v, sg, gn, s0, *, bsz, seq, row0, tl):
    d = cum.shape[1]
    nt = seq // tl
    blk0 = row0 // tl
    row = pl.BlockSpec((tl, d), lambda b, j: (blk0 + b * nt + j, 0))
    orow = pl.BlockSpec((tl, d), lambda b, j: (b * nt + j, 0))
    sspec = pl.BlockSpec((1, H_A, DK_A, DV_A), lambda b, j: (b, 0, 0, 0))
    return pl.pallas_call(
        _hgrn_kernel,
        grid=(bsz, nt),
        in_specs=[row] * 7 + [pl.BlockSpec((1, DV_A), lambda b, j: (0, 0)), sspec],
        out_specs=[orow, sspec],
        out_shape=[jax.ShapeDtypeStruct((bsz * seq, d), BF16),
                   jax.ShapeDtypeStruct((bsz, H_A, DK_A, DV_A), F32)],
        scratch_shapes=[pltpu.VMEM((H_A, DK_A, DV_A), F32),
                        pltpu.VMEM((tl // HGRN_BLOCK, d), F32), pltpu.VMEM((tl, d), F32)],
        compiler_params=_cparams(("arbitrary", "arbitrary")),
        name="hgrn",
    )(qd, qm, km, kl, cum, v, sg, gn, s0)


SM_KB, SM_VB, SM_QI, SM_KI, SM_WI = 0, DH_B, 2 * DH_B, 2 * DH_B + H_I * D_I, 2 * DH_B + H_I * D_I + D_I
SM_USED = SM_WI + H_I
SM_WIDTH = -(-SM_USED // LANES) * LANES
INDEX_SCALE = (H_I * D_I) ** -0.5
NEG_INF = float("-inf")
LOG2_E = 1.4426950408889634


F32_EXP_MASK = 0x7F800000
F32_MIN_NORMAL = 0x00800000
SEARCH_UNROLL = 4


def _key_to_float(u):
    key = u ^ jnp.int32(INT_MIN)
    bits = jnp.where(key < 0, key ^ jnp.int32(0x7FFFFFFF), key)
    below_neg_inf = (bits < 0) & ((bits & jnp.int32(0x7FFFFFFF)) > jnp.int32(F32_EXP_MASK))
    return jnp.where(below_neg_inf, NEG_INF, lax.bitcast_convert_type(bits, F32))


def _positive_bits_to_float(bits):
    return jnp.where(bits < jnp.int32(F32_MIN_NORMAL), 0.0, lax.bitcast_convert_type(bits, F32))


def _count(mask):
    return jnp.sum(jnp.where(mask, 1.0, 0.0), axis=1, keepdims=True)


def _select_bias(score_scr, bias_scr, adm, s, n_sel):
    tq = score_scr.shape[0]

    def resolve_bit(i, t_u, cnt_t):
        cand_u = t_u | lax.shift_left(jnp.int32(1), 31 - i)
        cnt = _count(score_scr[:, :s] >= _key_to_float(cand_u))
        ok = cnt >= n_sel
        return jnp.where(ok, cand_u, t_u), jnp.where(ok, cnt, cnt_t)

    def unresolved(c):
        i, _, cnt_t = c
        return jnp.logical_and(i < 32, jnp.any(cnt_t != n_sel))

    def resolve_bits(c):
        i, t_u, cnt_t = c
        for b in range(SEARCH_UNROLL):
            t_u, cnt_t = resolve_bit(i + b, t_u, cnt_t)
        return i + SEARCH_UNROLL, t_u, cnt_t

    _, t_u, cnt_t = lax.while_loop(unresolved, resolve_bits,
                                   (jnp.int32(0), jnp.zeros((tq, 1), I32), jnp.full((tq, 1), float(s), F32)))
    t_f = _key_to_float(t_u)
    score = score_scr[:, :s]
    bias_scr[:, :s] = jnp.where((score >= t_f) & adm, 0.0, NEG_INF)

    @pl.when(jnp.any(cnt_t > n_sel))
    def _():
        above = score >= _key_to_float(t_u + 1)
        bucket = (score >= t_f) & jnp.logical_not(above)
        need = n_sel - _count(above)
        off = jnp.where(bucket, score - jnp.where(t_f == NEG_INF, 0.0, t_f), -1.0)

        def resolve_offset_bit(i, r_bits):
            cand = r_bits | lax.shift_left(jnp.int32(1), 30 - i)
            ok = _count(off >= _positive_bits_to_float(cand)) >= need
            return jnp.where(ok, cand, r_bits)

        ranked = jnp.any(off > 0.0)
        r_bits = lax.fori_loop(0, jnp.where(ranked, 31, 0), resolve_offset_bit,
                               jnp.full((tq, 1), jnp.where(ranked, 0, F32_MIN_NORMAL - 1), I32))
        above2 = off >= _positive_bits_to_float(r_bits + 1)
        tie = (off >= _positive_bits_to_float(r_bits)) & jnp.logical_not(above2)
        need2 = need - _count(above2)
        upper = (lax.broadcasted_iota(I32, (LANES, LANES), 0)
                 <= lax.broadcasted_iota(I32, (LANES, LANES), 1)).astype(BF16)
        carry = jnp.zeros((tq, 1), F32)
        for c in range(s // LANES):
            cs = slice(c * LANES, (c + 1) * LANES)
            rank = jnp.dot(jnp.where(tie[:, cs], 1.0, 0.0).astype(BF16), upper, preferred_element_type=F32) + carry
            sel = (above[:, cs] | above2[:, cs] | (tie[:, cs] & (rank <= need2))) & adm[:, cs]
            bias_scr[:, cs] = jnp.where(sel, 0.0, NEG_INF)
            carry = rank[:, LANES - 1:LANES]


def _dsa_tile(qb_ref, sm_ref, *refs, segs, search, n_sel):
    key_refs, (lim_ref, o_ref, score_scr, bias_scr) = refs[:3 * len(segs)], refs[3 * len(segs):]
    tq = qb_ref.shape[0]
    s = sum(segs)
    nt = (((1,), (1,)), ((), ()))
    adm = lax.broadcasted_iota(I32, (tq, s), 1) < lim_ref[...]
    seg_keys = [[key_refs[3 * g + j][0, :sg, :].astype(BF16) for j in range(3)] for g, sg in enumerate(segs)]
    starts = [sum(segs[:g]) for g in range(len(segs))]
    against_keys = lambda lhs, j: jnp.concatenate(
        [lax.dot_general(lhs, kv[j], nt, preferred_element_type=F32) for kv in seg_keys], axis=1)
    times_values = lambda p: sum(jnp.dot(p[:, st:st + sg], kv[1], preferred_element_type=F32)
                                 for st, sg, kv in zip(starts, segs, seg_keys))
    stack = tq * max(H_I, H_B) <= MXU_ROWS
    head_rows = lambda a, h: a[h * tq:(h + 1) * tq]
    qi_head = lambda h: sm_ref[:, SM_QI + h * D_I:SM_QI + (h + 1) * D_I].astype(BF16)
    q_head = lambda h: qb_ref[:, h * DH_B:(h + 1) * DH_B]

    if search:
        if stack:
            stacked = against_keys(jnp.concatenate([qi_head(h) for h in range(H_I)], axis=0), 2)
        score = jnp.zeros((tq, s), F32)
        for h in range(H_I):
            sh = head_rows(stacked, h) if stack else against_keys(qi_head(h), 2)
            wih = sm_ref[:, SM_WI + h:SM_WI + h + 1] * INDEX_SCALE
            score = score + jnp.maximum(sh, 0.0) * wih
        score_scr[:, :s] = jnp.where(adm, score, NEG_INF)
        _select_bias(score_scr, bias_scr, adm, s, n_sel)
    else:
        bias_scr[:, :s] = jnp.where(adm, 0.0, NEG_INF)

    def softmax_terms(logits):
        p = jnp.exp2((logits - jnp.max(logits, axis=1, keepdims=True)) * (DH_B ** -0.5 * LOG2_E))
        return p.astype(BF16), jnp.sum(p, axis=1, keepdims=True)

    if stack:
        stacked = against_keys(jnp.concatenate([q_head(h) for h in range(H_B)], axis=0), 0)
        terms = [softmax_terms(head_rows(stacked, h) + bias_scr[:, :s]) for h in range(H_B)]
        o_all = times_values(jnp.concatenate([p for p, _ in terms], axis=0))
        for h in range(H_B):
            o_ref[:, h * DH_B:(h + 1) * DH_B] = (head_rows(o_all, h) / terms[h][1]).astype(BF16)
    else:
        for h in range(H_B):
            p, denom = softmax_terms(against_keys(q_head(h), 0) + bias_scr[:, :s])
            o_ref[:, h * DH_B:(h + 1) * DH_B] = (times_values(p) / denom).astype(BF16)


DSA_KEY_GRAN = 256


def _dsa(qb, sm, key_segments, limits, *, bsz, seq, row0, tq, n_sel):
    d = qb.shape[1]
    seg_rows = [seg[0].shape[1] for seg in key_segments]
    n_keys = sum(seg_rows)
    nt = seq // tq
    blk0 = row0 // tq
    limits = np.asarray(limits, np.int32)
    tile_max = limits.reshape(nt, tq).max(axis=1)
    per_tile = [(int(min(n_keys, -(-m // DSA_KEY_GRAN) * DSA_KEY_GRAN)), bool(m > n_sel)) for m in tile_max]
    lim2d = jnp.asarray(limits.reshape(seq, 1))
    outs, lo = [], 0
    while lo < nt:
        hi = lo
        while hi + 1 < nt and per_tile[hi + 1] == per_tile[lo]:
            hi += 1
        s, search = per_tile[lo]
        assert s == n_keys or len(key_segments) == 1
        segs = tuple(seg_rows) if s == n_keys else (s,)
        ntv = hi - lo + 1
        row = lambda w, lo=lo: pl.BlockSpec((tq, w), lambda b, j: (blk0 + b * nt + lo + j, 0))
        keys = lambda rows, w: pl.BlockSpec((1, rows, w), lambda b, j: (b, 0, 0))
        out = pl.pallas_call(
            functools.partial(_dsa_tile, segs=segs, search=search, n_sel=n_sel),
            grid=(bsz, ntv),
            in_specs=[row(d), row(sm.shape[1])]
            + [keys(rows, a.shape[2]) for rows, seg in zip(segs, key_segments) for a in seg]
            + [pl.BlockSpec((tq, 1), lambda b, j, lo=lo: (lo + j, 0))],
            out_specs=pl.BlockSpec((tq, d), lambda b, j, ntv=ntv: (b * ntv + j, 0)),
            out_shape=jax.ShapeDtypeStruct((bsz * ntv * tq, d), BF16),
            scratch_shapes=[pltpu.VMEM((tq, s), F32), pltpu.VMEM((tq, s), F32)],
            compiler_params=_cparams(("arbitrary", "arbitrary")),
            name="dsa",
        )(qb, sm, *[a for seg in key_segments for a in seg], lim2d)
        outs.append(out.reshape(bsz, ntv * tq, d))
        lo = hi + 1
    return jnp.concatenate(outs, axis=1).reshape(bsz * seq, d)


def _split_bf16(a):
    hi = a.astype(BF16)
    return hi, (a - hi.astype(F32)).astype(BF16)


def _mix_kernel(oap_ref, oas_ref, obp_ref, obs_ref, xp_ref, xs_ref, woa_ref, wob_ref, wga_ref, wgb_ref, wout_ref,
                g1_ref, b1_ref, wrh_ref, wrl_ref, br_ref, x1_ref, eidx_ref, gate_ref, cnt_ref, *, alpha, nt_p):
    i = pl.program_id(0)
    x = _group_tile(xp_ref, xs_ref, nt_p)
    xb = x.astype(BF16)
    dot = lambda a, w_ref: jnp.dot(a, w_ref[...], preferred_element_type=F32)
    y_a = dot(_group_tile(oap_ref, oas_ref, nt_p), woa_ref)
    y_b = dot(_group_tile(obp_ref, obs_ref, nt_p), wob_ref)
    merged = _sigmoid(dot(xb, wga_ref)) * y_a + _sigmoid(dot(xb, wgb_ref)) * y_b
    mixed = dot(merged.astype(BF16), wout_ref)
    x1 = _layer_norm(alpha * x + mixed, g1_ref[...], b1_ref[...])
    x1_ref[...] = x1

    nt = (((1,), (1,)), ((), ()))
    xh, xl = _split_bf16(x1)
    wh, wl = wrh_ref[...], wrl_ref[...]
    logits = (lax.dot_general(wh, xh, nt, preferred_element_type=F32)
              + lax.dot_general(wh, xl, nt, preferred_element_type=F32)
              + lax.dot_general(wl, xh, nt, preferred_element_type=F32)) + br_ref[...]
    ne, tm = logits.shape
    erow = lax.broadcasted_iota(I32, (ne, tm), 0)
    vals, idxs = [], []
    for _ in range(TOP_K):
        m = jnp.max(logits, axis=0, keepdims=True)
        idx = jnp.min(jnp.where(logits == m, erow, ne), axis=0, keepdims=True)
        vals.append(m)
        idxs.append(idx)
        logits = jnp.where(erow == idx, NEG_INF, logits)
    ex = [jnp.exp(v - vals[0]) for v in vals]
    denom = ex[0] + ex[1] + ex[2] + ex[3]
    pad = SUBLANES - TOP_K
    eidx = jnp.concatenate(idxs + [jnp.zeros((pad, tm), I32)], axis=0)
    eidx_ref[...] = eidx
    gate_ref[...] = jnp.concatenate([e / denom for e in ex] + [jnp.zeros((pad, tm), F32)], axis=0)

    onehot = jnp.zeros((ne, tm), F32)
    for idx in idxs:
        onehot = onehot + jnp.where(erow == idx, 1.0, 0.0)
    tile_cnt = jnp.broadcast_to(jnp.sum(onehot, axis=1, keepdims=True), cnt_ref.shape)

    @pl.when(i == 0)
    def _():
        cnt_ref[...] = tile_cnt

    @pl.when(i > 0)
    def _():
        cnt_ref[...] = cnt_ref[...] + tile_cnt


def _mix(oa, ob, x, woa, wob, wga, wgb, wout, g1, b1, wrh, wrl, br, *, tm, alpha):
    d = x[0].shape[1]
    n = x[0].shape[0] + x[1].shape[0]
    nt_p = x[0].shape[0] // tm
    ne = wrh.shape[0]
    row = lambda w: pl.BlockSpec((tm, w), lambda i: (i, 0))
    col = pl.BlockSpec((SUBLANES, tm), lambda i: (0, i))
    full = lambda a: pl.BlockSpec(a.shape, lambda i: (0,) * a.ndim)
    return pl.pallas_call(
        functools.partial(_mix_kernel, alpha=alpha, nt_p=nt_p),
        grid=(n // tm,),
        in_specs=[*_two_group_specs(tm, d, nt_p)] * 3
        + [full(a) for a in (woa, wob, wga, wgb, wout, g1, b1, wrh, wrl, br)],
        out_specs=[row(d), col, col, pl.BlockSpec((ne, LANES), lambda i: (0, 0))],
        out_shape=[jax.ShapeDtypeStruct((n, d), F32), jax.ShapeDtypeStruct((SUBLANES, n), I32),
                   jax.ShapeDtypeStruct((SUBLANES, n), F32), jax.ShapeDtypeStruct((ne, LANES), F32)],
        compiler_params=_cparams(("arbitrary",)),
        name="mix",
    )(*oa, *ob, *x, woa, wob, wga, wgb, wout, g1, b1, wrh, wrl, br)


def _sublane_cumsum(a):
    n = a.shape[0]
    row = lax.broadcasted_iota(I32, a.shape, 0)
    s = 1
    while s < n:
        a = a + jnp.where(row >= s, pltpu.roll(a, s, 0), 0.0)
        s *= 2
    return a


def _route_kernel(eidx_ref, cnt_ref, dest_ref, blke_ref, meta_ref, carry_scr, *, bm, sub):
    i = pl.program_id(0)
    ne = cnt_ref.shape[0]
    tl = eidx_ref.shape[1]

    @pl.when(i == 0)
    def _():
        carry_scr[...] = jnp.zeros_like(carry_scr)

    counts = cnt_ref[...]
    padded = jnp.ceil(counts / bm) * bm
    pend = _sublane_cumsum(padded)
    pstart = (pend - padded)[:, 0:1]

    @pl.when(i == 0)
    def _():
        nb = blke_ref.shape[1]
        first_row = (lax.broadcasted_iota(I32, (ne, nb), 1) * bm).astype(F32)
        below = jnp.sum(jnp.where(pend[:, 0:1] <= first_row, 1.0, 0.0), axis=0, keepdims=True)
        blke_ref[...] = jnp.minimum(below, ne - 1.0).astype(I32)
        pend_lanes = jnp.concatenate([pend, jnp.zeros((LANES - ne, LANES), F32)], axis=0).T[0:1, :]
        n_used = jnp.max(pend, axis=0, keepdims=True) / bm
        meta_ref[...] = jnp.concatenate([pend_lanes, n_used, jnp.zeros((SUBLANES - 2, LANES), F32)],
                                        axis=0).astype(I32)

    eidx = eidx_ref[...]
    erow = lax.broadcasted_iota(I32, (ne, tl), 0)
    hot = [jnp.where(erow == eidx[k:k + 1, :], 1.0, 0.0) for k in range(TOP_K)]
    onehot = hot[0] + hot[1] + hot[2] + hot[3]
    before = (lax.broadcasted_iota(I32, (tl, tl), 0) < lax.broadcasted_iota(I32, (tl, tl), 1)).astype(BF16)
    base = jnp.dot(onehot.astype(BF16), before, preferred_element_type=F32) + carry_scr[:, 0:1] + pstart
    dest = jnp.concatenate([jnp.sum(hk * base, axis=0, keepdims=True) for hk in hot], axis=0).astype(I32)
    for c in range(tl // sub):
        dest_ref[c] = dest[:, c * sub:(c + 1) * sub]
    carry_scr[...] = carry_scr[...] + jnp.sum(onehot, axis=1, keepdims=True)


def _route(eidx, counts, *, tl, sub, bm, nb):
    n = eidx.shape[1]
    ne = counts.shape[0]
    nb_pad = -(-nb // LANES) * LANES
    return pl.pallas_call(
        functools.partial(_route_kernel, bm=bm, sub=sub),
        grid=(n // tl,),
        in_specs=[pl.BlockSpec((SUBLANES, tl), lambda i: (0, i)), pl.BlockSpec(counts.shape, lambda i: (0, 0))],
        out_specs=[pl.BlockSpec((tl // sub, TOP_K, sub), lambda i: (i, 0, 0)),
                   pl.BlockSpec((1, nb_pad), lambda i: (0, 0)),
                   pl.BlockSpec((SUBLANES, LANES), lambda i: (0, 0))],
        out_shape=[jax.ShapeDtypeStruct((n // sub, TOP_K, sub), I32), jax.ShapeDtypeStruct((1, nb_pad), I32),
                   jax.ShapeDtypeStruct((SUBLANES, LANES), I32)],
        scratch_shapes=[pltpu.VMEM((ne, LANES), F32)],
        compiler_params=_cparams(("arbitrary",)),
        name="route",
    )(eidx, counts)


ROW_COPY_UNROLL = 8


def _dispatch_kernel(dest_ref, meta_ref, x_ref, xs_ref, zbuf, sem, zsem, *, bm, ne, nb):
    i = pl.program_id(0)
    tm = x_ref.shape[0]

    @pl.when(i == 0)
    def _():
        zbuf[...] = jnp.zeros_like(zbuf)
        zero_block = lambda row0: pltpu.make_async_copy(zbuf, xs_ref.at[pl.ds(row0, bm), :], zsem)

        def each_padding_block(action):
            for e in range(ne):
                seg_end = meta_ref[0, e]
                seg_start = meta_ref[0, e - 1] if e > 0 else 0

                @pl.when(seg_end > seg_start)
                def _(seg_end=seg_end):
                    action(zero_block(pl.multiple_of(seg_end - bm, bm)))

            def tail_block(b, carry):
                action(zero_block(pl.multiple_of(b * bm, bm)))
                return carry
            lax.fori_loop(meta_ref[1, 0], nb, tail_block, 0)

        each_padding_block(lambda copy: copy.start())
        each_padding_block(lambda copy: copy.wait())

    def start(t, carry):
        for k in range(TOP_K):
            pltpu.make_async_copy(x_ref.at[pl.ds(t, 1), :], xs_ref.at[pl.ds(dest_ref[0, k, t], 1), :], sem).start()
        return carry
    lax.fori_loop(0, tm, start, 0, unroll=ROW_COPY_UNROLL)

    for _ in range(TOP_K):
        pltpu.make_async_copy(x_ref, xs_ref.at[pl.ds(0, tm), :], sem).wait()


def _dispatch(dest, meta, x1, n_rows, *, tm, bm, ne):
    n, d = x1.shape
    return pl.pallas_call(
        functools.partial(_dispatch_kernel, bm=bm, ne=ne, nb=n_rows // bm),
        grid=(n // tm,),
        in_specs=[pl.BlockSpec((1, TOP_K, tm), lambda i: (i, 0, 0), memory_space=pltpu.SMEM),
                  pl.BlockSpec(meta.shape, lambda i: (0, 0), memory_space=pltpu.SMEM),
                  pl.BlockSpec((tm, d), lambda i: (i, 0))],
        out_specs=pl.BlockSpec(memory_space=pl.ANY),
        out_shape=jax.ShapeDtypeStruct((n_rows, d), x1.dtype),
        scratch_shapes=[pltpu.VMEM((bm, d), x1.dtype), pltpu.SemaphoreType.DMA(()), pltpu.SemaphoreType.DMA(())],
        compiler_params=_cparams(("arbitrary",)),
        name="dispatch",
    )(dest, meta, x1)


def _experts_kernel(blke_ref, nused_ref, segend_ref, xs_ref, wu_ref, bu_ref, wd_ref, bd_ref, ys_ref,
                    wu32, wd32, wu16, wd16, sems):
    i = pl.program_id(0)
    bm = xs_ref.shape[0]
    dff = wd16.shape[0]
    n_used = nused_ref[0]

    def weight_copies(e):
        return (pltpu.make_async_copy(wu_ref.at[e], wu32, sems.at[0]),
                pltpu.make_async_copy(wd_ref.at[e], wd32, sems.at[1]))

    @pl.when(i < n_used)
    def _():
        e_cur = blke_ref[i]

        @pl.when(jnp.logical_or(i == 0, e_cur != blke_ref[jnp.maximum(i - 1, 0)]))
        def _():
            @pl.when(i == 0)
            def _():
                for copy in weight_copies(e_cur):
                    copy.start()

            for copy in weight_copies(e_cur):
                copy.wait()
            wu16[...] = wu32[...].astype(BF16)
            wd16[...] = wd32[...].astype(BF16)

            nxt = segend_ref[e_cur] // bm

            @pl.when(nxt < n_used)
            def _():
                for copy in weight_copies(blke_ref[jnp.minimum(nxt, blke_ref.shape[0] - 1)]):
                    copy.start()

        h = jnp.dot(xs_ref[...].astype(BF16), wu16[...], preferred_element_type=F32) + bu_ref[0]
        glu = jnp.minimum(h[:, :dff], SWIGLU_LIMIT)
        lin = jnp.clip(h[:, dff:], -SWIGLU_LIMIT, SWIGLU_LIMIT)
        act = glu * _sigmoid(SWIGLU_ALPHA * glu) * (lin + 1.0)
        ys_ref[...] = jnp.dot(act.astype(BF16), wd16[...], preferred_element_type=F32) + bd_ref[0]

    @pl.when(i >= nused_ref[0])
    def _():
        ys_ref[...] = jnp.zeros_like(ys_ref)


def _experts(blk_e, n_used, seg_end, xs, wu, bu, wd, bd, *, bm):
    n_rows, d = xs.shape
    ne, _, dff2 = wu.shape
    dff = wd.shape[1]
    used = lambda i, nu: jnp.minimum(i, nu[0] - 1)
    grid_spec = pltpu.PrefetchScalarGridSpec(
        num_scalar_prefetch=3,
        grid=(n_rows // bm,),
        in_specs=[pl.BlockSpec((bm, d), lambda i, e, nu, se: (used(i, nu), 0)),
                  pl.BlockSpec(memory_space=pl.ANY),
                  pl.BlockSpec((1, 1, dff2), lambda i, e, nu, se: (e[i], 0, 0)),
                  pl.BlockSpec(memory_space=pl.ANY),
                  pl.BlockSpec((1, 1, d), lambda i, e, nu, se: (e[i], 0, 0))],
        out_specs=pl.BlockSpec((bm, d), lambda i, e, nu, se: (i, 0)),
        scratch_shapes=[pltpu.VMEM((d, dff2), F32), pltpu.VMEM((dff, d), F32),
                        pltpu.VMEM((d, dff2), BF16), pltpu.VMEM((dff, d), BF16),
                        pltpu.SemaphoreType.DMA((2,))],
    )
    return pl.pallas_call(
        _experts_kernel,
        grid_spec=grid_spec,
        out_shape=jax.ShapeDtypeStruct((n_rows, d), F32),
        compiler_params=_cparams(("arbitrary",)),
        name="experts",
    )(blk_e, n_used, seg_end, xs, wu, bu.reshape(ne, 1, dff2), wd, bd.reshape(ne, 1, d))


def _combine_kernel(dest_ref, ys_ref, gate_ref, x1_ref, g2_ref, b2_ref, outp_ref, outs_ref, buf, sem, *, alpha, nt_p):
    i = pl.program_id(0)
    tm = x1_ref.shape[0]

    def start(t, carry):
        for k in range(TOP_K):
            pltpu.make_async_copy(ys_ref.at[pl.ds(dest_ref[0, k, t], 1), :], buf.at[k, pl.ds(t, 1), :], sem).start()
        return carry
    lax.fori_loop(0, tm, start, 0, unroll=ROW_COPY_UNROLL)

    for k in range(TOP_K):
        pltpu.make_async_copy(ys_ref.at[pl.ds(0, tm), :], buf.at[k], sem).wait()

    gate = gate_ref[...].T
    y = buf[0] * gate[:, 0:1]
    for k in range(1, TOP_K):
        y = y + buf[k] * gate[:, k:k + 1]
    out = _layer_norm(alpha * x1_ref[...] + y, g2_ref[...], b2_ref[...])

    @pl.when(i < nt_p)
    def _():
        outp_ref[...] = out

    @pl.when(i >= nt_p)
    def _():
        outs_ref[...] = out


def _combine(dest, ys, gate, x1, g2, b2, *, n_p, tm, alpha):
    n, d = x1.shape
    nt_p = n_p // tm
    return pl.pallas_call(
        functools.partial(_combine_kernel, alpha=alpha, nt_p=nt_p),
        grid=(n // tm,),
        in_specs=[pl.BlockSpec((1, TOP_K, tm), lambda i: (i, 0, 0), memory_space=pltpu.SMEM),
                  pl.BlockSpec(memory_space=pl.ANY),
                  pl.BlockSpec((SUBLANES, tm), lambda i: (0, i)),
                  pl.BlockSpec((tm, d), lambda i: (i, 0)),
                  pl.BlockSpec((1, d), lambda i: (0, 0)),
                  pl.BlockSpec((1, d), lambda i: (0, 0))],
        out_specs=[pl.BlockSpec((tm, d), lambda i: (jnp.minimum(i, nt_p - 1), 0)),
                   pl.BlockSpec((tm, d), lambda i: (jnp.maximum(i - nt_p, 0), 0))],
        out_shape=[jax.ShapeDtypeStruct((n_p, d), F32), jax.ShapeDtypeStruct((n - n_p, d), F32)],
        scratch_shapes=[pltpu.VMEM((TOP_K, tm, d), F32), pltpu.SemaphoreType.DMA(())],
        compiler_params=_cparams(("arbitrary",)),
        name="combine",
    )(dest, ys, gate, x1, g2, b2)


PROJ_TM = 256
HGRN_TL = 256
DSA_TQ = 256
MIX_TM = 256
ROUTE_TL = 1280
MOE_TM = 256
MOE_BM = 512


def kernel(x_prompt, x_sample, cache_k, cache_v, cache_kidx, state_hgrn, w_in, lb_logits, gn_a, w_oa, w_ob,
           w_out, ln1_g, ln1_b, w_router, b_router, w_up, b_up, w_down, b_down, ln2_g, ln2_b):
    depth = w_in.shape[0]
    assert depth == 1
    bsz, seq, d = x_prompt.shape
    dbsz, dseq, _ = x_sample.shape
    past = cache_k.shape[2]
    n_p, n_s = bsz * seq, dbsz * dseq
    n = n_p + n_s
    alpha = (2 * depth) ** 0.25

    sizes = (H_A * DK_A, H_A * DK_A, H_A * DV_A, H_A * DV_A, H_B * DH_B, DH_B, DH_B, H_I * D_I, H_I, D_I, d, d)
    offs = np.concatenate([[0], np.cumsum(sizes)])
    grp = lambda g: w_in[0, :, offs[g]:offs[g + 1]].astype(BF16)
    wqa, wfa, wia, wg, wqb, wkb, wvb, wqi, wwi, wki, wgta, wgtb = (grp(g) for g in range(12))
    wsm = jnp.concatenate([wkb, wvb, wqi, wki, wwi, jnp.zeros((d, SM_WIDTH - SM_USED), BF16)], axis=1)

    x_groups = (x_prompt.reshape(n_p, d), x_sample.reshape(n_s, d))
    *rec, qb, sm = _proj(*x_groups, wqa, wfa, wia, wg, wqb, wsm, lb_logits, PROJ_TM)

    gn = gn_a[0].reshape(1, DV_A)
    oa_p, sfin_p = _hgrn(*rec, gn, jnp.zeros((bsz, H_A, DK_A, DV_A), F32),
                         bsz=bsz, seq=seq, row0=0, tl=HGRN_TL)
    oa_s, sfin_s = _hgrn(*rec, gn, state_hgrn[0], bsz=dbsz, seq=dseq, row0=n_p, tl=dseq)

    kb_p = sm[:n_p, SM_KB:SM_KB + DH_B].reshape(bsz, seq, DH_B)
    vb_p = sm[:n_p, SM_VB:SM_VB + DH_B].reshape(bsz, seq, DH_B)
    ki_p = sm[:n_p, SM_KI:SM_KI + D_I].reshape(bsz, seq, D_I)
    kb_s = sm[n_p:, SM_KB:SM_KB + DH_B].reshape(dbsz, dseq, DH_B)
    vb_s = sm[n_p:, SM_VB:SM_VB + DH_B].reshape(dbsz, dseq, DH_B)
    ki_s = sm[n_p:, SM_KI:SM_KI + D_I].reshape(dbsz, dseq, D_I)
    lim_p = (np.arange(seq) // CHUNK + 1) * CHUNK
    ob_p = _dsa(qb, sm, [(kb_p.astype(BF16), vb_p.astype(BF16), ki_p.astype(BF16))], lim_p,
                bsz=bsz, seq=seq, row0=0, tq=DSA_TQ, n_sel=min(TOPK_MAX, seq // 4))
    n_keys = past + dseq
    new_pad = -(-dseq // LANES) * LANES - dseq
    new_keys = tuple(jnp.pad(a.astype(BF16), ((0, 0), (0, new_pad), (0, 0))) for a in (kb_s, vb_s, ki_s))
    lim_s = np.full((dseq,), n_keys)
    ob_s = _dsa(qb, sm, [(cache_k[0], cache_v[0], cache_kidx[0]), new_keys], lim_s,
                bsz=dbsz, seq=dseq, row0=n_p, tq=dseq, n_sel=min(TOPK_MAX, n_keys // 4))

    wrh, wrl = _split_bf16(w_router[0].T)
    x1, eidx, gate, counts = _mix(
        (oa_p, oa_s), (ob_p, ob_s), x_groups, w_oa[0].astype(BF16), w_ob[0].astype(BF16), wgta, wgtb, w_out[0].astype(BF16),
        ln1_g[0].reshape(1, d), ln1_b[0].reshape(1, d), wrh, wrl, b_router[0].reshape(N_EXPERTS, 1),
        tm=MIX_TM, alpha=alpha)

    n_rows = -(-(n * TOP_K + N_EXPERTS * (MOE_BM - 1)) // MOE_BM) * MOE_BM
    dest, blk_e, meta = _route(eidx, counts, tl=ROUTE_TL, sub=MOE_TM, bm=MOE_BM, nb=n_rows // MOE_BM)
    xs = _dispatch(dest, meta, x1, n_rows, tm=MOE_TM, bm=MOE_BM, ne=N_EXPERTS)
    ys = _experts(blk_e[0, :n_rows // MOE_BM], meta[1, :1], meta[0, :N_EXPERTS], xs, w_up[0], b_up[0], w_down[0],
                  b_down[0], bm=MOE_BM)
    out_p, out_s = _combine(dest, ys, gate, x1, ln2_g[0].reshape(1, d), ln2_b[0].reshape(1, d),
                            n_p=n_p, tm=MOE_TM, alpha=alpha)

    return (out_p.reshape(bsz, seq, d), out_s.reshape(dbsz, dseq, d),
            kb_p[None], vb_p[None], ki_p[None], sfin_p[None],
            kb_s[None], vb_s[None], ki_s[None], sfin_s[None])
```
